```python
import jax, jax.numpy as jnp
from jax import lax
import numpy as np

D_MODEL = 2048
BATCH = 2
SEQ = 4096
DEPTH = 1

HEAD_DIM = 128
HEADS_PER_GROUP = 8
DILATED_GROUPS = ((128, 1), (512, 4), (2048, 16))
N_ATTN_GROUPS = len(DILATED_GROUPS)
ATTN_QKV = N_ATTN_GROUPS * HEADS_PER_GROUP * HEAD_DIM
ATTN_OUT = HEADS_PER_GROUP * HEAD_DIM
BLK = 128
ROPE_THETA = 500000.0
ROT_DIM = HEAD_DIM // 4
POOL_SIZES = (2, 4, 8, 16)
POOL_WIDTH = D_MODEL // 2
POOL_GROUP = POOL_WIDTH // len(POOL_SIZES)
IN_COLS = 3 * ATTN_QKV + ATTN_OUT + 2 * POOL_WIDTH + 2 * D_MODEL
NORM_EPS = 1e-6

kernel_name = "hybrid_dilated_attn_pool_gated_merge"


def rms_norm(x, g):
    xf = x.astype(jnp.float32)
    y = xf * lax.rsqrt(jnp.mean(xf * xf, axis=-1, keepdims=True) + NORM_EPS)
    return (y * g.astype(jnp.float32)).astype(x.dtype)


def partial_rope(t, pos):
    half = ROT_DIM // 2
    inv_freq = ROPE_THETA ** (-jnp.arange(0, ROT_DIM, 2, dtype=jnp.float32) / ROT_DIM)
    ang = pos.astype(jnp.float32)[:, None] * inv_freq[None, :]
    cos = jnp.concatenate([jnp.cos(ang), jnp.cos(ang)], axis=-1)[None, :, None, :]
    sin = jnp.concatenate([jnp.sin(ang), jnp.sin(ang)], axis=-1)[None, :, None, :]
    tr = t[..., :ROT_DIM].astype(jnp.float32)
    rot_half = jnp.concatenate([-tr[..., half:], tr[..., :half]], axis=-1)
    tr = (tr * cos + rot_half * sin).astype(t.dtype)
    return jnp.concatenate([tr, t[..., ROT_DIM:]], axis=-1)


def dilated_window_attention(q, k, v, window, dilation):
    B, S, H, C = q.shape
    w_sub = window // dilation
    L = S // dilation
    nb = -(-L // BLK)
    Lp = nb * BLK

    def to_blocks(t):
        t = t.reshape(B, L, dilation, H, C).transpose(0, 2, 1, 3, 4)
        t = jnp.pad(t, ((0, 0), (0, 0), (0, Lp - L), (0, 0), (0, 0)))
        return t.reshape(B, dilation, nb, BLK, H, C)

    def with_prev(t):
        prev = jnp.pad(t, ((0, 0), (0, 0), (1, 0), (0, 0), (0, 0), (0, 0)))[:, :, :nb]
        return jnp.concatenate([prev, t], axis=3)

    qb = to_blocks(q)
    kk = with_prev(to_blocks(k))
    vv = with_prev(to_blocks(v))
    s = jnp.einsum('brnqhc,brnkhc->brnhqk', qb, kk).astype(jnp.float32) * (C ** -0.5)
    blk = jnp.arange(nb)[:, None, None]
    qpos = blk * BLK + jnp.arange(BLK)[None, :, None]
    kpos = (blk - 1) * BLK + jnp.arange(2 * BLK)[None, None, :]
    dist = qpos - kpos
    mask = (kpos >= 0) & (dist >= 0) & (dist <= w_sub)
    s = jnp.where(mask[None, None, :, None], s, -jnp.inf)
    lse = jax.nn.logsumexp(s, axis=-1)
    p = jnp.exp(s - lse[..., None])
    o = jnp.einsum('brnhqk,brnkhc->brnqhc', p.astype(v.dtype), vv)
    o = o.reshape(B, dilation, Lp, H, C)[:, :, :L].transpose(0, 2, 1, 3, 4).reshape(B, S, H, C)
    lse = lse.transpose(0, 1, 2, 4, 3).reshape(B, dilation, Lp, H)[:, :, :L]
    lse = lse.transpose(0, 2, 1, 3).reshape(B, S, H)
    return o, lse


def causal_mean(u, k):
    S = u.shape[1]
    cs = jnp.cumsum(u.astype(jnp.float32), axis=1)
    prev = jnp.pad(cs, ((0, 0), (k, 0), (0, 0)))[:, :S]
    cnt = jnp.minimum(jnp.arange(S) + 1, k).astype(jnp.float32)
    return (cs - prev) / cnt[None, :, None]


def setup_inputs(seed: int = 0) -> dict:
    key = jax.random.key(seed)
    ks = jax.random.split(key, 12)
    f = jnp.float32
    x = jax.random.normal(ks[0], (BATCH, SEQ, D_MODEL), f)
    norm_gain = 1.0 + 0.02 * jax.random.normal(ks[1], (D_MODEL,), f)
    w_in = jax.random.normal(ks[2], (D_MODEL, IN_COLS), f) * D_MODEL ** -0.5
    b_gates = 0.02 * jax.random.normal(ks[3], (2 * D_MODEL,), f)
    q_norm_gain = 1.0 + 0.02 * jax.random.normal(ks[4], (HEAD_DIM,), f)
    k_norm_gain = 1.0 + 0.02 * jax.random.normal(ks[5], (HEAD_DIM,), f)
    pool_maps = jax.random.normal(ks[6], (len(POOL_SIZES), POOL_GROUP, POOL_GROUP), f) * POOL_GROUP ** -0.5
    pool_scale = 1.0 + 0.1 * jax.random.normal(ks[7], (POOL_WIDTH,), f)
    w_branch_attn = jax.random.normal(ks[8], (ATTN_OUT, D_MODEL), f) * ATTN_OUT ** -0.5
    w_branch_pool = jax.random.normal(ks[9], (POOL_WIDTH, D_MODEL), f) * POOL_WIDTH ** -0.5
    w_out = jax.random.normal(ks[10], (D_MODEL, D_MODEL), f) * D_MODEL ** -0.5
    return {"x": x, "norm_gain": norm_gain, "w_in": w_in, "b_gates": b_gates,
            "q_norm_gain": q_norm_gain, "k_norm_gain": k_norm_gain,
            "pool_maps": pool_maps, "pool_scale": pool_scale,
            "w_branch_attn": w_branch_attn, "w_branch_pool": w_branch_pool, "w_out": w_out}


def reference(x, norm_gain, w_in, b_gates, q_norm_gain, k_norm_gain, pool_maps, pool_scale,
              w_branch_attn, w_branch_pool, w_out):
    B, S, _ = x.shape
    pos = jnp.arange(S, dtype=jnp.int32)
    for _layer in range(DEPTH):
        h = rms_norm(x, norm_gain)
        proj = jnp.einsum('bsd,de->bse', h, w_in)
        splits = np.cumsum([ATTN_QKV, ATTN_QKV, ATTN_QKV, ATTN_OUT, POOL_WIDTH, POOL_WIDTH]).tolist()
        q, k, v, z_attn, u_pool, z_pool, gates = jnp.split(proj, splits, axis=-1)
        shp = (B, S, N_ATTN_GROUPS, HEADS_PER_GROUP, HEAD_DIM)
        q = rms_norm(q.reshape(shp), q_norm_gain)
        k = rms_norm(k.reshape(shp), k_norm_gain)
        v = v.reshape(shp)

        outs, lses = [], []
        for g, (window, dilation) in enumerate(DILATED_GROUPS):
            qg = partial_rope(q[:, :, g], pos)
            kg = partial_rope(k[:, :, g], pos)
            o_g, lse_g = dilated_window_attention(qg, kg, v[:, :, g], window, dilation)
            outs.append(o_g)
            lses.append(lse_g)
        mix_w = jax.nn.softmax(jnp.stack(lses, axis=0), axis=0)
        attn = jnp.sum(mix_w[..., None] * jnp.stack(outs, axis=0).astype(jnp.float32), axis=0)
        attn = attn.astype(x.dtype).reshape(B, S, ATTN_OUT)
        y_attn = jnp.einsum('bsc,cd->bsd', attn * jax.nn.silu(z_attn), w_branch_attn)

        pooled = []
        for g, ksz in enumerate(POOL_SIZES):
            u_g = u_pool[..., g * POOL_GROUP:(g + 1) * POOL_GROUP]
            d_g = (causal_mean(u_g, ksz) - u_g.astype(jnp.float32)).astype(x.dtype)
            pooled.append(jnp.einsum('bsc,ce->bse', d_g, pool_maps[g]))
        pool = jnp.concatenate(pooled, axis=-1) * pool_scale
        y_pool = jnp.einsum('bsc,cd->bsd', pool * jax.nn.silu(z_pool), w_branch_pool)

        gate = jax.nn.sigmoid((gates + b_gates).astype(jnp.float32)).astype(x.dtype)
        g_attn, g_pool = jnp.split(gate, 2, axis=-1)
        merged = g_attn * y_attn + g_pool * y_pool
        x = x + jnp.einsum('bsd,de->bse', merged, w_out)
    return x
```

```python
import functools

import jax
import jax.numpy as jnp
from jax import lax
from jax.experimental import pallas as pl
from jax.experimental.pallas import tpu as pltpu

D_MODEL = 2048
HEAD_DIM = 128
HEADS_PER_GROUP = 8
DILATED_GROUPS = ((128, 1), (512, 4), (2048, 16))
N_ATTN_GROUPS = len(DILATED_GROUPS)
ATTN_QKV = N_ATTN_GROUPS * HEADS_PER_GROUP * HEAD_DIM
ATTN_OUT = HEADS_PER_GROUP * HEAD_DIM
BLK = 128
ROPE_THETA = 500000.0
ROT_DIM = HEAD_DIM // 4
POOL_SIZES = (2, 4, 8, 16)
POOL_WIDTH = D_MODEL // 2
POOL_GROUP = POOL_WIDTH // len(POOL_SIZES)
NORM_EPS = 1e-6

LANES = 128
MXU_WIDTH = 256
VMEM_LIMIT_BYTES = 56 * 1024 * 1024

PROJ_TM = 1024
PROJ_TN = 1024
NORM_TM = 256
TAIL_TM = 256
POOL_HALO = 16

F32 = jnp.float32
BF16 = jnp.bfloat16


def _rmsnorm_kernel(x_ref, g_ref, h_ref):
    x = x_ref[...]
    ms = jnp.mean(x * x, axis=-1, keepdims=True)
    h_ref[...] = (x * lax.rsqrt(ms + NORM_EPS) * g_ref[...]).astype(h_ref.dtype)


def _rmsnorm(x2, gain):
    t, d = x2.shape
    return pl.pallas_call(
        _rmsnorm_kernel,
        grid=(t // NORM_TM,),
        in_specs=[pl.BlockSpec((NORM_TM, d), lambda i: (i, 0)),
                  pl.BlockSpec((1, d), lambda i: (0, 0))],
        out_specs=pl.BlockSpec((NORM_TM, d), lambda i: (i, 0)),
        out_shape=jax.ShapeDtypeStruct((t, d), BF16),
        compiler_params=pltpu.CompilerParams(dimension_semantics=("arbitrary",)),
        name="rmsnorm",
    )(x2, gain.reshape(1, d))


def _qk_epilogue(acc, cs, tab_ref):
    del cs
    outs = []
    for hh in range(acc.shape[1] // HEAD_DIM):
        a = acc[:, hh * HEAD_DIM:(hh + 1) * HEAD_DIM]
        ss = jnp.sum(a * a, axis=-1, keepdims=True)
        r = lax.rsqrt(ss * (1.0 / HEAD_DIM) + NORM_EPS)
        y = (a * tab_ref[0]
             + pltpu.roll(a, ROT_DIM // 2, 1) * tab_ref[1]
             + pltpu.roll(a, HEAD_DIM - ROT_DIM // 2, 1) * tab_ref[2])
        outs.append(y * r)
    return jnp.concatenate(outs, axis=1)


def _identity_epilogue(acc, cs):
    del cs
    return acc


def _rest_epilogue(acc, cs, coef_ref):
    c1 = coef_ref[0:1, cs]
    c2 = coef_ref[1:2, cs]
    c3 = coef_ref[2:3, cs]
    bias = coef_ref[3:4, cs]
    sig = 1.0 / (1.0 + jnp.exp(-(acc + bias)))
    return acc * c1 + sig * (acc * c2 + c3)


def _proj_kernel(*refs, epilogue, n_aux, tn):
    h_ref, w_ref = refs[0], refs[1]
    aux = refs[2:2 + n_aux]
    out_ref = refs[2 + n_aux]
    wbf_ref = refs[3 + n_aux]

    @pl.when(pl.program_id(1) == 0)
    def _():
        wbf_ref[...] = w_ref[...].astype(BF16)

    for c in range(tn // MXU_WIDTH):
        cs = slice(c * MXU_WIDTH, (c + 1) * MXU_WIDTH)
        acc = jnp.dot(h_ref[...], wbf_ref[:, cs], preferred_element_type=F32)
        out_ref[:, cs] = epilogue(acc, cs, *aux).astype(out_ref.dtype)


def _project(h, w_in, *, n_blocks, w_block_of, epilogue, out_dtype, aux=(), aux_specs=(), name):
    t, d = h.shape
    tm, tn = PROJ_TM, PROJ_TN
    kern = functools.partial(_proj_kernel, epilogue=epilogue, n_aux=len(aux), tn=tn)
    return pl.pallas_call(
        kern,
        grid=(n_blocks, t // tm),
        in_specs=[pl.BlockSpec((tm, d), lambda j, i: (i, 0)),
                  pl.BlockSpec((d, tn), lambda j, i: (0, w_block_of(j))),
                  *aux_specs],
        out_specs=pl.BlockSpec((tm, tn), lambda j, i: (i, j)),
        out_shape=jax.ShapeDtypeStruct((t, n_blocks * tn), out_dtype),
        scratch_shapes=[pltpu.VMEM((d, tn), BF16)],
        compiler_params=pltpu.CompilerParams(
            dimension_semantics=("arbitrary", "arbitrary"),
            vmem_limit_bytes=VMEM_LIMIT_BYTES),
        name=name,
    )(h, w_in, *aux)


def _rope_tables(seq, gain, scale):
    half = ROT_DIM // 2
    inv_freq = ROPE_THETA ** (-jnp.arange(0, ROT_DIM, 2, dtype=F32) / ROT_DIM)
    ang = jnp.arange(seq, dtype=jnp.int32).astype(F32)[:, None] * inv_freq[None, :]
    cos, sin = jnp.cos(ang), jnp.sin(ang)
    g = gain.astype(F32) * scale
    zeros = lambda n: jnp.zeros((seq, n), F32)
    t_self = jnp.concatenate(
        [cos * g[:half], cos * g[half:ROT_DIM],
         jnp.broadcast_to(g[ROT_DIM:], (seq, HEAD_DIM - ROT_DIM))], axis=1)
    t_up = jnp.concatenate([zeros(half), sin * g[:half], zeros(HEAD_DIM - ROT_DIM)], axis=1)
    t_down = jnp.concatenate([-sin * g[half:ROT_DIM], zeros(HEAD_DIM - half)], axis=1)
    return jnp.stack([t_self, t_up, t_down])


def _attn_kernel(q_ref, kp_ref, kc_ref, vp_ref, vc_ref, o_ref, lse_ref, *, qb):
    n = pl.program_id(2)
    row = lax.broadcasted_iota(jnp.int32, (BLK, 2 * BLK), 0)
    col = lax.broadcasted_iota(jnp.int32, (BLK, 2 * BLK), 1)
    band = (col >= row) & (col <= row + BLK)
    band_first = band & ((col >= BLK) | (n > 0))
    lane = lax.broadcasted_iota(jnp.int32, (BLK, LANES), 1)
    for b in range(qb // BLK):
        rs = slice(b * BLK, (b + 1) * BLK)
        lse_tile = jnp.zeros((BLK, LANES), F32)
        for h in range(HEADS_PER_GROUP):
            cs = slice(h * HEAD_DIM, (h + 1) * HEAD_DIM)
            q = q_ref[rs, cs]
            if b == 0:
                kk = jnp.concatenate([kp_ref[:, cs], kc_ref[0:BLK, cs]], axis=0)
                vv = jnp.concatenate([vp_ref[:, cs], vc_ref[0:BLK, cs]], axis=0)
                mask = band_first
            else:
                kk = kc_ref[(b - 1) * BLK:(b + 1) * BLK, cs]
                vv = vc_ref[(b - 1) * BLK:(b + 1) * BLK, cs]
                mask = band
            s = lax.dot_general(q, kk, (((1,), (1,)), ((), ())), preferred_element_type=F32)
            s = jnp.where(mask, s, -jnp.inf)
            m = jnp.max(s, axis=-1, keepdims=True)
            p = jnp.exp(s - m)
            l = jnp.sum(p, axis=-1, keepdims=True)
            o = jnp.dot(p.astype(BF16), vv, preferred_element_type=F32)
            o_ref[rs, cs] = (o / l).astype(o_ref.dtype)
            lse_tile = jnp.where(lane == h, m + jnp.log(l), lse_tile)
        lse_ref[rs, :] = lse_tile


def _attention_group(qk, rest, g, batch, seq, *, v_block0):
    _, d = DILATED_GROUPS[g]
    sub = seq // d
    qb = min(2 * BLK, sub)
    nb = sub // qb
    per = qb // BLK
    qk_blocks = qk.shape[1] // ATTN_OUT
    rest_blocks = rest.shape[1] // ATTN_OUT
    qk3 = qk.reshape(batch, sub, d * qk.shape[1])
    rest3 = rest.reshape(batch, sub, d * rest.shape[1])
    q_blk = g
    k_blk = N_ATTN_GROUPS + g
    v_blk = v_block0 + g

    def prev(n):
        return jnp.maximum(n * per - 1, 0)

    o, lse = pl.pallas_call(
        functools.partial(_attn_kernel, qb=qb),
        grid=(batch, d, nb),
        in_specs=[
            pl.BlockSpec((None, qb, ATTN_OUT), lambda b, r, n: (b, n, r * qk_blocks + q_blk)),
            pl.BlockSpec((None, BLK, ATTN_OUT), lambda b, r, n: (b, prev(n), r * qk_blocks + k_blk)),
            pl.BlockSpec((None, qb, ATTN_OUT), lambda b, r, n: (b, n, r * qk_blocks + k_blk)),
            pl.BlockSpec((None, BLK, ATTN_OUT), lambda b, r, n: (b, prev(n), r * rest_blocks + v_blk)),
            pl.BlockSpec((None, qb, ATTN_OUT), lambda b, r, n: (b, n, r * rest_blocks + v_blk)),
        ],
        out_specs=[
            pl.BlockSpec((None, qb, ATTN_OUT), lambda b, r, n: (b, n, r)),
            pl.BlockSpec((None, qb, LANES), lambda b, r, n: (b, n, r)),
        ],
        out_shape=[jax.ShapeDtypeStruct((batch, sub, d * ATTN_OUT), BF16),
                   jax.ShapeDtypeStruct((batch, sub, d * LANES), F32)],
        compiler_params=pltpu.CompilerParams(
            dimension_semantics=("arbitrary", "arbitrary", "arbitrary"),
            vmem_limit_bytes=VMEM_LIMIT_BYTES),
        name=f"dilated_attention_g{g}",
    )(qk3, qk3, qk3, rest3, rest3)
    t = batch * seq
    return o.reshape(t, ATTN_OUT), lse.reshape(t, LANES)


def _tail_kernel(o0_ref, o1_ref, o2_ref, l0_ref, l1_ref, l2_ref, ga_ref, gp_ref, za_ref, zp_ref,
                 u_ref, uh_ref, x_ref, wba_ref, wbp_ref, pm_ref, ps_ref, wo_ref, out_ref, ubuf_ref,
                 *, tm, seq):
    pos0 = (pl.program_id(0) * tm) % seq

    l0, l1, l2 = l0_ref[...], l1_ref[...], l2_ref[...]
    mx = jnp.maximum(jnp.maximum(l0, l1), l2)
    e0, e1, e2 = jnp.exp(l0 - mx), jnp.exp(l1 - mx), jnp.exp(l2 - mx)
    inv = 1.0 / (e0 + e1 + e2)
    w0, w1, w2 = e0 * inv, e1 * inv, e2 * inv
    parts = []
    for h in range(HEADS_PER_GROUP):
        cs = slice(h * HEAD_DIM, (h + 1) * HEAD_DIM)
        a = (w0[:, h:h + 1] * o0_ref[:, cs].astype(F32)
             + w1[:, h:h + 1] * o1_ref[:, cs].astype(F32)
             + w2[:, h:h + 1] * o2_ref[:, cs].astype(F32))
        parts.append((a * za_ref[:, cs].astype(F32)).astype(BF16))
    y_attn = jnp.dot(jnp.concatenate(parts, axis=1), wba_ref[...], preferred_element_type=F32)

    halo = uh_ref[...]
    ubuf_ref[0:POOL_HALO, :] = jnp.where(pos0 == 0, jnp.zeros_like(halo), halo)
    ubuf_ref[POOL_HALO:POOL_HALO + tm, :] = u_ref[...]
    pos = pos0 + lax.broadcasted_iota(jnp.int32, (tm, 1), 0)
    pooled = []
    for g, ksz in enumerate(POOL_SIZES):
        cs = slice(g * POOL_GROUP, (g + 1) * POOL_GROUP)
        u = ubuf_ref[POOL_HALO:POOL_HALO + tm, cs]
        win = u
        for s in range(1, ksz):
            win = win + ubuf_ref[POOL_HALO - s:POOL_HALO - s + tm, cs]
        cnt = jnp.minimum(pos + 1, ksz).astype(F32)
        dlt = win / cnt - u
        pooled.append(jnp.dot(dlt.astype(BF16), pm_ref[g], preferred_element_type=F32))
    pool = jnp.concatenate(pooled, axis=1) * ps_ref[...]
    y_pool = jnp.dot((pool * zp_ref[...].astype(F32)).astype(BF16), wbp_ref[...],
                     preferred_element_type=F32)

    merged = ga_ref[...].astype(F32) * y_attn + gp_ref[...].astype(F32) * y_pool
    out_ref[...] = x_ref[...] + jnp.dot(merged.astype(BF16), wo_ref[...], preferred_element_type=F32)


def _tail(os_, lses, rest, u, x2, wba, wbp, pm, ps, wo, seq, *, gate_block0, za_block, zp_block):
    t, d = x2.shape
    tm = TAIL_TM
    halo_per_tile = tm // POOL_HALO
    resident = functools.partial(pl.BlockSpec, pipeline_mode=pl.Buffered(1))
    row = lambda blk: (lambda i: (i, blk))
    gate_blocks = D_MODEL // ATTN_OUT
    in_specs = (
        [pl.BlockSpec((tm, ATTN_OUT), row(0))] * 3
        + [pl.BlockSpec((tm, LANES), row(0))] * 3
        + [pl.BlockSpec((tm, D_MODEL), row(gate_block0 // gate_blocks)),
           pl.BlockSpec((tm, D_MODEL), row(gate_block0 // gate_blocks + 1)),
           pl.BlockSpec((tm, ATTN_OUT), row(za_block)),
           pl.BlockSpec((tm, POOL_WIDTH), row(zp_block)),
           pl.BlockSpec((tm, POOL_WIDTH), row(0)),
           pl.BlockSpec((POOL_HALO, POOL_WIDTH),
                        lambda i: (jnp.maximum(i * halo_per_tile - 1, 0), 0)),
           pl.BlockSpec((tm, d), row(0)),
           resident(wba.shape, lambda i: (0, 0)),
           resident(wbp.shape, lambda i: (0, 0)),
           resident(pm.shape, lambda i: (0, 0, 0)),
           resident(ps.shape, lambda i: (0, 0)),
           resident(wo.shape, lambda i: (0, 0))])
    return pl.pallas_call(
        functools.partial(_tail_kernel, tm=tm, seq=seq),
        grid=(t // tm,),
        in_specs=in_specs,
        out_specs=pl.BlockSpec((tm, d), row(0)),
        out_shape=jax.ShapeDtypeStruct((t, d), F32),
        scratch_shapes=[pltpu.VMEM((POOL_HALO + tm, POOL_WIDTH), F32)],
        compiler_params=pltpu.CompilerParams(
            dimension_semantics=("arbitrary",),
            vmem_limit_bytes=VMEM_LIMIT_BYTES),
        name="merge_pool_out",
    )(*os_, *lses, rest, rest, rest, rest, u, u, x2, wba, wbp, pm, ps, wo)


def kernel(x, norm_gain, w_in, b_gates, q_norm_gain, k_norm_gain, pool_maps, pool_scale,
           w_branch_attn, w_branch_pool, w_out):
    batch, seq, d = x.shape
    t = batch * seq
    x2 = x.reshape(t, d)
    h = _rmsnorm(x2, norm_gain)

    tabs = jnp.stack([_rope_tables(seq, q_norm_gain, HEAD_DIM ** -0.5),
                      _rope_tables(seq, k_norm_gain, 1.0)])
    seq_tiles = seq // PROJ_TM
    qk = _project(
        h, w_in, n_blocks=6, w_block_of=lambda j: j, epilogue=_qk_epilogue, out_dtype=BF16,
        aux=(tabs,),
        aux_specs=(pl.BlockSpec((None, 3, PROJ_TM, HEAD_DIM),
                                lambda j, i: (j // N_ATTN_GROUPS, 0, i % seq_tiles, 0)),),
        name="proj_qk")
    u = _project(h, w_in, n_blocks=1, w_block_of=lambda j: j + 10, epilogue=_identity_epilogue,
                 out_dtype=F32, name="proj_u")

    n_gate = 2 * D_MODEL
    n_rest = n_gate + ATTN_QKV + ATTN_OUT + POOL_WIDTH
    ones = lambda a, b: jnp.zeros((n_rest,), F32).at[a:b].set(1.0)
    coef = jnp.stack([ones(n_gate, n_gate + ATTN_QKV),
                      ones(n_gate + ATTN_QKV, n_rest),
                      ones(0, n_gate),
                      jnp.zeros((n_rest,), F32).at[:n_gate].set(b_gates.astype(F32))])
    rest_w_block = lambda j: jnp.where(j < 4, j + 12, jnp.where(j < 7, j + 2, jnp.where(j == 7, 9, 11)))
    rest = _project(
        h, w_in, n_blocks=n_rest // PROJ_TN, w_block_of=rest_w_block, epilogue=_rest_epilogue,
        out_dtype=BF16, aux=(coef,),
        aux_specs=(pl.BlockSpec((4, PROJ_TN), lambda j, i: (0, j)),),
        name="proj_rest")

    os_, lses = [], []
    for g in range(N_ATTN_GROUPS):
        o_g, lse_g = _attention_group(qk, rest, g, batch, seq, v_block0=4)
        os_.append(o_g)
        lses.append(lse_g)

    out = _tail(os_, lses, rest, u, x2,
                w_branch_attn.astype(BF16), w_branch_pool.astype(BF16), pool_maps.astype(BF16),
                pool_scale.astype(F32).reshape(1, POOL_WIDTH), w_out.astype(BF16), seq,
                gate_block0=0, za_block=7, zp_block=8)
    return out.reshape(batch, seq, d)
```

```python
import functools

import numpy as np
import jax
import jax.numpy as jnp
from jax import lax
from jax.experimental import pallas as pl
from jax.experimental.pallas import tpu as pltpu

D_MODEL = 2048
HEAD_DIM = 128
HEADS_PER_GROUP = 8
DILATED_GROUPS = ((128, 1), (512, 4), (2048, 16))
N_ATTN_GROUPS = len(DILATED_GROUPS)
ATTN_OUT = HEADS_PER_GROUP * HEAD_DIM
BLK = 128
ROPE_THETA = 500000.0
ROT_DIM = HEAD_DIM // 4
ROT_HALF = ROT_DIM // 2
POOL_SIZES = (2, 4, 8, 16)
POOL_WIDTH = D_MODEL // 2
POOL_GROUP = POOL_WIDTH // len(POOL_SIZES)
NORM_EPS = 1e-6

LANES = 128
MXU_WIDTH = 256
VMEM_LIMIT_BYTES = 56 * 1024 * 1024

PROJ_TM = 1024
PROJ_TN = 1024
PERM_TM = 256
POOL_HALO = 16

F32 = jnp.float32
BF16 = jnp.bfloat16

_NT = (((1,), (1,)), ((), ()))
_TN = (((0,), (0,)), ((), ()))


def _deinterleave_matrix(d):
    n = PERM_TM // d
    i = np.arange(PERM_TM)
    p = np.zeros((PERM_TM, PERM_TM), np.float32)
    p[i, (i % n) * d + i // n] = 1.0
    return p


def _rmsnorm_kernel(x_ref, g_ref, p4_ref, p16_ref, h1_ref, h4_ref, h16_ref):
    x = x_ref[...]
    ms = jnp.mean(x * x, axis=-1, keepdims=True)
    h = (x * lax.rsqrt(ms + NORM_EPS) * g_ref[...]).astype(BF16)
    h1_ref[...] = h
    for p_ref, out_ref in ((p4_ref, h4_ref), (p16_ref, h16_ref)):
        d, n = out_ref.shape[0], out_ref.shape[1]
        hp = jnp.dot(p_ref[...], h, preferred_element_type=F32).astype(BF16)
        for r in range(d):
            out_ref[r] = hp[r * n:(r + 1) * n, :]


def _rmsnorm(x2, gain, batch, seq):
    t, dm = x2.shape
    tiles = seq // PERM_TM
    perms = [jnp.asarray(_deinterleave_matrix(d), BF16) for _, d in DILATED_GROUPS[1:]]
    out_shape = [jax.ShapeDtypeStruct((t, dm), BF16)]
    out_specs = [pl.BlockSpec((PERM_TM, dm), lambda i: (i, 0))]
    for _, d in DILATED_GROUPS[1:]:
        out_shape.append(jax.ShapeDtypeStruct((batch, d, seq // d, dm), BF16))
        out_specs.append(pl.BlockSpec((None, d, PERM_TM // d, dm),
                                      lambda i: (i // tiles, 0, i % tiles, 0)))
    const = lambda i: (0, 0)
    h1, h4, h16 = pl.pallas_call(
        _rmsnorm_kernel,
        grid=(t // PERM_TM,),
        in_specs=[pl.BlockSpec((PERM_TM, dm), lambda i: (i, 0)),
                  pl.BlockSpec((1, dm), const),
                  pl.BlockSpec((PERM_TM, PERM_TM), const),
                  pl.BlockSpec((PERM_TM, PERM_TM), const)],
        out_specs=out_specs,
        out_shape=out_shape,
        compiler_params=pltpu.CompilerParams(dimension_semantics=("arbitrary",),
                                             vmem_limit_bytes=VMEM_LIMIT_BYTES),
        name="rmsnorm",
    )(x2, gain.reshape(1, dm), *perms)
    return h1, h4.reshape(t, dm), h16.reshape(t, dm)


def _qkv_kernel(h_ref, w_ref, gain_ref, cos_ref, sin_ref, out_ref, wt_ref, *, tm, tn):
    j = pl.program_id(0)

    @pl.when(pl.program_id(1) == 0)
    def _():
        for c in range(tn // MXU_WIDTH):
            cs = slice(c * MXU_WIDTH, (c + 1) * MXU_WIDTH)
            wt_ref[cs, :] = w_ref[:, cs].T.astype(BF16)

    def chunk(c):
        rs = slice(c * MXU_WIDTH, (c + 1) * MXU_WIDTH)
        return rs, lax.dot_general(wt_ref[rs, :], h_ref[...], _NT, preferred_element_type=F32)

    @pl.when(j < 2)
    def _():
        for c in range(tn // MXU_WIDTH):
            rs, acc = chunk(c)
            for hh in range(MXU_WIDTH // HEAD_DIM):
                for lt in range(tm // LANES):
                    ls = slice(lt * LANES, (lt + 1) * LANES)
                    a = acc[hh * HEAD_DIM:(hh + 1) * HEAD_DIM, ls]
                    ss = jnp.sum(a * a, axis=0, keepdims=True)
                    r = lax.rsqrt(ss * (1.0 / HEAD_DIM) + NORM_EPS)
                    ag = a * gain_ref[...]
                    lo, hi = ag[0:ROT_HALF], ag[ROT_HALF:ROT_DIM]
                    cos, sin = cos_ref[:, ls], sin_ref[:, ls]
                    y = jnp.concatenate([lo * cos - hi * sin, hi * cos + lo * sin, ag[ROT_DIM:]], axis=0)
                    row0 = c * MXU_WIDTH + hh * HEAD_DIM
                    out_ref[row0:row0 + HEAD_DIM, ls] = (y * r).astype(out_ref.dtype)

    @pl.when(j == 2)
    def _():
        for c in range(tn // MXU_WIDTH):
            rs, acc = chunk(c)
            out_ref[rs, :] = acc.astype(out_ref.dtype)


def _project_qkv(h, w_in, g, gains, cos_t, sin_t, seq):
    t, dm = h.shape
    tm, tn = PROJ_TM, PROJ_TN
    seq_tiles = seq // tm
    return pl.pallas_call(
        functools.partial(_qkv_kernel, tm=tm, tn=tn),
        grid=(3, t // tm),
        in_specs=[pl.BlockSpec((tm, dm), lambda j, i: (i, 0)),
                  pl.BlockSpec((dm, tn), lambda j, i: (0, j * N_ATTN_GROUPS + g)),
                  pl.BlockSpec((None, HEAD_DIM, LANES), lambda j, i: (jnp.minimum(j, 1), 0, 0)),
                  pl.BlockSpec((ROT_HALF, tm), lambda j, i: (0, i % seq_tiles)),
                  pl.BlockSpec((ROT_HALF, tm), lambda j, i: (0, i % seq_tiles))],
        out_specs=pl.BlockSpec((tn, tm), lambda j, i: (j, i)),
        out_shape=jax.ShapeDtypeStruct((3 * tn, t), BF16),
        scratch_shapes=[pltpu.VMEM((tn, dm), BF16)],
        compiler_params=pltpu.CompilerParams(
            dimension_semantics=("arbitrary", "arbitrary"),
            vmem_limit_bytes=VMEM_LIMIT_BYTES),
        name=f"proj_qkv_g{g}",
    )(h, w_in, gains, cos_t, sin_t)


def _rope_tables_t(seq, d):
    inv_freq = ROPE_THETA ** (-jnp.arange(0, ROT_DIM, 2, dtype=F32) / ROT_DIM)
    pos = jnp.arange(seq, dtype=jnp.int32).reshape(seq // d, d).T.reshape(seq)
    ang = pos.astype(F32)[None, :] * inv_freq[:, None]
    return jnp.cos(ang), jnp.sin(ang)


def _identity_epilogue(acc, cs):
    del cs
    return acc


def _rest_epilogue(acc, cs, coef_ref):
    c2 = coef_ref[0:1, cs]
    c3 = coef_ref[1:2, cs]
    bias = coef_ref[2:3, cs]
    sig = 1.0 / (1.0 + jnp.exp(-(acc + bias)))
    return sig * (acc * c2 + c3)


def _proj_kernel(*refs, epilogue, n_aux, tn):
    h_ref, w_ref = refs[0], refs[1]
    aux = refs[2:2 + n_aux]
    out_ref = refs[2 + n_aux]
    wbf_ref = refs[3 + n_aux]

    @pl.when(pl.program_id(1) == 0)
    def _():
        wbf_ref[...] = w_ref[...].astype(BF16)

    for c in range(tn // MXU_WIDTH):
        cs = slice(c * MXU_WIDTH, (c + 1) * MXU_WIDTH)
        acc = jnp.dot(h_ref[...], wbf_ref[:, cs], preferred_element_type=F32)
        out_ref[:, cs] = epilogue(acc, cs, *aux).astype(out_ref.dtype)


def _project(h, w_in, *, n_blocks, w_block_of, epilogue, out_dtype, aux=(), aux_specs=(), name):
    t, d = h.shape
    tm, tn = PROJ_TM, PROJ_TN
    kern = functools.partial(_proj_kernel, epilogue=epilogue, n_aux=len(aux), tn=tn)
    return pl.pallas_call(
        kern,
        grid=(n_blocks, t // tm),
        in_specs=[pl.BlockSpec((tm, d), lambda j, i: (i, 0)),
                  pl.BlockSpec((d, tn), lambda j, i: (0, w_block_of(j))),
                  *aux_specs],
        out_specs=pl.BlockSpec((tm, tn), lambda j, i: (i, j)),
        out_shape=jax.ShapeDtypeStruct((t, n_blocks * tn), out_dtype),
        scratch_shapes=[pltpu.VMEM((d, tn), BF16)],
        compiler_params=pltpu.CompilerParams(
            dimension_semantics=("arbitrary", "arbitrary"),
            vmem_limit_bytes=VMEM_LIMIT_BYTES),
        name=name,
    )(h, w_in, *aux)


def _attn_kernel(q_ref, kp_ref, kc_ref, vp_ref, vc_ref, o_ref, lse_ref, *, qb):
    n = pl.program_id(1)
    row = lax.broadcasted_iota(jnp.int32, (BLK, 2 * BLK), 0)
    col = lax.broadcasted_iota(jnp.int32, (BLK, 2 * BLK), 1)
    band = (col >= row) & (col <= row + BLK)
    band_first = band & ((col >= BLK) | (n > 0))
    lane = lax.broadcasted_iota(jnp.int32, (BLK, LANES), 1)
    for b in range(qb // BLK):
        ts = slice(b * BLK, (b + 1) * BLK)
        lse_tile = jnp.zeros((BLK, LANES), F32)
        for h in range(HEADS_PER_GROUP):
            cs = slice(h * HEAD_DIM, (h + 1) * HEAD_DIM)
            q_t = q_ref[cs, ts]
            if b == 0:
                kk = jnp.concatenate([kp_ref[cs, :], kc_ref[cs, 0:BLK]], axis=1)
                vv = jnp.concatenate([vp_ref[cs, :], vc_ref[cs, 0:BLK]], axis=1)
                mask = band_first
            else:
                kk = kc_ref[cs, (b - 1) * BLK:(b + 1) * BLK]
                vv = vc_ref[cs, (b - 1) * BLK:(b + 1) * BLK]
                mask = band
            s = lax.dot_general(q_t, kk, _TN, preferred_element_type=F32)
            s = jnp.where(mask, s, -jnp.inf)
            m = jnp.max(s, axis=-1, keepdims=True)
            p = jnp.exp(s - m)
            l = jnp.sum(p, axis=-1, keepdims=True)
            o = lax.dot_general(p.astype(BF16), vv, _NT, preferred_element_type=F32)
            o_ref[ts, cs] = (o / l).astype(o_ref.dtype)
            lse_tile = jnp.where(lane == h, m + jnp.log(l), lse_tile)
        lse_ref[ts, :] = lse_tile


def _attention_group(qkv_t, g, batch, seq):
    _, d = DILATED_GROUPS[g]
    sub = seq // d
    t = batch * seq
    qb = min(2 * BLK, sub)
    nb = sub // qb
    per = qb // BLK
    blocks_per_sub = sub // BLK

    def prev(s, n):
        return s * blocks_per_sub + jnp.maximum(n * per - 1, 0)

    return pl.pallas_call(
        functools.partial(_attn_kernel, qb=qb),
        grid=(batch * d, nb),
        in_specs=[
            pl.BlockSpec((ATTN_OUT, qb), lambda s, n: (0, s * nb + n)),
            pl.BlockSpec((ATTN_OUT, BLK), lambda s, n: (1, prev(s, n))),
            pl.BlockSpec((ATTN_OUT, qb), lambda s, n: (1, s * nb + n)),
            pl.BlockSpec((ATTN_OUT, BLK), lambda s, n: (2, prev(s, n))),
            pl.BlockSpec((ATTN_OUT, qb), lambda s, n: (2, s * nb + n)),
        ],
        out_specs=[
            pl.BlockSpec((qb, ATTN_OUT), lambda s, n: (s * nb + n, 0)),
            pl.BlockSpec((qb, LANES), lambda s, n: (s * nb + n, 0)),
        ],
        out_shape=[jax.ShapeDtypeStruct((t, ATTN_OUT), BF16),
                   jax.ShapeDtypeStruct((t, LANES), F32)],
        compiler_params=pltpu.CompilerParams(
            dimension_semantics=("arbitrary", "arbitrary"),
            vmem_limit_bytes=VMEM_LIMIT_BYTES),
        name=f"window_attention_g{g}",
    )(qkv_t, qkv_t, qkv_t, qkv_t, qkv_t)


def _rows(ref):
    return jnp.concatenate([ref[r] for r in range(ref.shape[0])], axis=0)


def _unpermute_f32(q, x):
    hi = x.astype(BF16)
    r1 = x - hi.astype(F32)
    mid = r1.astype(BF16)
    lo = (r1 - mid.astype(F32)).astype(BF16)
    dot = lambda v: jnp.dot(q, v, preferred_element_type=F32)
    return (dot(hi) + dot(mid)) + dot(lo)


def _tail_kernel(o0_ref, o1_ref, o2_ref, l0_ref, l1_ref, l2_ref, q4_ref, q16_ref, ga_ref, gp_ref,
                 za_ref, zp_ref, u_ref, uh_ref, x_ref, wba_ref, wbp_ref, pm_ref, ps_ref, wo_ref,
                 out_ref, ubuf_ref, *, tm, seq):
    pos0 = (pl.program_id(0) * tm) % seq

    q4, q16 = q4_ref[...], q16_ref[...]
    o0 = o0_ref[...]
    o1 = jnp.dot(q4, _rows(o1_ref), preferred_element_type=F32)
    o2 = jnp.dot(q16, _rows(o2_ref), preferred_element_type=F32)
    l0 = l0_ref[...]
    l1 = _unpermute_f32(q4, _rows(l1_ref))
    l2 = _unpermute_f32(q16, _rows(l2_ref))

    mx = jnp.maximum(jnp.maximum(l0, l1), l2)
    e0, e1, e2 = jnp.exp(l0 - mx), jnp.exp(l1 - mx), jnp.exp(l2 - mx)
    inv = 1.0 / (e0 + e1 + e2)
    w0, w1, w2 = e0 * inv, e1 * inv, e2 * inv
    parts = []
    for h in range(HEADS_PER_GROUP):
        cs = slice(h * HEAD_DIM, (h + 1) * HEAD_DIM)
        a = (w0[:, h:h + 1] * o0[:, cs].astype(F32)
             + w1[:, h:h + 1] * o1[:, cs]
             + w2[:, h:h + 1] * o2[:, cs])
        parts.append((a * za_ref[:, cs].astype(F32)).astype(BF16))
    y_attn = jnp.dot(jnp.concatenate(parts, axis=1), wba_ref[...], preferred_element_type=F32)

    halo = uh_ref[...]
    ubuf_ref[0:POOL_HALO, :] = jnp.where(pos0 == 0, jnp.zeros_like(halo), halo)
    ubuf_ref[POOL_HALO:POOL_HALO + tm, :] = u_ref[...]
    pos = pos0 + lax.broadcasted_iota(jnp.int32, (tm, 1), 0)
    pooled = []
    for g, ksz in enumerate(POOL_SIZES):
        cs = slice(g * POOL_GROUP, (g + 1) * POOL_GROUP)
        u = ubuf_ref[POOL_HALO:POOL_HALO + tm, cs]
        win = u
        for s in range(1, ksz):
            win = win + ubuf_ref[POOL_HALO - s:POOL_HALO - s + tm, cs]
        cnt = jnp.minimum(pos + 1, ksz).astype(F32)
        dlt = win / cnt - u
        pooled.append(jnp.dot(dlt.astype(BF16), pm_ref[g], preferred_element_type=F32))
    pool = jnp.concatenate(pooled, axis=1) * ps_ref[...]
    y_pool = jnp.dot((pool * zp_ref[...].astype(F32)).astype(BF16), wbp_ref[...],
                     preferred_element_type=F32)

    merged = ga_ref[...].astype(F32) * y_attn + gp_ref[...].astype(F32) * y_pool
    out_ref[...] = x_ref[...] + jnp.dot(merged.astype(BF16), wo_ref[...], preferred_element_type=F32)


def _tail(os_, lses, rest, u, x2, wba, wbp, pm, ps, wo, batch, seq, *, gate_block0, za_block, zp_block):
    t, dm = x2.shape
    tm = PERM_TM
    tiles = seq // tm
    halo_per_tile = tm // POOL_HALO
    resident = functools.partial(pl.BlockSpec, pipeline_mode=pl.Buffered(1))
    row = lambda blk: (lambda i: (i, blk))
    grouped = lambda i: (i // tiles, 0, i % tiles, 0)
    gate_blocks = D_MODEL // ATTN_OUT
    unperms = [jnp.asarray(_deinterleave_matrix(d).T, BF16) for _, d in DILATED_GROUPS[1:]]
    o_in, l_in, o_specs, l_specs = [os_[0]], [lses[0]], [pl.BlockSpec((tm, ATTN_OUT), row(0))], \
        [pl.BlockSpec((tm, LANES), row(0))]
    for g in range(1, N_ATTN_GROUPS):
        d = DILATED_GROUPS[g][1]
        o_in.append(os_[g].reshape(batch, d, seq // d, ATTN_OUT))
        l_in.append(lses[g].reshape(batch, d, seq // d, LANES))
        o_specs.append(pl.BlockSpec((None, d, tm // d, ATTN_OUT), grouped))
        l_specs.append(pl.BlockSpec((None, d, tm // d, LANES), grouped))
    in_specs = (
        o_specs + l_specs
        + [resident((tm, tm), lambda i: (0, 0)), resident((tm, tm), lambda i: (0, 0)),
           pl.BlockSpec((tm, D_MODEL), row(gate_block0 // gate_blocks)),
           pl.BlockSpec((tm, D_MODEL), row(gate_block0 // gate_blocks + 1)),
           pl.BlockSpec((tm, ATTN_OUT), row(za_block)),
           pl.BlockSpec((tm, POOL_WIDTH), row(zp_block)),
           pl.BlockSpec((tm, POOL_WIDTH), row(0)),
           pl.BlockSpec((POOL_HALO, POOL_WIDTH),
                        lambda i: (jnp.maximum(i * halo_per_tile - 1, 0), 0)),
           pl.BlockSpec((tm, dm), row(0)),
           resident(wba.shape, lambda i: (0, 0)),
           resident(wbp.shape, lambda i: (0, 0)),
           resident(pm.shape, lambda i: (0, 0, 0)),
           resident(ps.shape, lambda i: (0, 0)),
           resident(wo.shape, lambda i: (0, 0))])
    return pl.pallas_call(
        functools.partial(_tail_kernel, tm=tm, seq=seq),
        grid=(t // tm,),
        in_specs=in_specs,
        out_specs=pl.BlockSpec((tm, dm), row(0)),
        out_shape=jax.ShapeDtypeStruct((t, dm), F32),
        scratch_shapes=[pltpu.VMEM((POOL_HALO + tm, POOL_WIDTH), F32)],
        compiler_params=pltpu.CompilerParams(
            dimension_semantics=("arbitrary",),
            vmem_limit_bytes=VMEM_LIMIT_BYTES),
        name="merge_pool_out",
    )(*o_in, *l_in, *unperms, rest, rest, rest, rest, u, u, x2, wba, wbp, pm, ps, wo)


def kernel(x, norm_gain, w_in, b_gates, q_norm_gain, k_norm_gain, pool_maps, pool_scale,
           w_branch_attn, w_branch_pool, w_out):
    batch, seq, dm = x.shape
    t = batch * seq
    x2 = x.reshape(t, dm)
    hs = _rmsnorm(x2, norm_gain, batch, seq)

    lane_rep = lambda g: jnp.broadcast_to(g.astype(F32)[:, None], (HEAD_DIM, LANES))
    gains = jnp.stack([lane_rep(q_norm_gain * HEAD_DIM ** -0.5), lane_rep(k_norm_gain)])
    os_, lses = [], []
    for g, (_, d) in enumerate(DILATED_GROUPS):
        cos_t, sin_t = _rope_tables_t(seq, d)
        qkv_t = _project_qkv(hs[g], w_in, g, gains, cos_t, sin_t, seq)
        o_g, lse_g = _attention_group(qkv_t, g, batch, seq)
        os_.append(o_g)
        lses.append(lse_g)

    u = _project(hs[0], w_in, n_blocks=1, w_block_of=lambda j: j + 10, epilogue=_identity_epilogue,
                 out_dtype=F32, name="proj_u")

    n_gate = 2 * D_MODEL
    n_rest = n_gate + ATTN_OUT + POOL_WIDTH
    ones = lambda a, b: jnp.zeros((n_rest,), F32).at[a:b].set(1.0)
    coef = jnp.stack([ones(n_gate, n_rest), ones(0, n_gate),
                      jnp.zeros((n_rest,), F32).at[:n_gate].set(b_gates.astype(F32))])
    rest_w_block = lambda j: jnp.where(j < 4, j + 12, jnp.where(j == 4, 9, 11))
    rest = _project(
        hs[0], w_in, n_blocks=n_rest // PROJ_TN, w_block_of=rest_w_block, epilogue=_rest_epilogue,
        out_dtype=BF16, aux=(coef,),
        aux_specs=(pl.BlockSpec((3, PROJ_TN), lambda j, i: (0, j)),),
        name="proj_rest")

    out = _tail(os_, lses, rest, u, x2,
                w_branch_attn.astype(BF16), w_branch_pool.astype(BF16), pool_maps.astype(BF16),
                pool_scale.astype(F32).reshape(1, POOL_WIDTH), w_out.astype(BF16), batch, seq,
                gate_block0=0, za_block=4, zp_block=5)
    return out.reshape(batch, seq, dm)
```

```python
import functools

import numpy as np
import jax
import jax.numpy as jnp
from jax import lax
from jax.experimental import pallas as pl
from jax.experimental.pallas import tpu as pltpu

D_MODEL = 2048
HEAD_DIM = 128
HEADS_PER_GROUP = 8
DILATED_GROUPS = ((128, 1), (512, 4), (2048, 16))
N_ATTN_GROUPS = len(DILATED_GROUPS)
ATTN_OUT = HEADS_PER_GROUP * HEAD_DIM
BLK = 128
ROPE_THETA = 500000.0
ROT_DIM = HEAD_DIM // 4
ROT_HALF = ROT_DIM // 2
POOL_SIZES = (2, 4, 8, 16)
POOL_WIDTH = D_MODEL // 2
POOL_GROUP = POOL_WIDTH // len(POOL_SIZES)
NORM_EPS = 1e-6

LANES = 128
MXU_WIDTH = 256
VMEM_LIMIT_BYTES = 56 * 1024 * 1024

PROJ_TM = 1024
PROJ_TN = 1024
PERM_TM = 256
POOL_HALO = 16

F32 = jnp.float32
BF16 = jnp.bfloat16

_NT = (((1,), (1,)), ((), ()))
_TN = (((0,), (0,)), ((), ()))


def _deinterleave_matrix(d):
    n = PERM_TM // d
    i = np.arange(PERM_TM)
    p = np.zeros((PERM_TM, PERM_TM), np.float32)
    p[i, (i % n) * d + i // n] = 1.0
    return p


def _rmsnorm_kernel(x_ref, g_ref, p4_ref, p16_ref, h1_ref, h4_ref, h16_ref):
    x = x_ref[...]
    ms = jnp.mean(x * x, axis=-1, keepdims=True)
    h = (x * lax.rsqrt(ms + NORM_EPS) * g_ref[...]).astype(BF16)
    h1_ref[...] = h
    for p_ref, out_ref in ((p4_ref, h4_ref), (p16_ref, h16_ref)):
        d, n = out_ref.shape[0], out_ref.shape[1]
        hp = jnp.dot(p_ref[...], h, preferred_element_type=F32).astype(BF16)
        for r in range(d):
            out_ref[r] = hp[r * n:(r + 1) * n, :]


def _rmsnorm(x2, gain, batch, seq):
    t, dm = x2.shape
    tiles = seq // PERM_TM
    perms = [jnp.asarray(_deinterleave_matrix(d), BF16) for _, d in DILATED_GROUPS[1:]]
    out_shape = [jax.ShapeDtypeStruct((t, dm), BF16)]
    out_specs = [pl.BlockSpec((PERM_TM, dm), lambda i: (i, 0))]
    for _, d in DILATED_GROUPS[1:]:
        out_shape.append(jax.ShapeDtypeStruct((batch, d, seq // d, dm), BF16))
        out_specs.append(pl.BlockSpec((None, d, PERM_TM // d, dm),
                                      lambda i: (i // tiles, 0, i % tiles, 0)))
    const = lambda i: (0, 0)
    h1, h4, h16 = pl.pallas_call(
        _rmsnorm_kernel,
        grid=(t // PERM_TM,),
        in_specs=[pl.BlockSpec((PERM_TM, dm), lambda i: (i, 0)),
                  pl.BlockSpec((1, dm), const),
                  pl.BlockSpec((PERM_TM, PERM_TM), const),
                  pl.BlockSpec((PERM_TM, PERM_TM), const)],
        out_specs=out_specs,
        out_shape=out_shape,
        compiler_params=pltpu.CompilerParams(dimension_semantics=("arbitrary",),
                                             vmem_limit_bytes=VMEM_LIMIT_BYTES),
        name="rmsnorm",
    )(x2, gain.reshape(1, dm), *perms)
    return h1, h4.reshape(t, dm), h16.reshape(t, dm)


def _qkv_kernel(h_ref, w_ref, gain_ref, cos_ref, sin_ref, out_ref, wt_ref, *, tm, tn):
    j = pl.program_id(0)

    @pl.when(pl.program_id(1) == 0)
    def _():
        for c in range(tn // MXU_WIDTH):
            cs = slice(c * MXU_WIDTH, (c + 1) * MXU_WIDTH)
            wt_ref[cs, :] = w_ref[:, cs].T.astype(BF16)

    def chunk(c):
        ts = slice(c * MXU_WIDTH, (c + 1) * MXU_WIDTH)
        return ts, lax.dot_general(wt_ref[...], h_ref[ts, :], _NT, preferred_element_type=F32)

    @pl.when(j < 2)
    def _():
        for c in range(tm // MXU_WIDTH):
            ts, acc = chunk(c)
            for hh in range(tn // HEAD_DIM):
                for lt in range(MXU_WIDTH // LANES):
                    ls = slice(c * MXU_WIDTH + lt * LANES, c * MXU_WIDTH + (lt + 1) * LANES)
                    a = acc[hh * HEAD_DIM:(hh + 1) * HEAD_DIM, lt * LANES:(lt + 1) * LANES]
                    ss = jnp.sum(a * a, axis=0, keepdims=True)
                    r = lax.rsqrt(ss * (1.0 / HEAD_DIM) + NORM_EPS)
                    ag = a * gain_ref[...]
                    lo, hi = ag[0:ROT_HALF], ag[ROT_HALF:ROT_DIM]
                    cos, sin = cos_ref[:, ls], sin_ref[:, ls]
                    y = jnp.concatenate([lo * cos - hi * sin, hi * cos + lo * sin, ag[ROT_DIM:]], axis=0)
                    out_ref[hh * HEAD_DIM:(hh + 1) * HEAD_DIM, ls] = (y * r).astype(out_ref.dtype)

    @pl.when(j == 2)
    def _():
        for c in range(tm // MXU_WIDTH):
            ts, acc = chunk(c)
            out_ref[:, ts] = acc.astype(out_ref.dtype)


def _project_qkv(h, w_in, g, gains, cos_t, sin_t, seq):
    t, dm = h.shape
    tm, tn = PROJ_TM, PROJ_TN
    seq_tiles = seq // tm
    return pl.pallas_call(
        functools.partial(_qkv_kernel, tm=tm, tn=tn),
        grid=(3, t // tm),
        in_specs=[pl.BlockSpec((tm, dm), lambda j, i: (i, 0)),
                  pl.BlockSpec((dm, tn), lambda j, i: (0, j * N_ATTN_GROUPS + g)),
                  pl.BlockSpec((None, HEAD_DIM, LANES), lambda j, i: (jnp.minimum(j, 1), 0, 0)),
                  pl.BlockSpec((ROT_HALF, tm), lambda j, i: (0, i % seq_tiles)),
                  pl.BlockSpec((ROT_HALF, tm), lambda j, i: (0, i % seq_tiles))],
        out_specs=pl.BlockSpec((tn, tm), lambda j, i: (j, i)),
        out_shape=jax.ShapeDtypeStruct((3 * tn, t), BF16),
        scratch_shapes=[pltpu.VMEM((tn, dm), BF16)],
        compiler_params=pltpu.CompilerParams(
            dimension_semantics=("arbitrary", "arbitrary"),
            vmem_limit_bytes=VMEM_LIMIT_BYTES),
        name=f"proj_qkv_g{g}",
    )(h, w_in, gains, cos_t, sin_t)


def _rope_tables_t(seq, d):
    inv_freq = ROPE_THETA ** (-jnp.arange(0, ROT_DIM, 2, dtype=F32) / ROT_DIM)
    pos = jnp.arange(seq, dtype=jnp.int32).reshape(seq // d, d).T.reshape(seq)
    ang = pos.astype(F32)[None, :] * inv_freq[:, None]
    return jnp.cos(ang), jnp.sin(ang)


def _identity_epilogue(acc, cs):
    del cs
    return acc


def _gate_epilogue(acc, cs, half_bias_ref):
    return 0.5 + 0.5 * jnp.tanh(0.5 * acc + half_bias_ref[:, cs])


def _silu_epilogue(acc, cs, half_bias_ref):
    del cs, half_bias_ref
    half = 0.5 * acc
    return half + half * jnp.tanh(half)


def _proj_kernel(*refs, epilogues, n_aux, tn):
    h_ref, w_ref = refs[0], refs[1]
    aux = refs[2:2 + n_aux]
    out_ref = refs[2 + n_aux]
    wbf_ref = refs[3 + n_aux]
    j = pl.program_id(0)

    @pl.when(pl.program_id(1) == 0)
    def _():
        wbf_ref[...] = w_ref[...].astype(BF16)

    def body(epilogue):
        for c in range(tn // MXU_WIDTH):
            cs = slice(c * MXU_WIDTH, (c + 1) * MXU_WIDTH)
            acc = jnp.dot(h_ref[...], wbf_ref[:, cs], preferred_element_type=F32)
            out_ref[:, cs] = epilogue(acc, cs, *aux).astype(out_ref.dtype)

    if len(epilogues) == 1:
        body(epilogues[0][1])
    else:
        bounds = [first for first, _ in epilogues[1:]] + [None]
        for (first, fn), last in zip(epilogues, bounds):
            cond = (j >= first) if last is None else ((j >= first) & (j < last))
            pl.when(cond)(functools.partial(body, fn))


def _project(h, w_in, *, n_blocks, w_block_of, epilogues, out_dtype, aux=(), aux_specs=(), name):
    t, d = h.shape
    tm, tn = PROJ_TM, PROJ_TN
    kern = functools.partial(_proj_kernel, epilogues=epilogues, n_aux=len(aux), tn=tn)
    return pl.pallas_call(
        kern,
        grid=(n_blocks, t // tm),
        in_specs=[pl.BlockSpec((tm, d), lambda j, i: (i, 0)),
                  pl.BlockSpec((d, tn), lambda j, i: (0, w_block_of(j))),
                  *aux_specs],
        out_specs=pl.BlockSpec((tm, tn), lambda j, i: (i, j)),
        out_shape=jax.ShapeDtypeStruct((t, n_blocks * tn), out_dtype),
        scratch_shapes=[pltpu.VMEM((d, tn), BF16)],
        compiler_params=pltpu.CompilerParams(
            dimension_semantics=("arbitrary", "arbitrary"),
            vmem_limit_bytes=VMEM_LIMIT_BYTES),
        name=name,
    )(h, w_in, *aux)


def _attn_kernel(q_ref, kp_ref, kc_ref, vp_ref, vc_ref, o_ref, lse_ref, *, qb):
    n = pl.program_id(1)
    row = lax.broadcasted_iota(jnp.int32, (BLK, 2 * BLK), 0)
    col = lax.broadcasted_iota(jnp.int32, (BLK, 2 * BLK), 1)
    band = (col >= row) & (col <= row + BLK)
    band_first = band & ((col >= BLK) | (n > 0))
    lane = lax.broadcasted_iota(jnp.int32, (BLK, LANES), 1)
    for b in range(qb // BLK):
        ts = slice(b * BLK, (b + 1) * BLK)
        lse_tile = jnp.zeros((BLK, LANES), F32)
        for h in range(HEADS_PER_GROUP):
            cs = slice(h * HEAD_DIM, (h + 1) * HEAD_DIM)
            q_t = q_ref[cs, ts]
            if b == 0:
                kk = jnp.concatenate([kp_ref[cs, :], kc_ref[cs, 0:BLK]], axis=1)
                vv = jnp.concatenate([vp_ref[cs, :], vc_ref[cs, 0:BLK]], axis=1)
                mask = band_first
            else:
                kk = kc_ref[cs, (b - 1) * BLK:(b + 1) * BLK]
                vv = vc_ref[cs, (b - 1) * BLK:(b + 1) * BLK]
                mask = band
            s = lax.dot_general(q_t, kk, _TN, preferred_element_type=F32)
            s = jnp.where(mask, s, -jnp.inf)
            m = jnp.max(s, axis=-1, keepdims=True)
            p = jnp.exp(s - m)
            l = jnp.sum(p, axis=-1, keepdims=True)
            o = lax.dot_general(p.astype(BF16), vv, _NT, preferred_element_type=F32)
            o_ref[ts, cs] = (o / l).astype(o_ref.dtype)
            lse_tile = jnp.where(lane == h, m + jnp.log(l), lse_tile)
        lse_ref[ts, :] = lse_tile


def _attention_group(qkv_t, g, batch, seq):
    _, d = DILATED_GROUPS[g]
    sub = seq // d
    t = batch * seq
    qb = min(2 * BLK, sub)
    nb = sub // qb
    per = qb // BLK
    blocks_per_sub = sub // BLK

    def prev(s, n):
        return s * blocks_per_sub + jnp.maximum(n * per - 1, 0)

    return pl.pallas_call(
        functools.partial(_attn_kernel, qb=qb),
        grid=(batch * d, nb),
        in_specs=[
            pl.BlockSpec((ATTN_OUT, qb), lambda s, n: (0, s * nb + n)),
            pl.BlockSpec((ATTN_OUT, BLK), lambda s, n: (1, prev(s, n))),
            pl.BlockSpec((ATTN_OUT, qb), lambda s, n: (1, s * nb + n)),
            pl.BlockSpec((ATTN_OUT, BLK), lambda s, n: (2, prev(s, n))),
            pl.BlockSpec((ATTN_OUT, qb), lambda s, n: (2, s * nb + n)),
        ],
        out_specs=[
            pl.BlockSpec((qb, ATTN_OUT), lambda s, n: (s * nb + n, 0)),
            pl.BlockSpec((qb, LANES), lambda s, n: (s * nb + n, 0)),
        ],
        out_shape=[jax.ShapeDtypeStruct((t, ATTN_OUT), BF16),
                   jax.ShapeDtypeStruct((t, LANES), F32)],
        compiler_params=pltpu.CompilerParams(
            dimension_semantics=("arbitrary", "arbitrary"),
            vmem_limit_bytes=VMEM_LIMIT_BYTES),
        name=f"window_attention_g{g}",
    )(qkv_t, qkv_t, qkv_t, qkv_t, qkv_t)


def _rows(ref):
    return jnp.concatenate([ref[r] for r in range(ref.shape[0])], axis=0)


def _unpermute_f32(q, x):
    hi = x.astype(BF16)
    r1 = x - hi.astype(F32)
    mid = r1.astype(BF16)
    lo = (r1 - mid.astype(F32)).astype(BF16)
    dot = lambda v: jnp.dot(q, v, preferred_element_type=F32)
    return (dot(hi) + dot(mid)) + dot(lo)


def _tail_kernel(o0_ref, o1_ref, o2_ref, l0_ref, l1_ref, l2_ref, q4_ref, q16_ref, ga_ref, gp_ref,
                 za_ref, zp_ref, u_ref, uh_ref, x_ref, wba_ref, wbp_ref, pm_ref, ps_ref, wo_ref,
                 out_ref, ubuf_ref, *, tm, seq):
    pos0 = (pl.program_id(0) * tm) % seq

    q4, q16 = q4_ref[...], q16_ref[...]
    o0 = o0_ref[...]
    o1 = jnp.dot(q4, _rows(o1_ref), preferred_element_type=F32)
    o2 = jnp.dot(q16, _rows(o2_ref), preferred_element_type=F32)
    l0 = l0_ref[...]
    l1 = _unpermute_f32(q4, _rows(l1_ref))
    l2 = _unpermute_f32(q16, _rows(l2_ref))

    mx = jnp.maximum(jnp.maximum(l0, l1), l2)
    e0, e1, e2 = jnp.exp(l0 - mx), jnp.exp(l1 - mx), jnp.exp(l2 - mx)
    inv = 1.0 / (e0 + e1 + e2)
    w0, w1, w2 = e0 * inv, e1 * inv, e2 * inv
    parts = []
    for h in range(HEADS_PER_GROUP):
        cs = slice(h * HEAD_DIM, (h + 1) * HEAD_DIM)
        a = (w0[:, h:h + 1] * o0[:, cs].astype(F32)
             + w1[:, h:h + 1] * o1[:, cs]
             + w2[:, h:h + 1] * o2[:, cs])
        parts.append((a * za_ref[:, cs].astype(F32)).astype(BF16))
    y_attn = jnp.dot(jnp.concatenate(parts, axis=1), wba_ref[...], preferred_element_type=F32)

    halo = uh_ref[...]
    ubuf_ref[0:POOL_HALO, :] = jnp.where(pos0 == 0, jnp.zeros_like(halo), halo)
    ubuf_ref[POOL_HALO:POOL_HALO + tm, :] = u_ref[...]
    pos = pos0 + lax.broadcasted_iota(jnp.int32, (tm, 1), 0)
    pooled = []
    for g, ksz in enumerate(POOL_SIZES):
        cs = slice(g * POOL_GROUP, (g + 1) * POOL_GROUP)
        u = ubuf_ref[POOL_HALO:POOL_HALO + tm, cs]
        win = u
        for s in range(1, ksz):
            win = win + ubuf_ref[POOL_HALO - s:POOL_HALO - s + tm, cs]
        cnt = jnp.minimum(pos + 1, ksz).astype(F32)
        dlt = win / cnt - u
        pooled.append(jnp.dot(dlt.astype(BF16), pm_ref[g], preferred_element_type=F32))
    pool = jnp.concatenate(pooled, axis=1) * ps_ref[...]
    y_pool = jnp.dot((pool * zp_ref[...].astype(F32)).astype(BF16), wbp_ref[...],
                     preferred_element_type=F32)

    merged = ga_ref[...].astype(F32) * y_attn + gp_ref[...].astype(F32) * y_pool
    out_ref[...] = x_ref[...] + jnp.dot(merged.astype(BF16), wo_ref[...], preferred_element_type=F32)


def _tail(os_, lses, rest, u, x2, wba, wbp, pm, ps, wo, batch, seq, *, gate_block0, za_block, zp_block):
    t, dm = x2.shape
    tm = PERM_TM
    tiles = seq // tm
    halo_per_tile = tm // POOL_HALO
    resident = functools.partial(pl.BlockSpec, pipeline_mode=pl.Buffered(1))
    row = lambda blk: (lambda i: (i, blk))
    grouped = lambda i: (i // tiles, 0, i % tiles, 0)
    gate_blocks = D_MODEL // ATTN_OUT
    unperms = [jnp.asarray(_deinterleave_matrix(d).T, BF16) for _, d in DILATED_GROUPS[1:]]
    o_in, l_in, o_specs, l_specs = [os_[0]], [lses[0]], [pl.BlockSpec((tm, ATTN_OUT), row(0))], \
        [pl.BlockSpec((tm, LANES), row(0))]
    for g in range(1, N_ATTN_GROUPS):
        d = DILATED_GROUPS[g][1]
        o_in.append(os_[g].reshape(batch, d, seq // d, ATTN_OUT))
        l_in.append(lses[g].reshape(batch, d, seq // d, LANES))
        o_specs.append(pl.BlockSpec((None, d, tm // d, ATTN_OUT), grouped))
        l_specs.append(pl.BlockSpec((None, d, tm // d, LANES), grouped))
    in_specs = (
        o_specs + l_specs
        + [resident((tm, tm), lambda i: (0, 0)), resident((tm, tm), lambda i: (0, 0)),
           pl.BlockSpec((tm, D_MODEL), row(gate_block0 // gate_blocks)),
           pl.BlockSpec((tm, D_MODEL), row(gate_block0 // gate_blocks + 1)),
           pl.BlockSpec((tm, ATTN_OUT), row(za_block)),
           pl.BlockSpec((tm, POOL_WIDTH), row(zp_block)),
           pl.BlockSpec((tm, POOL_WIDTH), row(0)),
           pl.BlockSpec((POOL_HALO, POOL_WIDTH),
                        lambda i: (jnp.maximum(i * halo_per_tile - 1, 0), 0)),
           pl.BlockSpec((tm, dm), row(0)),
           resident(wba.shape, lambda i: (0, 0)),
           resident(wbp.shape, lambda i: (0, 0)),
           resident(pm.shape, lambda i: (0, 0, 0)),
           resident(ps.shape, lambda i: (0, 0)),
           resident(wo.shape, lambda i: (0, 0))])
    return pl.pallas_call(
        functools.partial(_tail_kernel, tm=tm, seq=seq),
        grid=(t // tm,),
        in_specs=in_specs,
        out_specs=pl.BlockSpec((tm, dm), row(0)),
        out_shape=jax.ShapeDtypeStruct((t, dm), F32),
        scratch_shapes=[pltpu.VMEM((POOL_HALO + tm, POOL_WIDTH), F32)],
        compiler_params=pltpu.CompilerParams(
            dimension_semantics=("arbitrary",),
            vmem_limit_bytes=VMEM_LIMIT_BYTES),
        name="merge_pool_out",
    )(*o_in, *l_in, *unperms, rest, rest, rest, rest, u, u, x2, wba, wbp, pm, ps, wo)


def kernel(x, norm_gain, w_in, b_gates, q_norm_gain, k_norm_gain, pool_maps, pool_scale,
           w_branch_attn, w_branch_pool, w_out):
    batch, seq, dm = x.shape
    t = batch * seq
    x2 = x.reshape(t, dm)
    hs = _rmsnorm(x2, norm_gain, batch, seq)

    lane_rep = lambda g: jnp.broadcast_to(g.astype(F32)[:, None], (HEAD_DIM, LANES))
    gains = jnp.stack([lane_rep(q_norm_gain * HEAD_DIM ** -0.5), lane_rep(k_norm_gain)])
    os_, lses = [], []
    for g, (_, d) in enumerate(DILATED_GROUPS):
        cos_t, sin_t = _rope_tables_t(seq, d)
        qkv_t = _project_qkv(hs[g], w_in, g, gains, cos_t, sin_t, seq)
        o_g, lse_g = _attention_group(qkv_t, g, batch, seq)
        os_.append(o_g)
        lses.append(lse_g)

    u = _project(hs[0], w_in, n_blocks=1, w_block_of=lambda j: j + 10,
                 epilogues=((0, _identity_epilogue),), out_dtype=F32, name="proj_u")

    n_gate_blocks = 2 * D_MODEL // PROJ_TN
    half_bias = (0.5 * b_gates.astype(F32)).reshape(1, 2 * D_MODEL)
    rest_w_block = lambda j: jnp.where(j < 4, j + 12, jnp.where(j == 4, 9, 11))
    rest = _project(
        hs[0], w_in, n_blocks=n_gate_blocks + 2, w_block_of=rest_w_block,
        epilogues=((0, _gate_epilogue), (n_gate_blocks, _silu_epilogue)),
        out_dtype=BF16, aux=(half_bias,),
        aux_specs=(pl.BlockSpec((1, PROJ_TN), lambda j, i: (0, jnp.minimum(j, n_gate_blocks - 1))),),
        name="proj_rest")

    out = _tail(os_, lses, rest, u, x2,
                w_branch_attn.astype(BF16), w_branch_pool.astype(BF16), pool_maps.astype(BF16),
                pool_scale.astype(F32).reshape(1, POOL_WIDTH), w_out.astype(BF16), batch, seq,
                gate_block0=0, za_block=4, zp_block=5)
    return out.reshape(batch, seq, dm)
```

```python
import functools

import numpy as np
import jax
import jax.numpy as jnp
from jax import lax
from jax.experimental import pallas as pl
from jax.experimental.pallas import tpu as pltpu

D_MODEL = 2048
HEAD_DIM = 128
HEADS_PER_GROUP = 8
DILATED_GROUPS = ((128, 1), (512, 4), (2048, 16))
N_ATTN_GROUPS = len(DILATED_GROUPS)
ATTN_OUT = HEADS_PER_GROUP * HEAD_DIM
BLK = 128
ROPE_THETA = 500000.0
ROT_DIM = HEAD_DIM // 4
ROT_HALF = ROT_DIM // 2
POOL_SIZES = (2, 4, 8, 16)
POOL_WIDTH = D_MODEL // 2
POOL_GROUP = POOL_WIDTH // len(POOL_SIZES)
NORM_EPS = 1e-6

LANES = 128
MXU_WIDTH = 256
VMEM_LIMIT_BYTES = 56 * 1024 * 1024

PROJ_TM = 1024
PROJ_TN = 1024
PERM_TM = 256
ATTN_TB = 256
POOL_HALO = 16

F32 = jnp.float32
BF16 = jnp.bfloat16

_NT = (((1,), (1,)), ((), ()))
_TN = (((0,), (0,)), ((), ()))


def _deinterleave_matrix(d):
    n = PERM_TM // d
    i = np.arange(PERM_TM)
    p = np.zeros((PERM_TM, PERM_TM), np.float32)
    p[i, (i % n) * d + i // n] = 1.0
    return p


def _rmsnorm_kernel(x_ref, g_ref, p4_ref, p16_ref, h1_ref, h4_ref, h16_ref):
    x = x_ref[...]
    ms = jnp.mean(x * x, axis=-1, keepdims=True)
    h = (x * lax.rsqrt(ms + NORM_EPS) * g_ref[...]).astype(BF16)
    h1_ref[...] = h
    for p_ref, out_ref in ((p4_ref, h4_ref), (p16_ref, h16_ref)):
        d, n = out_ref.shape[0], out_ref.shape[1]
        hp = jnp.dot(p_ref[...], h, preferred_element_type=F32).astype(BF16)
        for r in range(d):
            out_ref[r] = hp[r * n:(r + 1) * n, :]


def _rmsnorm(x2, gain, batch, seq):
    t, dm = x2.shape
    tiles = seq // PERM_TM
    perms = [jnp.asarray(_deinterleave_matrix(d), BF16) for _, d in DILATED_GROUPS[1:]]
    out_shape = [jax.ShapeDtypeStruct((t, dm), BF16)]
    out_specs = [pl.BlockSpec((PERM_TM, dm), lambda i: (i, 0))]
    for _, d in DILATED_GROUPS[1:]:
        out_shape.append(jax.ShapeDtypeStruct((batch, d, seq // d, dm), BF16))
        out_specs.append(pl.BlockSpec((None, d, PERM_TM // d, dm),
                                      lambda i: (i // tiles, 0, i % tiles, 0)))
    const = lambda i: (0, 0)
    h1, h4, h16 = pl.pallas_call(
        _rmsnorm_kernel,
        grid=(t // PERM_TM,),
        in_specs=[pl.BlockSpec((PERM_TM, dm), lambda i: (i, 0)),
                  pl.BlockSpec((1, dm), const),
                  pl.BlockSpec((PERM_TM, PERM_TM), const),
                  pl.BlockSpec((PERM_TM, PERM_TM), const)],
        out_specs=out_specs,
        out_shape=out_shape,
        compiler_params=pltpu.CompilerParams(dimension_semantics=("arbitrary",),
                                             vmem_limit_bytes=VMEM_LIMIT_BYTES),
        name="rmsnorm",
    )(x2, gain.reshape(1, dm), *perms)
    return h1, h4.reshape(t, dm), h16.reshape(t, dm)


def _qkv_kernel(h_ref, w_ref, gain_ref, cos_ref, sin_ref, out_ref, wt_ref, *, tm, tn):
    j = pl.program_id(0)

    @pl.when(pl.program_id(1) == 0)
    def _():
        for c in range(tn // MXU_WIDTH):
            cs = slice(c * MXU_WIDTH, (c + 1) * MXU_WIDTH)
            wt_ref[cs, :] = w_ref[:, cs].T.astype(BF16)

    def chunk(c):
        ts = slice(c * ATTN_TB, (c + 1) * ATTN_TB)
        return lax.dot_general(wt_ref[...], h_ref[ts, :], _NT, preferred_element_type=F32)

    @pl.when(j < 2)
    def _():
        for c in range(tm // ATTN_TB):
            acc = chunk(c)
            for hh in range(tn // HEAD_DIM):
                hs = slice(hh * HEAD_DIM, (hh + 1) * HEAD_DIM)
                for lt in range(ATTN_TB // LANES):
                    ls = slice(lt * LANES, (lt + 1) * LANES)
                    pos = slice(c * ATTN_TB + lt * LANES, c * ATTN_TB + (lt + 1) * LANES)
                    a = acc[hs, ls]
                    ss = jnp.sum(a * a, axis=0, keepdims=True)
                    r = lax.rsqrt(ss * (1.0 / HEAD_DIM) + NORM_EPS)
                    ag = a * gain_ref[...]
                    lo, hi = ag[0:ROT_HALF], ag[ROT_HALF:ROT_DIM]
                    cos, sin = cos_ref[:, pos], sin_ref[:, pos]
                    y = jnp.concatenate([lo * cos - hi * sin, hi * cos + lo * sin, ag[ROT_DIM:]], axis=0)
                    out_ref[c, hs, ls] = (y * r).astype(out_ref.dtype)

    @pl.when(j == 2)
    def _():
        for c in range(tm // ATTN_TB):
            out_ref[c] = chunk(c).astype(out_ref.dtype)


def _project_qkv(h, w_in, g, gains, cos_t, sin_t, seq):
    t, dm = h.shape
    tm, tn = PROJ_TM, PROJ_TN
    seq_tiles = seq // tm
    return pl.pallas_call(
        functools.partial(_qkv_kernel, tm=tm, tn=tn),
        grid=(3, t // tm),
        in_specs=[pl.BlockSpec((tm, dm), lambda j, i: (i, 0)),
                  pl.BlockSpec((dm, tn), lambda j, i: (0, j * N_ATTN_GROUPS + g)),
                  pl.BlockSpec((None, HEAD_DIM, LANES), lambda j, i: (jnp.minimum(j, 1), 0, 0)),
                  pl.BlockSpec((ROT_HALF, tm), lambda j, i: (0, i % seq_tiles)),
                  pl.BlockSpec((ROT_HALF, tm), lambda j, i: (0, i % seq_tiles))],
        out_specs=pl.BlockSpec((None, tm // ATTN_TB, tn, ATTN_TB), lambda j, i: (j, i, 0, 0)),
        out_shape=jax.ShapeDtypeStruct((3, t // ATTN_TB, tn, ATTN_TB), BF16),
        scratch_shapes=[pltpu.VMEM((tn, dm), BF16)],
        compiler_params=pltpu.CompilerParams(
            dimension_semantics=("arbitrary", "arbitrary"),
            vmem_limit_bytes=VMEM_LIMIT_BYTES),
        name=f"proj_qkv_g{g}",
    )(h, w_in, gains, cos_t, sin_t)


def _rope_tables_t(seq, d):
    inv_freq = ROPE_THETA ** (-jnp.arange(0, ROT_DIM, 2, dtype=F32) / ROT_DIM)
    pos = jnp.arange(seq, dtype=jnp.int32).reshape(seq // d, d).T.reshape(seq)
    ang = pos.astype(F32)[None, :] * inv_freq[:, None]
    return jnp.cos(ang), jnp.sin(ang)


def _identity_epilogue(acc, cs):
    del cs
    return acc


def _gate_epilogue(acc, cs, half_bias_ref):
    return 0.5 + 0.5 * jnp.tanh(0.5 * acc + half_bias_ref[:, cs])


def _silu_epilogue(acc, cs, half_bias_ref):
    del cs, half_bias_ref
    half = 0.5 * acc
    return half + half * jnp.tanh(half)


def _proj_kernel(*refs, epilogues, n_aux, tn):
    h_ref, w_ref = refs[0], refs[1]
    aux = refs[2:2 + n_aux]
    out_ref = refs[2 + n_aux]
    wbf_ref = refs[3 + n_aux]
    j = pl.program_id(0)

    @pl.when(pl.program_id(1) == 0)
    def _():
        wbf_ref[...] = w_ref[...].astype(BF16)

    def body(epilogue):
        for c in range(tn // MXU_WIDTH):
            cs = slice(c * MXU_WIDTH, (c + 1) * MXU_WIDTH)
            acc = jnp.dot(h_ref[...], wbf_ref[:, cs], preferred_element_type=F32)
            out_ref[:, cs] = epilogue(acc, cs, *aux).astype(out_ref.dtype)

    if len(epilogues) == 1:
        body(epilogues[0][1])
    else:
        bounds = [first for first, _ in epilogues[1:]] + [None]
        for (first, fn), last in zip(epilogues, bounds):
            cond = (j >= first) if last is None else ((j >= first) & (j < last))
            pl.when(cond)(functools.partial(body, fn))


def _project(h, w_in, *, n_blocks, w_block_of, epilogues, out_dtype, aux=(), aux_specs=(), name):
    t, d = h.shape
    tm, tn = PROJ_TM, PROJ_TN
    kern = functools.partial(_proj_kernel, epilogues=epilogues, n_aux=len(aux), tn=tn)
    return pl.pallas_call(
        kern,
        grid=(n_blocks, t // tm),
        in_specs=[pl.BlockSpec((tm, d), lambda j, i: (i, 0)),
                  pl.BlockSpec((d, tn), lambda j, i: (0, w_block_of(j))),
                  *aux_specs],
        out_specs=pl.BlockSpec((tm, tn), lambda j, i: (i, j)),
        out_shape=jax.ShapeDtypeStruct((t, n_blocks * tn), out_dtype),
        scratch_shapes=[pltpu.VMEM((d, tn), BF16)],
        compiler_params=pltpu.CompilerParams(
            dimension_semantics=("arbitrary", "arbitrary"),
            vmem_limit_bytes=VMEM_LIMIT_BYTES),
        name=name,
    )(h, w_in, *aux)


def _band_bias():
    i = np.arange(BLK)[:, None]
    j = np.arange(2 * BLK)[None, :]
    band = (j >= i) & (j <= i + BLK)
    first = band & (j >= BLK)
    return np.where(np.stack([band, first]), 0.0, -np.inf).astype(np.float32)


def _attn_kernel(bias_ref, q_ref, k_ref, v_ref, o_ref, lse_ref, kprev_ref, vprev_ref, *, blocks_per_sub):
    per = ATTN_TB // BLK
    step = pl.program_id(0)

    @pl.when(step == 0)
    def _():
        kprev_ref[...] = jnp.zeros_like(kprev_ref)
        vprev_ref[...] = jnp.zeros_like(vprev_ref)

    lane = lax.broadcasted_iota(jnp.int32, (BLK, LANES), 1)
    for b in range(per):
        ts = slice(b * BLK, (b + 1) * BLK)
        if b == 0:
            first = step % (blocks_per_sub // per) == 0
            bias = bias_ref[jnp.where(first, 1, 0)]
        else:
            bias = bias_ref[0]
        m_tile = jnp.zeros((BLK, LANES), F32)
        l_tile = jnp.ones((BLK, LANES), F32)
        for h in range(HEADS_PER_GROUP):
            cs = slice(h * HEAD_DIM, (h + 1) * HEAD_DIM)
            q_t = q_ref[cs, ts]
            if b == 0:
                kk = jnp.concatenate([kprev_ref[cs, :], k_ref[cs, 0:BLK]], axis=1)
                vv = jnp.concatenate([vprev_ref[cs, :], v_ref[cs, 0:BLK]], axis=1)
            else:
                kk = k_ref[cs, (b - 1) * BLK:(b + 1) * BLK]
                vv = v_ref[cs, (b - 1) * BLK:(b + 1) * BLK]
            s = lax.dot_general(q_t, kk, _TN, preferred_element_type=F32) + bias
            m = jnp.max(s, axis=-1, keepdims=True)
            p = jnp.exp(s - m)
            l = jnp.sum(p, axis=-1, keepdims=True)
            o = lax.dot_general(p.astype(BF16), vv, _NT, preferred_element_type=F32)
            o_ref[ts, cs] = (o / l).astype(o_ref.dtype)
            m_tile = jnp.where(lane == h, m, m_tile)
            l_tile = jnp.where(lane == h, l, l_tile)
        lse_ref[ts, :] = m_tile + jnp.log(l_tile)
    kprev_ref[...] = k_ref[:, (per - 1) * BLK:per * BLK]
    vprev_ref[...] = v_ref[:, (per - 1) * BLK:per * BLK]


def _attention_group(qkv_t, g, batch, seq):
    _, d = DILATED_GROUPS[g]
    blocks_per_sub = seq // d // BLK
    t = batch * seq
    tb = ATTN_TB
    tile = lambda part: pl.BlockSpec((None, None, ATTN_OUT, tb), lambda s: (part, s, 0, 0))
    return pl.pallas_call(
        functools.partial(_attn_kernel, blocks_per_sub=blocks_per_sub),
        grid=(t // tb,),
        in_specs=[
            pl.BlockSpec((2, BLK, 2 * BLK), lambda s: (0, 0, 0), pipeline_mode=pl.Buffered(1)),
            tile(0), tile(1), tile(2),
        ],
        out_specs=[
            pl.BlockSpec((tb, ATTN_OUT), lambda s: (s, 0)),
            pl.BlockSpec((tb, LANES), lambda s: (s, 0)),
        ],
        out_shape=[jax.ShapeDtypeStruct((t, ATTN_OUT), BF16),
                   jax.ShapeDtypeStruct((t, LANES), F32)],
        scratch_shapes=[pltpu.VMEM((ATTN_OUT, BLK), BF16), pltpu.VMEM((ATTN_OUT, BLK), BF16)],
        compiler_params=pltpu.CompilerParams(
            dimension_semantics=("arbitrary",),
            vmem_limit_bytes=VMEM_LIMIT_BYTES),
        name=f"window_attention_g{g}",
    )(jnp.asarray(_band_bias()), qkv_t, qkv_t, qkv_t)


def _rows(ref):
    return jnp.concatenate([ref[r] for r in range(ref.shape[0])], axis=0)


def _unpermute_f32(q, x):
    hi = x.astype(BF16)
    r1 = x - hi.astype(F32)
    mid = r1.astype(BF16)
    lo = (r1 - mid.astype(F32)).astype(BF16)
    dot = lambda v: jnp.dot(q, v, preferred_element_type=F32)
    return (dot(hi) + dot(mid)) + dot(lo)


def _tail_kernel(o0_ref, o1_ref, o2_ref, l0_ref, l1_ref, l2_ref, q4_ref, q16_ref, ga_ref, gp_ref,
                 za_ref, zp_ref, u_ref, uh_ref, wba_ref, wbp_ref, pm_ref, ps_ref,
                 out_ref, *, tm, seq):
    pos0 = (pl.program_id(0) * tm) % seq

    q4, q16 = q4_ref[...], q16_ref[...]
    o0 = o0_ref[...]
    o1 = jnp.dot(q4, _rows(o1_ref), preferred_element_type=F32)
    o2 = jnp.dot(q16, _rows(o2_ref), preferred_element_type=F32)
    l0 = l0_ref[...]
    l1 = _unpermute_f32(q4, _rows(l1_ref))
    l2 = _unpermute_f32(q16, _rows(l2_ref))

    mx = jnp.maximum(jnp.maximum(l0, l1), l2)
    e0, e1, e2 = jnp.exp(l0 - mx), jnp.exp(l1 - mx), jnp.exp(l2 - mx)
    inv = 1.0 / (e0 + e1 + e2)
    w0, w1, w2 = e0 * inv, e1 * inv, e2 * inv
    parts = []
    for h in range(HEADS_PER_GROUP):
        cs = slice(h * HEAD_DIM, (h + 1) * HEAD_DIM)
        a = (w0[:, h:h + 1] * o0[:, cs].astype(F32)
             + w1[:, h:h + 1] * o1[:, cs]
             + w2[:, h:h + 1] * o2[:, cs])
        parts.append((a * za_ref[:, cs].astype(F32)).astype(BF16))
    y_attn = jnp.dot(jnp.concatenate(parts, axis=1), wba_ref[...], preferred_element_type=F32)

    halo = uh_ref[...]
    halo = jnp.where(pos0 == 0, jnp.zeros_like(halo), halo)
    pos = pos0 + lax.broadcasted_iota(jnp.int32, (tm, 1), 0)
    pooled = []
    for g, ksz in enumerate(POOL_SIZES):
        cs = slice(g * POOL_GROUP, (g + 1) * POOL_GROUP)
        ext = jnp.concatenate([halo[:, cs], u_ref[:, cs]], axis=0)
        win, shift = ext, 1
        while shift < ksz:
            win = win + pltpu.roll(win, shift, 0)
            shift *= 2
        u = ext[POOL_HALO:]
        cnt = jnp.minimum(pos + 1, ksz).astype(F32)
        dlt = win[POOL_HALO:] / cnt - u
        pooled.append(jnp.dot(dlt.astype(BF16), pm_ref[g], preferred_element_type=F32))
    pool = jnp.concatenate(pooled, axis=1) * ps_ref[...]
    y_pool = jnp.dot((pool * zp_ref[...].astype(F32)).astype(BF16), wbp_ref[...],
                     preferred_element_type=F32)

    merged = ga_ref[...].astype(F32) * y_attn + gp_ref[...].astype(F32) * y_pool
    out_ref[...] = merged.astype(out_ref.dtype)


def _residual_epilogue(acc, cs, x_ref):
    return acc + x_ref[:, cs]


def _tail(os_, lses, rest, u, wba, wbp, pm, ps, batch, seq, *, gate_block0, za_block, zp_block):
    t, dm = u.shape[0], D_MODEL
    tm = PERM_TM
    tiles = seq // tm
    halo_per_tile = tm // POOL_HALO
    resident = functools.partial(pl.BlockSpec, pipeline_mode=pl.Buffered(1))
    row = lambda blk: (lambda i: (i, blk))
    grouped = lambda i: (i // tiles, 0, i % tiles, 0)
    gate_blocks = D_MODEL // ATTN_OUT
    unperms = [jnp.asarray(_deinterleave_matrix(d).T, BF16) for _, d in DILATED_GROUPS[1:]]
    o_in, l_in, o_specs, l_specs = [os_[0]], [lses[0]], [pl.BlockSpec((tm, ATTN_OUT), row(0))], \
        [pl.BlockSpec((tm, LANES), row(0))]
    for g in range(1, N_ATTN_GROUPS):
        d = DILATED_GROUPS[g][1]
        o_in.append(os_[g].reshape(batch, d, seq // d, ATTN_OUT))
        l_in.append(lses[g].reshape(batch, d, seq // d, LANES))
        o_specs.append(pl.BlockSpec((None, d, tm // d, ATTN_OUT), grouped))
        l_specs.append(pl.BlockSpec((None, d, tm // d, LANES), grouped))
    in_specs = (
        o_specs + l_specs
        + [resident((tm, tm), lambda i: (0, 0)), resident((tm, tm), lambda i: (0, 0)),
           pl.BlockSpec((tm, D_MODEL), row(gate_block0 // gate_blocks)),
           pl.BlockSpec((tm, D_MODEL), row(gate_block0 // gate_blocks + 1)),
           pl.BlockSpec((tm, ATTN_OUT), row(za_block)),
           pl.BlockSpec((tm, POOL_WIDTH), row(zp_block)),
           pl.BlockSpec((tm, POOL_WIDTH), row(0)),
           pl.BlockSpec((POOL_HALO, POOL_WIDTH),
                        lambda i: (jnp.maximum(i * halo_per_tile - 1, 0), 0)),
           resident(wba.shape, lambda i: (0, 0)),
           resident(wbp.shape, lambda i: (0, 0)),
           resident(pm.shape, lambda i: (0, 0, 0)),
           resident(ps.shape, lambda i: (0, 0))])
    return pl.pallas_call(
        functools.partial(_tail_kernel, tm=tm, seq=seq),
        grid=(t // tm,),
        in_specs=in_specs,
        out_specs=pl.BlockSpec((tm, dm), row(0)),
        out_shape=jax.ShapeDtypeStruct((t, dm), BF16),
        compiler_params=pltpu.CompilerParams(
            dimension_semantics=("arbitrary",),
            vmem_limit_bytes=VMEM_LIMIT_BYTES),
        name="merge_pool",
    )(*o_in, *l_in, *unperms, rest, rest, rest, rest, u, u, wba, wbp, pm, ps)


def kernel(x, norm_gain, w_in, b_gates, q_norm_gain, k_norm_gain, pool_maps, pool_scale,
           w_branch_attn, w_branch_pool, w_out):
    batch, seq, dm = x.shape
    t = batch * seq
    x2 = x.reshape(t, dm)
    hs = _rmsnorm(x2, norm_gain, batch, seq)

    lane_rep = lambda g: jnp.broadcast_to(g.astype(F32)[:, None], (HEAD_DIM, LANES))
    gains = jnp.stack([lane_rep(q_norm_gain * HEAD_DIM ** -0.5), lane_rep(k_norm_gain)])
    os_, lses = [], []
    for g, (_, d) in enumerate(DILATED_GROUPS):
        cos_t, sin_t = _rope_tables_t(seq, d)
        qkv_t = _project_qkv(hs[g], w_in, g, gains, cos_t, sin_t, seq)
        o_g, lse_g = _attention_group(qkv_t, g, batch, seq)
        os_.append(o_g)
        lses.append(lse_g)

    u = _project(hs[0], w_in, n_blocks=1, w_block_of=lambda j: j + 10,
                 epilogues=((0, _identity_epilogue),), out_dtype=F32, name="proj_u")

    n_gate_blocks = 2 * D_MODEL // PROJ_TN
    half_bias = (0.5 * b_gates.astype(F32)).reshape(1, 2 * D_MODEL)
    rest_w_block = lambda j: jnp.where(j < 4, j + 12, jnp.where(j == 4, 9, 11))
    rest = _project(
        hs[0], w_in, n_blocks=n_gate_blocks + 2, w_block_of=rest_w_block,
        epilogues=((0, _gate_epilogue), (n_gate_blocks, _silu_epilogue)),
        out_dtype=BF16, aux=(half_bias,),
        aux_specs=(pl.BlockSpec((1, PROJ_TN), lambda j, i: (0, jnp.minimum(j, n_gate_blocks - 1))),),
        name="proj_rest")

    merged = _tail(os_, lses, rest, u,
                   w_branch_attn.astype(BF16), w_branch_pool.astype(BF16), pool_maps.astype(BF16),
                   pool_scale.astype(F32).reshape(1, POOL_WIDTH), batch, seq,
                   gate_block0=0, za_block=4, zp_block=5)
    out = _project(merged, w_out, n_blocks=dm // PROJ_TN, w_block_of=lambda j: j,
                   epilogues=((0, _residual_epilogue),), out_dtype=F32, aux=(x2,),
                   aux_specs=(pl.BlockSpec((PROJ_TM, PROJ_TN), lambda j, i: (i, j)),),
                   name="proj_out")
    return out.reshape(batch, seq, dm)
```

```python
import functools

import numpy as np
import jax
import jax.numpy as jnp
from jax import lax
from jax.experimental import pallas as pl
from jax.experimental.pallas import tpu as pltpu

D_MODEL = 2048
HEAD_DIM = 128
HEADS_PER_GROUP = 8
DILATED_GROUPS = ((128, 1), (512, 4), (2048, 16))
N_ATTN_GROUPS = len(DILATED_GROUPS)
ATTN_OUT = HEADS_PER_GROUP * HEAD_DIM
BLK = 128
ROPE_THETA = 500000.0
ROT_DIM = HEAD_DIM // 4
ROT_HALF = ROT_DIM // 2
POOL_SIZES = (2, 4, 8, 16)
POOL_WIDTH = D_MODEL // 2
POOL_GROUP = POOL_WIDTH // len(POOL_SIZES)
NORM_EPS = 1e-6

LANES = 128
MXU_WIDTH = 256
VMEM_LIMIT_BYTES = 56 * 1024 * 1024

PROJ_TM = 1024
PROJ_TN = 1024
PERM_TM = 256
ATTN_TB = 256
POOL_HALO = 16

F32 = jnp.float32
BF16 = jnp.bfloat16

_NT = (((1,), (1,)), ((), ()))
_TN = (((0,), (0,)), ((), ()))


def _deinterleave_matrix(d):
    n = PERM_TM // d
    i = np.arange(PERM_TM)
    p = np.zeros((PERM_TM, PERM_TM), np.float32)
    p[i, (i % n) * d + i // n] = 1.0
    return p


def _rmsnorm_kernel(x_ref, g_ref, p4_ref, p16_ref, h1_ref, h4_ref, h16_ref):
    x = x_ref[...]
    ms = jnp.mean(x * x, axis=-1, keepdims=True)
    h = (x * lax.rsqrt(ms + NORM_EPS) * g_ref[...]).astype(BF16)
    h1_ref[...] = h
    for p_ref, out_ref in ((p4_ref, h4_ref), (p16_ref, h16_ref)):
        d, n = out_ref.shape[0], out_ref.shape[1]
        hp = jnp.dot(p_ref[...], h, preferred_element_type=F32).astype(BF16)
        for r in range(d):
            out_ref[r] = hp[r * n:(r + 1) * n, :]


def _rmsnorm(x2, gain, batch, seq):
    t, dm = x2.shape
    tiles = seq // PERM_TM
    perms = [jnp.asarray(_deinterleave_matrix(d), BF16) for _, d in DILATED_GROUPS[1:]]
    out_shape = [jax.ShapeDtypeStruct((t, dm), BF16)]
    out_specs = [pl.BlockSpec((PERM_TM, dm), lambda i: (i, 0))]
    for _, d in DILATED_GROUPS[1:]:
        out_shape.append(jax.ShapeDtypeStruct((batch, d, seq // d, dm), BF16))
        out_specs.append(pl.BlockSpec((None, d, PERM_TM // d, dm),
                                      lambda i: (i // tiles, 0, i % tiles, 0)))
    const = lambda i: (0, 0)
    h1, h4, h16 = pl.pallas_call(
        _rmsnorm_kernel,
        grid=(t // PERM_TM,),
        in_specs=[pl.BlockSpec((PERM_TM, dm), lambda i: (i, 0)),
                  pl.BlockSpec((1, dm), const),
                  pl.BlockSpec((PERM_TM, PERM_TM), const),
                  pl.BlockSpec((PERM_TM, PERM_TM), const)],
        out_specs=out_specs,
        out_shape=out_shape,
        compiler_params=pltpu.CompilerParams(dimension_semantics=("arbitrary",),
                                             vmem_limit_bytes=VMEM_LIMIT_BYTES),
        name="rmsnorm",
    )(x2, gain.reshape(1, dm), *perms)
    return h1, h4.reshape(t, dm), h16.reshape(t, dm)


def _qkv_kernel(h_ref, w_ref, gain_ref, cos_ref, sin_ref, out_ref, wt_ref, *, tm, tn):
    j = pl.program_id(0)

    @pl.when(pl.program_id(1) == 0)
    def _():
        for c in range(tn // MXU_WIDTH):
            cs = slice(c * MXU_WIDTH, (c + 1) * MXU_WIDTH)
            wt_ref[cs, :] = w_ref[:, cs].T.astype(BF16)

    def chunk(c):
        ts = slice(c * ATTN_TB, (c + 1) * ATTN_TB)
        return lax.dot_general(wt_ref[...], h_ref[ts, :], _NT, preferred_element_type=F32)

    @pl.when(j < 2)
    def _():
        for c in range(tm // ATTN_TB):
            acc = chunk(c)
            for hh in range(tn // HEAD_DIM):
                hs = slice(hh * HEAD_DIM, (hh + 1) * HEAD_DIM)
                for lt in range(ATTN_TB // LANES):
                    ls = slice(lt * LANES, (lt + 1) * LANES)
                    pos = slice(c * ATTN_TB + lt * LANES, c * ATTN_TB + (lt + 1) * LANES)
                    a = acc[hs, ls]
                    ss = jnp.sum(a * a, axis=0, keepdims=True)
                    r = lax.rsqrt(ss * (1.0 / HEAD_DIM) + NORM_EPS)
                    ag = a * gain_ref[...]
                    lo, hi = ag[0:ROT_HALF], ag[ROT_HALF:ROT_DIM]
                    cos, sin = cos_ref[:, pos], sin_ref[:, pos]
                    y = jnp.concatenate([lo * cos - hi * sin, hi * cos + lo * sin, ag[ROT_DIM:]], axis=0)
                    out_ref[c, hs, ls] = (y * r).astype(out_ref.dtype)

    @pl.when(j == 2)
    def _():
        for c in range(tm // ATTN_TB):
            out_ref[c] = chunk(c).astype(out_ref.dtype)


def _project_qkv(h, w_in, g, gains, cos_t, sin_t, seq):
    t, dm = h.shape
    tm, tn = PROJ_TM, PROJ_TN
    seq_tiles = seq // tm
    return pl.pallas_call(
        functools.partial(_qkv_kernel, tm=tm, tn=tn),
        grid=(3, t // tm),
        in_specs=[pl.BlockSpec((tm, dm), lambda j, i: (i, 0)),
                  pl.BlockSpec((dm, tn), lambda j, i: (0, j * N_ATTN_GROUPS + g)),
                  pl.BlockSpec((None, HEAD_DIM, LANES), lambda j, i: (jnp.minimum(j, 1), 0, 0)),
                  pl.BlockSpec((ROT_HALF, tm), lambda j, i: (0, i % seq_tiles)),
                  pl.BlockSpec((ROT_HALF, tm), lambda j, i: (0, i % seq_tiles))],
        out_specs=pl.BlockSpec((None, tm // ATTN_TB, tn, ATTN_TB), lambda j, i: (j, i, 0, 0)),
        out_shape=jax.ShapeDtypeStruct((3, t // ATTN_TB, tn, ATTN_TB), BF16),
        scratch_shapes=[pltpu.VMEM((tn, dm), BF16)],
        compiler_params=pltpu.CompilerParams(
            dimension_semantics=("arbitrary", "arbitrary"),
            vmem_limit_bytes=VMEM_LIMIT_BYTES),
        name=f"proj_qkv_g{g}",
    )(h, w_in, gains, cos_t, sin_t)


def _rope_tables_t(seq, d):
    inv_freq = ROPE_THETA ** (-jnp.arange(0, ROT_DIM, 2, dtype=F32) / ROT_DIM)
    pos = jnp.arange(seq, dtype=jnp.int32).reshape(seq // d, d).T.reshape(seq)
    ang = pos.astype(F32)[None, :] * inv_freq[:, None]
    return jnp.cos(ang), jnp.sin(ang)


def _identity_epilogue(acc, cs):
    del cs
    return acc


def _gate_epilogue(acc, cs, half_bias_ref):
    return 0.5 + 0.5 * jnp.tanh(0.5 * acc + half_bias_ref[:, cs])


def _silu_epilogue(acc, cs, half_bias_ref):
    del cs, half_bias_ref
    half = 0.5 * acc
    return half + half * jnp.tanh(half)


def _proj_kernel(*refs, epilogues, n_aux, tn):
    h_ref, w_ref = refs[0], refs[1]
    aux = refs[2:2 + n_aux]
    out_ref = refs[2 + n_aux]
    wbf_ref = refs[3 + n_aux]
    j = pl.program_id(0)

    @pl.when(pl.program_id(1) == 0)
    def _():
        wbf_ref[...] = w_ref[...].astype(BF16)

    def body(epilogue):
        for c in range(tn // MXU_WIDTH):
            cs = slice(c * MXU_WIDTH, (c + 1) * MXU_WIDTH)
            acc = jnp.dot(h_ref[...], wbf_ref[:, cs], preferred_element_type=F32)
            out_ref[:, cs] = epilogue(acc, cs, *aux).astype(out_ref.dtype)

    if len(epilogues) == 1:
        body(epilogues[0][1])
    else:
        bounds = [first for first, _ in epilogues[1:]] + [None]
        for (first, fn), last in zip(epilogues, bounds):
            cond = (j >= first) if last is None else ((j >= first) & (j < last))
            pl.when(cond)(functools.partial(body, fn))


def _project(h, w_in, *, n_blocks, w_block_of, epilogues, out_dtype, aux=(), aux_specs=(), name):
    t, d = h.shape
    tm, tn = PROJ_TM, PROJ_TN
    kern = functools.partial(_proj_kernel, epilogues=epilogues, n_aux=len(aux), tn=tn)
    return pl.pallas_call(
        kern,
        grid=(n_blocks, t // tm),
        in_specs=[pl.BlockSpec((tm, d), lambda j, i: (i, 0)),
                  pl.BlockSpec((d, tn), lambda j, i: (0, w_block_of(j))),
                  *aux_specs],
        out_specs=pl.BlockSpec((tm, tn), lambda j, i: (i, j)),
        out_shape=jax.ShapeDtypeStruct((t, n_blocks * tn), out_dtype),
        scratch_shapes=[pltpu.VMEM((d, tn), BF16)],
        compiler_params=pltpu.CompilerParams(
            dimension_semantics=("arbitrary", "arbitrary"),
            vmem_limit_bytes=VMEM_LIMIT_BYTES),
        name=name,
    )(h, w_in, *aux)


def _band_bias():
    i = np.arange(BLK)[:, None]
    j = np.arange(2 * BLK)[None, :]
    band = (j >= i) & (j <= i + BLK)
    first = band & (j >= BLK)
    return np.where(np.stack([band, first]), 0.0, -np.inf).astype(np.float32)


def _attn_kernel(bias_ref, q_ref, k_ref, v_ref, o_ref, lse_ref, kprev_ref, vprev_ref, *, blocks_per_sub):
    per = ATTN_TB // BLK
    step = pl.program_id(0)

    @pl.when(step == 0)
    def _():
        kprev_ref[...] = jnp.zeros_like(kprev_ref)
        vprev_ref[...] = jnp.zeros_like(vprev_ref)

    lane = lax.broadcasted_iota(jnp.int32, (BLK, LANES), 1)
    for b in range(per):
        ts = slice(b * BLK, (b + 1) * BLK)
        if b == 0:
            first = step % (blocks_per_sub // per) == 0
            bias = bias_ref[jnp.where(first, 1, 0)]
        else:
            bias = bias_ref[0]
        m_tile = jnp.zeros((BLK, LANES), F32)
        l_tile = jnp.ones((BLK, LANES), F32)
        for h in range(HEADS_PER_GROUP):
            cs = slice(h * HEAD_DIM, (h + 1) * HEAD_DIM)
            q_t = q_ref[cs, ts]
            if b == 0:
                kk = jnp.concatenate([kprev_ref[cs, :], k_ref[cs, 0:BLK]], axis=1)
                vv = jnp.concatenate([vprev_ref[cs, :], v_ref[cs, 0:BLK]], axis=1)
            else:
                kk = k_ref[cs, (b - 1) * BLK:(b + 1) * BLK]
                vv = v_ref[cs, (b - 1) * BLK:(b + 1) * BLK]
            s = lax.dot_general(q_t, kk, _TN, preferred_element_type=F32) + bias
            m = jnp.max(s, axis=-1, keepdims=True)
            p = jnp.exp(s - m)
            l = jnp.sum(p, axis=-1, keepdims=True)
            o = lax.dot_general(p.astype(BF16), vv, _NT, preferred_element_type=F32)
            o_ref[ts, cs] = (o / l).astype(o_ref.dtype)
            m_tile = jnp.where(lane == h, m, m_tile)
            l_tile = jnp.where(lane == h, l, l_tile)
        lse_ref[ts, :] = m_tile + jnp.log(l_tile)
    kprev_ref[...] = k_ref[:, (per - 1) * BLK:per * BLK]
    vprev_ref[...] = v_ref[:, (per - 1) * BLK:per * BLK]


def _attention_group(qkv_t, g, batch, seq):
    _, d = DILATED_GROUPS[g]
    blocks_per_sub = seq // d // BLK
    t = batch * seq
    tb = ATTN_TB
    tile = lambda part: pl.BlockSpec((None, None, ATTN_OUT, tb), lambda s: (part, s, 0, 0))
    return pl.pallas_call(
        functools.partial(_attn_kernel, blocks_per_sub=blocks_per_sub),
        grid=(t // tb,),
        in_specs=[
            pl.BlockSpec((2, BLK, 2 * BLK), lambda s: (0, 0, 0), pipeline_mode=pl.Buffered(1)),
            tile(0), tile(1), tile(2),
        ],
        out_specs=[
            pl.BlockSpec((tb, ATTN_OUT), lambda s: (s, 0)),
            pl.BlockSpec((tb, LANES), lambda s: (s, 0)),
        ],
        out_shape=[jax.ShapeDtypeStruct((t, ATTN_OUT), BF16),
                   jax.ShapeDtypeStruct((t, LANES), F32)],
        scratch_shapes=[pltpu.VMEM((ATTN_OUT, BLK), BF16), pltpu.VMEM((ATTN_OUT, BLK), BF16)],
        compiler_params=pltpu.CompilerParams(
            dimension_semantics=("arbitrary",),
            vmem_limit_bytes=VMEM_LIMIT_BYTES),
        name=f"window_attention_g{g}",
    )(jnp.asarray(_band_bias()), qkv_t, qkv_t, qkv_t)


def _rows(ref):
    return jnp.concatenate([ref[r] for r in range(ref.shape[0])], axis=0)


def _unpermute_f32(q, x):
    hi = x.astype(BF16)
    r1 = x - hi.astype(F32)
    mid = r1.astype(BF16)
    lo = (r1 - mid.astype(F32)).astype(BF16)
    dot = lambda v: jnp.dot(q, v, preferred_element_type=F32)
    return (dot(hi) + dot(mid)) + dot(lo)


def _tail_kernel(o0_ref, o1_ref, o2_ref, l0_ref, l1_ref, l2_ref, q4_ref, q16_ref, ga_ref, gp_ref,
                 za_ref, zp_ref, u_ref, uh_ref, wba_ref, wbp_ref, pm_ref, ps_ref,
                 out_ref, a_ref, b_ref, *, tm, seq, n_tiles):
    step = pl.program_id(0)

    @pl.when(step == 0)
    def _():
        a_ref[...] = jnp.zeros_like(a_ref)
        b_ref[...] = jnp.zeros_like(b_ref)

    body = functools.partial(
        _tail_step, o0_ref, o1_ref, o2_ref, l0_ref, l1_ref, l2_ref, q4_ref, q16_ref, ga_ref, gp_ref,
        za_ref, zp_ref, u_ref, uh_ref, wba_ref, wbp_ref, pm_ref, ps_ref, out_ref,
        pos0=(jnp.minimum(step, n_tiles - 1) * tm) % seq, tm=tm)
    for parity in range(2):
        pl.when(step % 2 == parity)(
            functools.partial(body, a_ref.at[1 - parity], b_ref.at[1 - parity],
                              a_ref.at[parity], b_ref.at[parity]))


def _tail_step(o0_ref, o1_ref, o2_ref, l0_ref, l1_ref, l2_ref, q4_ref, q16_ref, ga_ref, gp_ref,
               za_ref, zp_ref, u_ref, uh_ref, wba_ref, wbp_ref, pm_ref, ps_ref, out_ref,
               a_in_ref, b_in_ref, a_ref, b_ref, *, pos0, tm):
    q4, q16 = q4_ref[...], q16_ref[...]
    o0 = o0_ref[...]
    o1 = jnp.dot(q4, _rows(o1_ref), preferred_element_type=F32)
    o2 = jnp.dot(q16, _rows(o2_ref), preferred_element_type=F32)
    l0 = l0_ref[...]
    l1 = _unpermute_f32(q4, _rows(l1_ref))
    l2 = _unpermute_f32(q16, _rows(l2_ref))

    y_attn = jnp.dot(a_in_ref[...], wba_ref[...], preferred_element_type=F32)
    y_pool = jnp.dot(b_in_ref[...], wbp_ref[...], preferred_element_type=F32)
    merged = ga_ref[...].astype(F32) * y_attn + gp_ref[...].astype(F32) * y_pool
    out_ref[...] = merged.astype(out_ref.dtype)

    mx = jnp.maximum(jnp.maximum(l0, l1), l2)
    e0, e1, e2 = jnp.exp(l0 - mx), jnp.exp(l1 - mx), jnp.exp(l2 - mx)
    inv = 1.0 / (e0 + e1 + e2)
    w0, w1, w2 = e0 * inv, e1 * inv, e2 * inv
    for h in range(HEADS_PER_GROUP):
        cs = slice(h * HEAD_DIM, (h + 1) * HEAD_DIM)
        a = (w0[:, h:h + 1] * o0[:, cs].astype(F32)
             + w1[:, h:h + 1] * o1[:, cs]
             + w2[:, h:h + 1] * o2[:, cs])
        a_ref[:, cs] = (a * za_ref[:, cs].astype(F32)).astype(BF16)

    halo = uh_ref[...]
    halo = jnp.where(pos0 == 0, jnp.zeros_like(halo), halo)
    pos = pos0 + lax.broadcasted_iota(jnp.int32, (tm, 1), 0)
    pooled = []
    for g, ksz in enumerate(POOL_SIZES):
        cs = slice(g * POOL_GROUP, (g + 1) * POOL_GROUP)
        ext = jnp.concatenate([halo[:, cs], u_ref[:, cs]], axis=0)
        win, shift = ext, 1
        while shift < ksz:
            win = win + pltpu.roll(win, shift, 0)
            shift *= 2
        u = ext[POOL_HALO:]
        cnt = jnp.minimum(pos + 1, ksz).astype(F32)
        dlt = win[POOL_HALO:] / cnt - u
        pooled.append(jnp.dot(dlt.astype(BF16), pm_ref[g], preferred_element_type=F32))
    pool = jnp.concatenate(pooled, axis=1) * ps_ref[...]
    b_ref[...] = (pool * zp_ref[...].astype(F32)).astype(BF16)


def _residual_epilogue(acc, cs, x_ref):
    return acc + x_ref[:, cs]


def _tail(os_, lses, rest, u, wba, wbp, pm, ps, batch, seq, *, gate_block0, za_block, zp_block):
    t, dm = u.shape[0], D_MODEL
    tm = PERM_TM
    tiles = seq // tm
    n_tiles = t // tm
    halo_per_tile = tm // POOL_HALO
    resident = functools.partial(pl.BlockSpec, pipeline_mode=pl.Buffered(1))
    cur = lambda i: jnp.minimum(i, n_tiles - 1)
    prv = lambda i: jnp.maximum(i - 1, 0)
    row = lambda blk: (lambda i: (cur(i), blk))
    row_prv = lambda blk: (lambda i: (prv(i), blk))
    grouped = lambda i: (cur(i) // tiles, 0, cur(i) % tiles, 0)
    gate_blocks = D_MODEL // ATTN_OUT
    unperms = [jnp.asarray(_deinterleave_matrix(d).T, BF16) for _, d in DILATED_GROUPS[1:]]
    o_in, l_in, o_specs, l_specs = [os_[0]], [lses[0]], [pl.BlockSpec((tm, ATTN_OUT), row(0))], \
        [pl.BlockSpec((tm, LANES), row(0))]
    for g in range(1, N_ATTN_GROUPS):
        d = DILATED_GROUPS[g][1]
        o_in.append(os_[g].reshape(batch, d, seq // d, ATTN_OUT))
        l_in.append(lses[g].reshape(batch, d, seq // d, LANES))
        o_specs.append(pl.BlockSpec((None, d, tm // d, ATTN_OUT), grouped))
        l_specs.append(pl.BlockSpec((None, d, tm // d, LANES), grouped))
    in_specs = (
        o_specs + l_specs
        + [resident((tm, tm), lambda i: (0, 0)), resident((tm, tm), lambda i: (0, 0)),
           pl.BlockSpec((tm, D_MODEL), row_prv(gate_block0 // gate_blocks)),
           pl.BlockSpec((tm, D_MODEL), row_prv(gate_block0 // gate_blocks + 1)),
           pl.BlockSpec((tm, ATTN_OUT), row(za_block)),
           pl.BlockSpec((tm, POOL_WIDTH), row(zp_block)),
           pl.BlockSpec((tm, POOL_WIDTH), row(0)),
           pl.BlockSpec((POOL_HALO, POOL_WIDTH),
                        lambda i: (jnp.maximum(cur(i) * halo_per_tile - 1, 0), 0)),
           resident(wba.shape, lambda i: (0, 0)),
           resident(wbp.shape, lambda i: (0, 0)),
           resident(pm.shape, lambda i: (0, 0, 0)),
           resident(ps.shape, lambda i: (0, 0))])
    return pl.pallas_call(
        functools.partial(_tail_kernel, tm=tm, seq=seq, n_tiles=n_tiles),
        grid=(n_tiles + 1,),
        in_specs=in_specs,
        out_specs=pl.BlockSpec((tm, dm), row_prv(0)),
        out_shape=jax.ShapeDtypeStruct((t, dm), BF16),
        scratch_shapes=[pltpu.VMEM((2, tm, ATTN_OUT), BF16), pltpu.VMEM((2, tm, POOL_WIDTH), BF16)],
        compiler_params=pltpu.CompilerParams(
            dimension_semantics=("arbitrary",),
            vmem_limit_bytes=VMEM_LIMIT_BYTES),
        name="merge_pool",
    )(*o_in, *l_in, *unperms, rest, rest, rest, rest, u, u, wba, wbp, pm, ps)


def kernel(x, norm_gain, w_in, b_gates, q_norm_gain, k_norm_gain, pool_maps, pool_scale,
           w_branch_attn, w_branch_pool, w_out):
    batch, seq, dm = x.shape
    t = batch * seq
    x2 = x.reshape(t, dm)
    hs = _rmsnorm(x2, norm_gain, batch, seq)

    lane_rep = lambda g: jnp.broadcast_to(g.astype(F32)[:, None], (HEAD_DIM, LANES))
    gains = jnp.stack([lane_rep(q_norm_gain * HEAD_DIM ** -0.5), lane_rep(k_norm_gain)])
    os_, lses = [], []
    for g, (_, d) in enumerate(DILATED_GROUPS):
        cos_t, sin_t = _rope_tables_t(seq, d)
        qkv_t = _project_qkv(hs[g], w_in, g, gains, cos_t, sin_t, seq)
        o_g, lse_g = _attention_group(qkv_t, g, batch, seq)
        os_.append(o_g)
        lses.append(lse_g)

    u = _project(hs[0], w_in, n_blocks=1, w_block_of=lambda j: j + 10,
                 epilogues=((0, _identity_epilogue),), out_dtype=F32, name="proj_u")

    n_gate_blocks = 2 * D_MODEL // PROJ_TN
    half_bias = (0.5 * b_gates.astype(F32)).reshape(1, 2 * D_MODEL)
    rest_w_block = lambda j: jnp.where(j < 4, j + 12, jnp.where(j == 4, 9, 11))
    rest = _project(
        hs[0], w_in, n_blocks=n_gate_blocks + 2, w_block_of=rest_w_block,
        epilogues=((0, _gate_epilogue), (n_gate_blocks, _silu_epilogue)),
        out_dtype=BF16, aux=(half_bias,),
        aux_specs=(pl.BlockSpec((1, PROJ_TN), lambda j, i: (0, jnp.minimum(j, n_gate_blocks - 1))),),
        name="proj_rest")

    merged = _tail(os_, lses, rest, u,
                   w_branch_attn.astype(BF16), w_branch_pool.astype(BF16), pool_maps.astype(BF16),
                   pool_scale.astype(F32).reshape(1, POOL_WIDTH), batch, seq,
                   gate_block0=0, za_block=4, zp_block=5)
    out = _project(merged, w_out, n_blocks=dm // PROJ_TN, w_block_of=lambda j: j,
                   epilogues=((0, _residual_epilogue),), out_dtype=F32, aux=(x2,),
                   aux_specs=(pl.BlockSpec((PROJ_TM, PROJ_TN), lambda j, i: (i, j)),),
                   name="proj_out")
    return out.reshape(batch, seq, dm)
```

```python
import functools

import numpy as np
import jax
import jax.numpy as jnp
from jax import lax
from jax.experimental import pallas as pl
from jax.experimental.pallas import tpu as pltpu

D_MODEL = 2048
HEAD_DIM = 128
HEADS_PER_GROUP = 8
DILATED_GROUPS = ((128, 1), (512, 4), (2048, 16))
N_ATTN_GROUPS = len(DILATED_GROUPS)
ATTN_OUT = HEADS_PER_GROUP * HEAD_DIM
BLK = 128
ROPE_THETA = 500000.0
ROT_DIM = HEAD_DIM // 4
ROT_HALF = ROT_DIM // 2
POOL_SIZES = (2, 4, 8, 16)
POOL_WIDTH = D_MODEL // 2
POOL_GROUP = POOL_WIDTH // len(POOL_SIZES)
NORM_EPS = 1e-6

LANES = 128
MXU_WIDTH = 256
VMEM_LIMIT_BYTES = 56 * 1024 * 1024

PROJ_TM = 1024
PROJ_TN = 1024
PERM_TM = 256
ATTN_TB = 256
POOL_HALO = 16

F32 = jnp.float32
BF16 = jnp.bfloat16

_NT = (((1,), (1,)), ((), ()))
_TN = (((0,), (0,)), ((), ()))


def _deinterleave_matrix(d):
    n = PERM_TM // d
    i = np.arange(PERM_TM)
    p = np.zeros((PERM_TM, PERM_TM), np.float32)
    p[i, (i % n) * d + i // n] = 1.0
    return p


def _rmsnorm_kernel(x_ref, g_ref, p4_ref, p16_ref, h1_ref, h4_ref, h16_ref):
    x = x_ref[...]
    ms = jnp.mean(x * x, axis=-1, keepdims=True)
    h = (x * lax.rsqrt(ms + NORM_EPS) * g_ref[...]).astype(BF16)
    h1_ref[...] = h
    for p_ref, out_ref in ((p4_ref, h4_ref), (p16_ref, h16_ref)):
        d, n = out_ref.shape[0], out_ref.shape[1]
        hp = jnp.dot(p_ref[...], h, preferred_element_type=F32).astype(BF16)
        for r in range(d):
            out_ref[r] = hp[r * n:(r + 1) * n, :]


def _rmsnorm(x2, gain, batch, seq):
    t, dm = x2.shape
    tiles = seq // PERM_TM
    perms = [jnp.asarray(_deinterleave_matrix(d), BF16) for _, d in DILATED_GROUPS[1:]]
    out_shape = [jax.ShapeDtypeStruct((t, dm), BF16)]
    out_specs = [pl.BlockSpec((PERM_TM, dm), lambda i: (i, 0))]
    for _, d in DILATED_GROUPS[1:]:
        out_shape.append(jax.ShapeDtypeStruct((batch, d, seq // d, dm), BF16))
        out_specs.append(pl.BlockSpec((None, d, PERM_TM // d, dm),
                                      lambda i: (i // tiles, 0, i % tiles, 0)))
    const = lambda i: (0, 0)
    h1, h4, h16 = pl.pallas_call(
        _rmsnorm_kernel,
        grid=(t // PERM_TM,),
        in_specs=[pl.BlockSpec((PERM_TM, dm), lambda i: (i, 0)),
                  pl.BlockSpec((1, dm), const),
                  pl.BlockSpec((PERM_TM, PERM_TM), const),
                  pl.BlockSpec((PERM_TM, PERM_TM), const)],
        out_specs=out_specs,
        out_shape=out_shape,
        compiler_params=pltpu.CompilerParams(dimension_semantics=("arbitrary",),
                                             vmem_limit_bytes=VMEM_LIMIT_BYTES),
        name="rmsnorm",
    )(x2, gain.reshape(1, dm), *perms)
    return h1, h4.reshape(t, dm), h16.reshape(t, dm)


def _qkv_kernel(h_ref, w_ref, gain_ref, cos_ref, sin_ref, out_ref, wt_ref, *, tm, tn):
    j = pl.program_id(0)

    @pl.when(pl.program_id(1) == 0)
    def _():
        for c in range(tn // MXU_WIDTH):
            cs = slice(c * MXU_WIDTH, (c + 1) * MXU_WIDTH)
            wt_ref[cs, :] = w_ref[:, cs].T.astype(BF16)

    def chunk(c):
        ts = slice(c * ATTN_TB, (c + 1) * ATTN_TB)
        return lax.dot_general(wt_ref[...], h_ref[ts, :], _NT, preferred_element_type=F32)

    @pl.when(j < 2)
    def _():
        for c in range(tm // ATTN_TB):
            acc = chunk(c)
            for hh in range(tn // HEAD_DIM):
                hs = slice(hh * HEAD_DIM, (hh + 1) * HEAD_DIM)
                for lt in range(ATTN_TB // LANES):
                    ls = slice(lt * LANES, (lt + 1) * LANES)
                    pos = slice(c * ATTN_TB + lt * LANES, c * ATTN_TB + (lt + 1) * LANES)
                    a = acc[hs, ls]
                    ss = jnp.sum(a * a, axis=0, keepdims=True)
                    r = lax.rsqrt(ss * (1.0 / HEAD_DIM) + NORM_EPS)
                    ag = a * gain_ref[...]
                    lo, hi = ag[0:ROT_HALF], ag[ROT_HALF:ROT_DIM]
                    cos, sin = cos_ref[:, pos], sin_ref[:, pos]
                    y = jnp.concatenate([lo * cos - hi * sin, hi * cos + lo * sin, ag[ROT_DIM:]], axis=0)
                    out_ref[c, hs, ls] = (y * r).astype(out_ref.dtype)

    @pl.when(j == 2)
    def _():
        for c in range(tm // ATTN_TB):
            out_ref[c] = chunk(c).astype(out_ref.dtype)


def _qkv_kernel_into(h_ref, w_ref, gain_ref, cos_ref, sin_ref, buf_ref, out_ref, wt_ref, **kw):
    del buf_ref
    _qkv_kernel(h_ref, w_ref, gain_ref, cos_ref, sin_ref, out_ref, wt_ref, **kw)


def _project_qkv(h, w_in, g, gains, cos_t, sin_t, seq, buf):
    t, dm = h.shape
    tm, tn = PROJ_TM, PROJ_TN
    seq_tiles = seq // tm
    in_specs = [pl.BlockSpec((tm, dm), lambda j, i: (i, 0)),
                pl.BlockSpec((dm, tn), lambda j, i: (0, j * N_ATTN_GROUPS + g)),
                pl.BlockSpec((None, HEAD_DIM, LANES), lambda j, i: (jnp.minimum(j, 1), 0, 0)),
                pl.BlockSpec((ROT_HALF, tm), lambda j, i: (0, i % seq_tiles)),
                pl.BlockSpec((ROT_HALF, tm), lambda j, i: (0, i % seq_tiles))]
    args = [h, w_in, gains, cos_t, sin_t]
    if buf is None:
        body, aliases = _qkv_kernel, {}
    else:
        body, aliases = _qkv_kernel_into, {len(args): 0}
        in_specs.append(pl.BlockSpec(memory_space=pl.ANY))
        args.append(buf)
    return pl.pallas_call(
        functools.partial(body, tm=tm, tn=tn),
        grid=(3, t // tm),
        in_specs=in_specs,
        out_specs=pl.BlockSpec((None, None, tm // ATTN_TB, tn, ATTN_TB), lambda j, i: (g, j, i, 0, 0)),
        out_shape=jax.ShapeDtypeStruct((N_ATTN_GROUPS, 3, t // ATTN_TB, tn, ATTN_TB), BF16),
        scratch_shapes=[pltpu.VMEM((tn, dm), BF16)],
        input_output_aliases=aliases,
        compiler_params=pltpu.CompilerParams(
            dimension_semantics=("arbitrary", "arbitrary"),
            vmem_limit_bytes=VMEM_LIMIT_BYTES),
        name=f"proj_qkv_g{g}",
    )(*args)


def _rope_tables_t(seq, d):
    inv_freq = ROPE_THETA ** (-jnp.arange(0, ROT_DIM, 2, dtype=F32) / ROT_DIM)
    pos = jnp.arange(seq, dtype=jnp.int32).reshape(seq // d, d).T.reshape(seq)
    ang = pos.astype(F32)[None, :] * inv_freq[:, None]
    return jnp.cos(ang), jnp.sin(ang)


def _identity_epilogue(acc, cs):
    del cs
    return acc


def _gate_epilogue(acc, cs, half_bias_ref):
    return 0.5 + 0.5 * jnp.tanh(0.5 * acc + half_bias_ref[:, cs])


def _silu_epilogue(acc, cs, half_bias_ref):
    del cs, half_bias_ref
    half = 0.5 * acc
    return half + half * jnp.tanh(half)


def _proj_kernel(*refs, epilogues, n_aux, tn):
    h_ref, w_ref = refs[0], refs[1]
    aux = refs[2:2 + n_aux]
    out_ref = refs[2 + n_aux]
    wbf_ref = refs[3 + n_aux]
    j = pl.program_id(0)

    @pl.when(pl.program_id(1) == 0)
    def _():
        wbf_ref[...] = w_ref[...].astype(BF16)

    def body(epilogue):
        for c in range(tn // MXU_WIDTH):
            cs = slice(c * MXU_WIDTH, (c + 1) * MXU_WIDTH)
            acc = jnp.dot(h_ref[...], wbf_ref[:, cs], preferred_element_type=F32)
            out_ref[:, cs] = epilogue(acc, cs, *aux).astype(out_ref.dtype)

    if len(epilogues) == 1:
        body(epilogues[0][1])
    else:
        bounds = [first for first, _ in epilogues[1:]] + [None]
        for (first, fn), last in zip(epilogues, bounds):
            cond = (j >= first) if last is None else ((j >= first) & (j < last))
            pl.when(cond)(functools.partial(body, fn))


def _project(h, w_in, *, n_blocks, w_block_of, epilogues, out_dtype, aux=(), aux_specs=(), name):
    t, d = h.shape
    tm, tn = PROJ_TM, PROJ_TN
    kern = functools.partial(_proj_kernel, epilogues=epilogues, n_aux=len(aux), tn=tn)
    return pl.pallas_call(
        kern,
        grid=(n_blocks, t // tm),
        in_specs=[pl.BlockSpec((tm, d), lambda j, i: (i, 0)),
                  pl.BlockSpec((d, tn), lambda j, i: (0, w_block_of(j))),
                  *aux_specs],
        out_specs=pl.BlockSpec((tm, tn), lambda j, i: (i, j)),
        out_shape=jax.ShapeDtypeStruct((t, n_blocks * tn), out_dtype),
        scratch_shapes=[pltpu.VMEM((d, tn), BF16)],
        compiler_params=pltpu.CompilerParams(
            dimension_semantics=("arbitrary", "arbitrary"),
            vmem_limit_bytes=VMEM_LIMIT_BYTES),
        name=name,
    )(h, w_in, *aux)


def _band_bias():
    i = np.arange(BLK)[:, None]
    j = np.arange(2 * BLK)[None, :]
    band = (j >= i) & (j <= i + BLK)
    first = band & (j >= BLK)
    return np.where(np.stack([band, first]), 0.0, -np.inf).astype(np.float32)


class _AttentionBlocks:
    def __init__(self, bias_ref, q_ref, k_ref, v_ref, o_ref, lse_ref, kprev_ref, vprev_ref, firsts):
        self.bias_ref, self.q_ref, self.k_ref, self.v_ref = bias_ref, q_ref, k_ref, v_ref
        self.o_ref, self.lse_ref, self.kprev_ref, self.vprev_ref = o_ref, lse_ref, kprev_ref, vprev_ref
        self.firsts = firsts
        self.per = ATTN_TB // BLK
        self.scores = {}

    def _block(self, ref, n, cs):
        lo = (n % self.per) * BLK
        return ref[n // self.per, cs, lo:lo + BLK]

    def _with_previous(self, ref, prev_ref, n, cs):
        prev = prev_ref[cs, :] if n == 0 else self._block(ref, n - 1, cs)
        return jnp.concatenate([prev, self._block(ref, n, cs)], axis=1)

    def score_phase(self, n):
        first = self.firsts[n]
        bias = self.bias_ref[int(first)] if isinstance(first, bool) else self.bias_ref[jnp.where(first, 1, 0)]
        for h in range(HEADS_PER_GROUP):
            cs = slice(h * HEAD_DIM, (h + 1) * HEAD_DIM)
            kk = self._with_previous(self.k_ref, self.kprev_ref, n, cs)
            s = lax.dot_general(self._block(self.q_ref, n, cs), kk, _TN, preferred_element_type=F32)
            self.scores[n, h] = s + bias

    def value_phase(self, n):
        ts = slice(n * BLK, (n + 1) * BLK)
        lane = lax.broadcasted_iota(jnp.int32, (BLK, LANES), 1)
        m_tile = jnp.zeros((BLK, LANES), F32)
        l_tile = jnp.ones((BLK, LANES), F32)
        for h in range(HEADS_PER_GROUP):
            cs = slice(h * HEAD_DIM, (h + 1) * HEAD_DIM)
            s = self.scores.pop((n, h))
            m = jnp.max(s, axis=-1, keepdims=True)
            p = jnp.exp(s - m)
            l = jnp.sum(p, axis=-1, keepdims=True)
            vv = self._with_previous(self.v_ref, self.vprev_ref, n, cs)
            o = lax.dot_general(p.astype(BF16), vv, _NT, preferred_element_type=F32)
            self.o_ref[ts, cs] = (o / l).astype(self.o_ref.dtype)
            m_tile = jnp.where(lane == h, m, m_tile)
            l_tile = jnp.where(lane == h, l, l_tile)
        self.lse_ref[ts, :] = m_tile + jnp.log(l_tile)

    def carry(self, n_blocks):
        self.kprev_ref[...] = self._block(self.k_ref, n_blocks - 1, slice(None))
        self.vprev_ref[...] = self._block(self.v_ref, n_blocks - 1, slice(None))


def _rest_attn_kernel(h_ref, w_ref, half_bias_ref, bias_ref, q_ref, k_ref, v_ref,
                      out_ref, o_ref, lse_ref, wbf_ref, kprev_ref, vprev_ref,
                      *, tn, n_gate_blocks, steps_per_group, tiles_per_sub):
    j, i = pl.program_id(0), pl.program_id(1)
    step = j * pl.num_programs(1) + i

    @pl.when(i == 0)
    def _():
        wbf_ref[...] = w_ref[...].astype(BF16)

    @pl.when(step == 0)
    def _():
        kprev_ref[...] = jnp.zeros_like(kprev_ref)
        vprev_ref[...] = jnp.zeros_like(vprev_ref)

    n_blocks = tn // MXU_WIDTH
    per = ATTN_TB // BLK
    tiles_per_step = n_blocks // per
    group = step // steps_per_group
    tps = functools.reduce(lambda acc, g: jnp.where(group == g, tiles_per_sub[g], acc),
                           range(len(tiles_per_sub)), tiles_per_sub[0])
    tile0 = (step % steps_per_group) * tiles_per_step
    firsts = [((tile0 + n // per) & (tps - 1)) == 0 if n % per == 0 else False for n in range(n_blocks)]

    def body(epilogue):
        attn = _AttentionBlocks(bias_ref, q_ref, k_ref, v_ref, o_ref, lse_ref, kprev_ref, vprev_ref, firsts)
        for c in range(n_blocks):
            cs = slice(c * MXU_WIDTH, (c + 1) * MXU_WIDTH)
            attn.score_phase(c)
            acc = jnp.dot(h_ref[...], wbf_ref[:, cs], preferred_element_type=F32)
            attn.value_phase(c)
            out_ref[:, cs] = epilogue(acc, cs, half_bias_ref).astype(out_ref.dtype)
        attn.carry(n_blocks)

    pl.when(j < n_gate_blocks)(functools.partial(body, _gate_epilogue))
    pl.when(j >= n_gate_blocks)(functools.partial(body, _silu_epilogue))


def _project_rest_with_attention(h, w_in, half_bias, qkv_t, seq, *, n_gate_blocks, w_block_of):
    t, dm = h.shape
    tm, tn = PROJ_TM, PROJ_TN
    n_blocks = n_gate_blocks + 2
    rows = t // tm
    n_groups, _, n_tiles = qkv_t.shape[:3]
    blocks_per_step = tn // MXU_WIDTH
    tiles_per_step = blocks_per_step * BLK // ATTN_TB
    steps_per_group = n_tiles // tiles_per_step
    assert n_groups * steps_per_group == n_blocks * rows, "attention tiles must fill the projection steps"
    tiles_per_sub = tuple(seq // d // ATTN_TB for _, d in DILATED_GROUPS)
    step = lambda j, i: j * rows + i
    tile = lambda part: pl.BlockSpec(
        (None, None, tiles_per_step, ATTN_OUT, ATTN_TB),
        lambda j, i: (step(j, i) // steps_per_group, part, step(j, i) % steps_per_group, 0, 0))
    tokens = tiles_per_step * ATTN_TB
    per_group = lambda width: pl.BlockSpec(
        (None, tokens, width), lambda j, i: (step(j, i) // steps_per_group, step(j, i) % steps_per_group, 0))
    return pl.pallas_call(
        functools.partial(_rest_attn_kernel, tn=tn, n_gate_blocks=n_gate_blocks,
                          steps_per_group=steps_per_group, tiles_per_sub=tiles_per_sub),
        grid=(n_blocks, rows),
        in_specs=[pl.BlockSpec((tm, dm), lambda j, i: (i, 0)),
                  pl.BlockSpec((dm, tn), lambda j, i: (0, w_block_of(j))),
                  pl.BlockSpec((1, tn), lambda j, i: (0, jnp.minimum(j, n_gate_blocks - 1))),
                  pl.BlockSpec((2, BLK, 2 * BLK), lambda j, i: (0, 0, 0), pipeline_mode=pl.Buffered(1)),
                  tile(0), tile(1), tile(2)],
        out_specs=[pl.BlockSpec((tm, tn), lambda j, i: (i, j)),
                   per_group(ATTN_OUT), per_group(LANES)],
        out_shape=[jax.ShapeDtypeStruct((t, n_blocks * tn), BF16),
                   jax.ShapeDtypeStruct((n_groups, t, ATTN_OUT), BF16),
                   jax.ShapeDtypeStruct((n_groups, t, LANES), F32)],
        scratch_shapes=[pltpu.VMEM((dm, tn), BF16),
                        pltpu.VMEM((ATTN_OUT, BLK), BF16), pltpu.VMEM((ATTN_OUT, BLK), BF16)],
        compiler_params=pltpu.CompilerParams(
            dimension_semantics=("arbitrary", "arbitrary"),
            vmem_limit_bytes=VMEM_LIMIT_BYTES),
        name="proj_rest_attention",
    )(h, w_in, half_bias, jnp.asarray(_band_bias()), qkv_t, qkv_t, qkv_t)


def _rows(ref):
    return jnp.concatenate([ref[r] for r in range(ref.shape[0])], axis=0)


def _unpermute_f32(q, x):
    hi = x.astype(BF16)
    r1 = x - hi.astype(F32)
    mid = r1.astype(BF16)
    lo = (r1 - mid.astype(F32)).astype(BF16)
    dot = lambda v: jnp.dot(q, v, preferred_element_type=F32)
    return (dot(hi) + dot(mid)) + dot(lo)


def _tail_kernel(o0_ref, o1_ref, o2_ref, l0_ref, l1_ref, l2_ref, q4_ref, q16_ref, ga_ref, gp_ref,
                 za_ref, zp_ref, u_ref, uh_ref, wba_ref, wbp_ref, pm_ref, ps_ref,
                 out_ref, a_ref, b_ref, *, tm, seq, n_tiles):
    step = pl.program_id(0)

    @pl.when(step == 0)
    def _():
        a_ref[...] = jnp.zeros_like(a_ref)
        b_ref[...] = jnp.zeros_like(b_ref)

    body = functools.partial(
        _tail_step, o0_ref, o1_ref, o2_ref, l0_ref, l1_ref, l2_ref, q4_ref, q16_ref, ga_ref, gp_ref,
        za_ref, zp_ref, u_ref, uh_ref, wba_ref, wbp_ref, pm_ref, ps_ref, out_ref,
        pos0=(jnp.minimum(step, n_tiles - 1) * tm) % seq, tm=tm)
    for parity in range(2):
        pl.when(step % 2 == parity)(
            functools.partial(body, a_ref.at[1 - parity], b_ref.at[1 - parity],
                              a_ref.at[parity], b_ref.at[parity]))


def _tail_step(o0_ref, o1_ref, o2_ref, l0_ref, l1_ref, l2_ref, q4_ref, q16_ref, ga_ref, gp_ref,
               za_ref, zp_ref, u_ref, uh_ref, wba_ref, wbp_ref, pm_ref, ps_ref, out_ref,
               a_in_ref, b_in_ref, a_ref, b_ref, *, pos0, tm):
    q4, q16 = q4_ref[...], q16_ref[...]
    o0 = o0_ref[...]
    o1 = jnp.dot(q4, _rows(o1_ref), preferred_element_type=F32)
    o2 = jnp.dot(q16, _rows(o2_ref), preferred_element_type=F32)
    l0 = l0_ref[...]
    l1 = _unpermute_f32(q4, _rows(l1_ref))
    l2 = _unpermute_f32(q16, _rows(l2_ref))

    y_attn = jnp.dot(a_in_ref[...], wba_ref[...], preferred_element_type=F32)
    y_pool = jnp.dot(b_in_ref[...], wbp_ref[...], preferred_element_type=F32)
    merged = ga_ref[...].astype(F32) * y_attn + gp_ref[...].astype(F32) * y_pool
    out_ref[...] = merged.astype(out_ref.dtype)

    mx = jnp.maximum(jnp.maximum(l0, l1), l2)
    e0, e1, e2 = jnp.exp(l0 - mx), jnp.exp(l1 - mx), jnp.exp(l2 - mx)
    inv = 1.0 / (e0 + e1 + e2)
    w0, w1, w2 = e0 * inv, e1 * inv, e2 * inv
    for h in range(HEADS_PER_GROUP):
        cs = slice(h * HEAD_DIM, (h + 1) * HEAD_DIM)
        a = (w0[:, h:h + 1] * o0[:, cs].astype(F32)
             + w1[:, h:h + 1] * o1[:, cs]
             + w2[:, h:h + 1] * o2[:, cs])
        a_ref[:, cs] = (a * za_ref[:, cs].astype(F32)).astype(BF16)

    halo = uh_ref[...]
    halo = jnp.where(pos0 == 0, jnp.zeros_like(halo), halo)
    pos = pos0 + lax.broadcasted_iota(jnp.int32, (tm, 1), 0)
    pooled = []
    for g, ksz in enumerate(POOL_SIZES):
        cs = slice(g * POOL_GROUP, (g + 1) * POOL_GROUP)
        ext = jnp.concatenate([halo[:, cs], u_ref[:, cs]], axis=0)
        win, shift = ext, 1
        while shift < ksz:
            win = win + pltpu.roll(win, shift, 0)
            shift *= 2
        u = ext[POOL_HALO:]
        cnt = jnp.minimum(pos + 1, ksz).astype(F32)
        dlt = win[POOL_HALO:] / cnt - u
        pooled.append(jnp.dot(dlt.astype(BF16), pm_ref[g], preferred_element_type=F32))
    pool = jnp.concatenate(pooled, axis=1) * ps_ref[...]
    b_ref[...] = (pool * zp_ref[...].astype(F32)).astype(BF16)


def _residual_epilogue(acc, cs, x_ref):
    return acc + x_ref[:, cs]


def _tail(os_, lses, rest, u, wba, wbp, pm, ps, batch, seq, *, gate_block0, za_block, zp_block):
    t, dm = u.shape[0], D_MODEL
    tm = PERM_TM
    tiles = seq // tm
    n_tiles = t // tm
    halo_per_tile = tm // POOL_HALO
    resident = functools.partial(pl.BlockSpec, pipeline_mode=pl.Buffered(1))
    cur = lambda i: jnp.minimum(i, n_tiles - 1)
    prv = lambda i: jnp.maximum(i - 1, 0)
    row = lambda blk: (lambda i: (cur(i), blk))
    row_prv = lambda blk: (lambda i: (prv(i), blk))
    gate_blocks = D_MODEL // ATTN_OUT
    unperms = [jnp.asarray(_deinterleave_matrix(d).T, BF16) for _, d in DILATED_GROUPS[1:]]
    o_in, l_in, o_specs, l_specs = [], [], [], []
    for g, (_, d) in enumerate(DILATED_GROUPS):
        for arr, width, ins, specs in ((os_, ATTN_OUT, o_in, o_specs), (lses, LANES, l_in, l_specs)):
            if d == 1:
                ins.append(arr)
                specs.append(pl.BlockSpec((None, tm, width), lambda i, g=g: (g, cur(i), 0)))
            else:
                ins.append(arr.reshape(arr.shape[0], batch, d, seq // d, width))
                specs.append(pl.BlockSpec((None, None, d, tm // d, width),
                                          lambda i, g=g: (g, cur(i) // tiles, 0, cur(i) % tiles, 0)))
    in_specs = (
        o_specs + l_specs
        + [resident((tm, tm), lambda i: (0, 0)), resident((tm, tm), lambda i: (0, 0)),
           pl.BlockSpec((tm, D_MODEL), row_prv(gate_block0 // gate_blocks)),
           pl.BlockSpec((tm, D_MODEL), row_prv(gate_block0 // gate_blocks + 1)),
           pl.BlockSpec((tm, ATTN_OUT), row(za_block)),
           pl.BlockSpec((tm, POOL_WIDTH), row(zp_block)),
           pl.BlockSpec((tm, POOL_WIDTH), row(0)),
           pl.BlockSpec((POOL_HALO, POOL_WIDTH),
                        lambda i: (jnp.maximum(cur(i) * halo_per_tile - 1, 0), 0)),
           resident(wba.shape, lambda i: (0, 0)),
           resident(wbp.shape, lambda i: (0, 0)),
           resident(pm.shape, lambda i: (0, 0, 0)),
           resident(ps.shape, lambda i: (0, 0))])
    return pl.pallas_call(
        functools.partial(_tail_kernel, tm=tm, seq=seq, n_tiles=n_tiles),
        grid=(n_tiles + 1,),
        in_specs=in_specs,
        out_specs=pl.BlockSpec((tm, dm), row_prv(0)),
        out_shape=jax.ShapeDtypeStruct((t, dm), BF16),
        scratch_shapes=[pltpu.VMEM((2, tm, ATTN_OUT), BF16), pltpu.VMEM((2, tm, POOL_WIDTH), BF16)],
        compiler_params=pltpu.CompilerParams(
            dimension_semantics=("arbitrary",),
            vmem_limit_bytes=VMEM_LIMIT_BYTES),
        name="merge_pool",
    )(*o_in, *l_in, *unperms, rest, rest, rest, rest, u, u, wba, wbp, pm, ps)


def kernel(x, norm_gain, w_in, b_gates, q_norm_gain, k_norm_gain, pool_maps, pool_scale,
           w_branch_attn, w_branch_pool, w_out):
    batch, seq, dm = x.shape
    t = batch * seq
    x2 = x.reshape(t, dm)
    hs = _rmsnorm(x2, norm_gain, batch, seq)

    lane_rep = lambda g: jnp.broadcast_to(g.astype(F32)[:, None], (HEAD_DIM, LANES))
    gains = jnp.stack([lane_rep(q_norm_gain * HEAD_DIM ** -0.5), lane_rep(k_norm_gain)])
    qkv_t = None
    for g, (_, d) in enumerate(DILATED_GROUPS):
        cos_t, sin_t = _rope_tables_t(seq, d)
        qkv_t = _project_qkv(hs[g], w_in, g, gains, cos_t, sin_t, seq, qkv_t)

    u = _project(hs[0], w_in, n_blocks=1, w_block_of=lambda j: j + 10,
                 epilogues=((0, _identity_epilogue),), out_dtype=F32, name="proj_u")

    n_gate_blocks = 2 * D_MODEL // PROJ_TN
    half_bias = (0.5 * b_gates.astype(F32)).reshape(1, 2 * D_MODEL)
    rest_w_block = lambda j: jnp.where(j < 4, j + 12, jnp.where(j == 4, 9, 11))
    rest, o_all, lse_all = _project_rest_with_attention(
        hs[0], w_in, half_bias, qkv_t, seq, n_gate_blocks=n_gate_blocks, w_block_of=rest_w_block)

    merged = _tail(o_all, lse_all, rest, u,
                   w_branch_attn.astype(BF16), w_branch_pool.astype(BF16), pool_maps.astype(BF16),
                   pool_scale.astype(F32).reshape(1, POOL_WIDTH), batch, seq,
                   gate_block0=0, za_block=4, zp_block=5)
    out = _project(merged, w_out, n_blocks=dm // PROJ_TN, w_block_of=lambda j: j,
                   epilogues=((0, _residual_epilogue),), out_dtype=F32, aux=(x2,),
                   aux_specs=(pl.BlockSpec((PROJ_TM, PROJ_TN), lambda j, i: (i, j)),),
                   name="proj_out")
    return out.reshape(batch, seq, dm)
```

```python
import functools

import numpy as np
import jax
import jax.numpy as jnp
from jax import lax
from jax.experimental import pallas as pl
from jax.experimental.pallas import tpu as pltpu

D_MODEL = 2048
HEAD_DIM = 128
HEADS_PER_GROUP = 8
DILATED_GROUPS = ((128, 1), (512, 4), (2048, 16))
N_ATTN_GROUPS = len(DILATED_GROUPS)
ATTN_OUT = HEADS_PER_GROUP * HEAD_DIM
BLK = 128
ROPE_THETA = 500000.0
ROT_DIM = HEAD_DIM // 4
ROT_HALF = ROT_DIM // 2
POOL_SIZES = (2, 4, 8, 16)
POOL_WIDTH = D_MODEL // 2
POOL_GROUP = POOL_WIDTH // len(POOL_SIZES)
NORM_EPS = 1e-6

LANES = 128
MXU_WIDTH = 256
VMEM_LIMIT_BYTES = 56 * 1024 * 1024

PROJ_TM = 1024
PROJ_TN = 1024
PERM_TM = 256
NORM_TM = 1024
ATTN_TB = 256
POOL_HALO = 16

F32 = jnp.float32
BF16 = jnp.bfloat16

_NT = (((1,), (1,)), ((), ()))
_TN = (((0,), (0,)), ((), ()))


def _deinterleave_matrix(d):
    n = PERM_TM // d
    i = np.arange(PERM_TM)
    p = np.zeros((PERM_TM, PERM_TM), np.float32)
    p[i, (i % n) * d + i // n] = 1.0
    return p


def _rmsnorm_kernel(x_ref, g_ref, p4_ref, p16_ref, h1_ref, h4_ref, h16_ref):
    for sb in range(x_ref.shape[0] // PERM_TM):
        rs = slice(sb * PERM_TM, (sb + 1) * PERM_TM)
        x = x_ref[rs, :]
        ms = jnp.mean(x * x, axis=-1, keepdims=True)
        h = (x * lax.rsqrt(ms + NORM_EPS) * g_ref[...]).astype(BF16)
        h1_ref[rs, :] = h
        for p_ref, out_ref in ((p4_ref, h4_ref), (p16_ref, h16_ref)):
            d = out_ref.shape[0]
            n = PERM_TM // d
            hp = jnp.dot(p_ref[...], h, preferred_element_type=F32).astype(BF16)
            for r in range(d):
                out_ref[r, sb * n:(sb + 1) * n, :] = hp[r * n:(r + 1) * n, :]


def _rmsnorm(x2, gain, batch, seq):
    t, dm = x2.shape
    tiles = seq // NORM_TM
    perms = [jnp.asarray(_deinterleave_matrix(d), BF16) for _, d in DILATED_GROUPS[1:]]
    out_shape = [jax.ShapeDtypeStruct((t, dm), BF16)]
    out_specs = [pl.BlockSpec((NORM_TM, dm), lambda i: (i, 0))]
    for _, d in DILATED_GROUPS[1:]:
        out_shape.append(jax.ShapeDtypeStruct((batch, d, seq // d, dm), BF16))
        out_specs.append(pl.BlockSpec((None, d, NORM_TM // d, dm),
                                      lambda i: (i // tiles, 0, i % tiles, 0)))
    const = lambda i: (0, 0)
    h1, h4, h16 = pl.pallas_call(
        _rmsnorm_kernel,
        grid=(t // NORM_TM,),
        in_specs=[pl.BlockSpec((NORM_TM, dm), lambda i: (i, 0)),
                  pl.BlockSpec((1, dm), const),
                  pl.BlockSpec((PERM_TM, PERM_TM), const),
                  pl.BlockSpec((PERM_TM, PERM_TM), const)],
        out_specs=out_specs,
        out_shape=out_shape,
        compiler_params=pltpu.CompilerParams(dimension_semantics=("arbitrary",),
                                             vmem_limit_bytes=VMEM_LIMIT_BYTES),
        name="rmsnorm",
    )(x2, gain.reshape(1, dm), *perms)
    return h1, h4.reshape(t, dm), h16.reshape(t, dm)


def _qkv_kernel(*refs, tm, tn):
    h_refs = refs[:N_ATTN_GROUPS]
    w_ref, gain_ref, cos_ref, sin_ref, out_ref, wt_ref = refs[N_ATTN_GROUPS:]
    group = pl.program_id(0) // 3
    part = pl.program_id(0) % 3

    @pl.when(pl.program_id(1) == 0)
    def _():
        for c in range(tn // MXU_WIDTH):
            cs = slice(c * MXU_WIDTH, (c + 1) * MXU_WIDTH)
            wt_ref[cs, :] = w_ref[:, cs].T.astype(BF16)

    for g, h_ref in enumerate(h_refs):
        body = functools.partial(_qkv_body, h_ref, gain_ref, cos_ref, sin_ref, out_ref, wt_ref, tm=tm, tn=tn)
        pl.when((group == g) & (part < 2))(functools.partial(body, normed=True))
        pl.when((group == g) & (part == 2))(functools.partial(body, normed=False))


def _qkv_body(h_ref, gain_ref, cos_ref, sin_ref, out_ref, wt_ref, *, tm, tn, normed):
    def chunk(c):
        ts = slice(c * ATTN_TB, (c + 1) * ATTN_TB)
        return lax.dot_general(wt_ref[...], h_ref[ts, :], _NT, preferred_element_type=F32)

    if not normed:
        for c in range(tm // ATTN_TB):
            out_ref[c] = chunk(c).astype(out_ref.dtype)
    else:
        for c in range(tm // ATTN_TB):
            acc = chunk(c)
            for hh in range(tn // HEAD_DIM):
                hs = slice(hh * HEAD_DIM, (hh + 1) * HEAD_DIM)
                for lt in range(ATTN_TB // LANES):
                    ls = slice(lt * LANES, (lt + 1) * LANES)
                    pos = slice(c * ATTN_TB + lt * LANES, c * ATTN_TB + (lt + 1) * LANES)
                    a = acc[hs, ls]
                    ss = jnp.sum(a * a, axis=0, keepdims=True)
                    r = lax.rsqrt(ss * (1.0 / HEAD_DIM) + NORM_EPS)
                    ag = a * gain_ref[...]
                    lo, hi = ag[0:ROT_HALF], ag[ROT_HALF:ROT_DIM]
                    cos, sin = cos_ref[:, pos], sin_ref[:, pos]
                    y = jnp.concatenate([lo * cos - hi * sin, hi * cos + lo * sin, ag[ROT_DIM:]], axis=0)
                    out_ref[c, hs, ls] = (y * r).astype(out_ref.dtype)


def _project_qkv(hs, w_in, gains, cos_t, sin_t, seq):
    t, dm = hs[0].shape
    tm, tn = PROJ_TM, PROJ_TN
    seq_tiles = seq // tm
    group = lambda gj: gj // 3
    part = lambda gj: gj % 3
    h_spec = lambda g: pl.BlockSpec((tm, dm), lambda gj, i: (jnp.where(group(gj) == g, i, 0), 0))
    table = pl.BlockSpec((None, ROT_HALF, tm), lambda gj, i: (group(gj), 0, i % seq_tiles))
    return pl.pallas_call(
        functools.partial(_qkv_kernel, tm=tm, tn=tn),
        grid=(3 * N_ATTN_GROUPS, t // tm),
        in_specs=[*(h_spec(g) for g in range(N_ATTN_GROUPS)),
                  pl.BlockSpec((dm, tn), lambda gj, i: (0, part(gj) * N_ATTN_GROUPS + group(gj))),
                  pl.BlockSpec((None, HEAD_DIM, LANES), lambda gj, i: (jnp.minimum(part(gj), 1), 0, 0)),
                  table, table],
        out_specs=pl.BlockSpec((None, None, tm // ATTN_TB, tn, ATTN_TB),
                               lambda gj, i: (group(gj), part(gj), i, 0, 0)),
        out_shape=jax.ShapeDtypeStruct((N_ATTN_GROUPS, 3, t // ATTN_TB, tn, ATTN_TB), BF16),
        scratch_shapes=[pltpu.VMEM((tn, dm), BF16)],
        compiler_params=pltpu.CompilerParams(
            dimension_semantics=("arbitrary", "arbitrary"),
            vmem_limit_bytes=VMEM_LIMIT_BYTES),
        name="proj_qkv",
    )(*hs, w_in, gains, cos_t, sin_t)


def _rope_tables_t(seq, d):
    inv_freq = ROPE_THETA ** (-jnp.arange(0, ROT_DIM, 2, dtype=F32) / ROT_DIM)
    pos = jnp.arange(seq, dtype=jnp.int32).reshape(seq // d, d).T.reshape(seq)
    ang = pos.astype(F32)[None, :] * inv_freq[:, None]
    return jnp.cos(ang), jnp.sin(ang)


def _identity_epilogue(acc, cs):
    del cs
    return acc


def _gate_epilogue(acc, cs, half_bias_ref):
    return 0.5 + 0.5 * jnp.tanh(0.5 * acc + half_bias_ref[:, cs])


def _silu_epilogue(acc, cs, half_bias_ref):
    del cs, half_bias_ref
    half = 0.5 * acc
    return half + half * jnp.tanh(half)


def _proj_kernel(*refs, epilogues, n_aux, tn):
    h_ref, w_ref = refs[0], refs[1]
    aux = refs[2:2 + n_aux]
    out_ref = refs[2 + n_aux]
    wbf_ref = refs[3 + n_aux]
    j = pl.program_id(0)

    @pl.when(pl.program_id(1) == 0)
    def _():
        wbf_ref[...] = w_ref[...].astype(BF16)

    def body(epilogue):
        for c in range(tn // MXU_WIDTH):
            cs = slice(c * MXU_WIDTH, (c + 1) * MXU_WIDTH)
            acc = jnp.dot(h_ref[...], wbf_ref[:, cs], preferred_element_type=F32)
            out_ref[:, cs] = epilogue(acc, cs, *aux).astype(out_ref.dtype)

    if len(epilogues) == 1:
        body(epilogues[0][1])
    else:
        bounds = [first for first, _ in epilogues[1:]] + [None]
        for (first, fn), last in zip(epilogues, bounds):
            cond = (j >= first) if last is None else ((j >= first) & (j < last))
            pl.when(cond)(functools.partial(body, fn))


def _project(h, w_in, *, n_blocks, w_block_of, epilogues, out_dtype, aux=(), aux_specs=(), name):
    t, d = h.shape
    tm, tn = PROJ_TM, PROJ_TN
    kern = functools.partial(_proj_kernel, epilogues=epilogues, n_aux=len(aux), tn=tn)
    return pl.pallas_call(
        kern,
        grid=(n_blocks, t // tm),
        in_specs=[pl.BlockSpec((tm, d), lambda j, i: (i, 0)),
                  pl.BlockSpec((d, tn), lambda j, i: (0, w_block_of(j))),
                  *aux_specs],
        out_specs=pl.BlockSpec((tm, tn), lambda j, i: (i, j)),
        out_shape=jax.ShapeDtypeStruct((t, n_blocks * tn), out_dtype),
        scratch_shapes=[pltpu.VMEM((d, tn), BF16)],
        compiler_params=pltpu.CompilerParams(
            dimension_semantics=("arbitrary", "arbitrary"),
            vmem_limit_bytes=VMEM_LIMIT_BYTES),
        name=name,
    )(h, w_in, *aux)


def _band_bias():
    i = np.arange(BLK)[:, None]
    j = np.arange(2 * BLK)[None, :]
    band = (j >= i) & (j <= i + BLK)
    first = band & (j >= BLK)
    return np.where(np.stack([band, first]), 0.0, -np.inf).astype(np.float32)


class _AttentionBlocks:
    def __init__(self, bias_ref, q_ref, k_ref, v_ref, o_ref, lse_ref, kprev_ref, vprev_ref, firsts):
        self.bias_ref, self.q_ref, self.k_ref, self.v_ref = bias_ref, q_ref, k_ref, v_ref
        self.o_ref, self.lse_ref, self.kprev_ref, self.vprev_ref = o_ref, lse_ref, kprev_ref, vprev_ref
        self.firsts = firsts
        self.per = ATTN_TB // BLK
        self.scores = {}

    def _block(self, ref, n, cs):
        lo = (n % self.per) * BLK
        return ref[n // self.per, cs, lo:lo + BLK]

    def _with_previous(self, ref, prev_ref, n, cs):
        prev = prev_ref[cs, :] if n == 0 else self._block(ref, n - 1, cs)
        return jnp.concatenate([prev, self._block(ref, n, cs)], axis=1)

    def score_phase(self, n):
        first = self.firsts[n]
        bias = self.bias_ref[int(first)] if isinstance(first, bool) else self.bias_ref[jnp.where(first, 1, 0)]
        for h in range(HEADS_PER_GROUP):
            cs = slice(h * HEAD_DIM, (h + 1) * HEAD_DIM)
            kk = self._with_previous(self.k_ref, self.kprev_ref, n, cs)
            s = lax.dot_general(self._block(self.q_ref, n, cs), kk, _TN, preferred_element_type=F32)
            self.scores[n, h] = s + bias

    def value_phase(self, n):
        ts = slice(n * BLK, (n + 1) * BLK)
        lane = lax.broadcasted_iota(jnp.int32, (BLK, LANES), 1)
        m_tile = jnp.zeros((BLK, LANES), F32)
        l_tile = jnp.ones((BLK, LANES), F32)
        for h in range(HEADS_PER_GROUP):
            cs = slice(h * HEAD_DIM, (h + 1) * HEAD_DIM)
            s = self.scores.pop((n, h))
            m = jnp.max(s, axis=-1, keepdims=True)
            p = jnp.exp(s - m)
            l = jnp.sum(p, axis=-1, keepdims=True)
            vv = self._with_previous(self.v_ref, self.vprev_ref, n, cs)
            o = lax.dot_general(p.astype(BF16), vv, _NT, preferred_element_type=F32)
            self.o_ref[ts, cs] = (o / l).astype(self.o_ref.dtype)
            m_tile = jnp.where(lane == h, m, m_tile)
            l_tile = jnp.where(lane == h, l, l_tile)
        self.lse_ref[ts, :] = m_tile + jnp.log(l_tile)

    def carry(self, n_blocks):
        self.kprev_ref[...] = self._block(self.k_ref, n_blocks - 1, slice(None))
        self.vprev_ref[...] = self._block(self.v_ref, n_blocks - 1, slice(None))


def _rest_attn_kernel(h_ref, w_ref, half_bias_ref, bias_ref, q_ref, k_ref, v_ref,
                      out_ref, o_ref, lse_ref, wbf_ref, kprev_ref, vprev_ref,
                      *, tn, n_gate_blocks, steps_per_group, tiles_per_sub):
    j, i = pl.program_id(0), pl.program_id(1)
    step = j * pl.num_programs(1) + i

    @pl.when(i == 0)
    def _():
        wbf_ref[...] = w_ref[...].astype(BF16)

    @pl.when(step == 0)
    def _():
        kprev_ref[...] = jnp.zeros_like(kprev_ref)
        vprev_ref[...] = jnp.zeros_like(vprev_ref)

    n_blocks = tn // MXU_WIDTH
    per = ATTN_TB // BLK
    tiles_per_step = n_blocks // per
    group = step // steps_per_group
    tps = functools.reduce(lambda acc, g: jnp.where(group == g, tiles_per_sub[g], acc),
                           range(len(tiles_per_sub)), tiles_per_sub[0])
    tile0 = (step % steps_per_group) * tiles_per_step
    firsts = [((tile0 + n // per) & (tps - 1)) == 0 if n % per == 0 else False for n in range(n_blocks)]

    def body(epilogue):
        attn = _AttentionBlocks(bias_ref, q_ref, k_ref, v_ref, o_ref, lse_ref, kprev_ref, vprev_ref, firsts)
        for c in range(n_blocks):
            cs = slice(c * MXU_WIDTH, (c + 1) * MXU_WIDTH)
            attn.score_phase(c)
            acc = jnp.dot(h_ref[...], wbf_ref[:, cs], preferred_element_type=F32)
            attn.value_phase(c)
            out_ref[:, cs] = epilogue(acc, cs, half_bias_ref).astype(out_ref.dtype)
        attn.carry(n_blocks)

    pl.when(j < n_gate_blocks)(functools.partial(body, _gate_epilogue))
    pl.when(j >= n_gate_blocks)(functools.partial(body, _silu_epilogue))


def _project_rest_with_attention(h, w_in, half_bias, qkv_t, seq, *, n_gate_blocks, w_block_of):
    t, dm = h.shape
    tm, tn = PROJ_TM, PROJ_TN
    n_blocks = n_gate_blocks + 2
    rows = t // tm
    n_groups, _, n_tiles = qkv_t.shape[:3]
    blocks_per_step = tn // MXU_WIDTH
    tiles_per_step = blocks_per_step * BLK // ATTN_TB
    steps_per_group = n_tiles // tiles_per_step
    assert n_groups * steps_per_group == n_blocks * rows, "attention tiles must fill the projection steps"
    tiles_per_sub = tuple(seq // d // ATTN_TB for _, d in DILATED_GROUPS)
    step = lambda j, i: j * rows + i
    tile = lambda part: pl.BlockSpec(
        (None, None, tiles_per_step, ATTN_OUT, ATTN_TB),
        lambda j, i: (step(j, i) // steps_per_group, part, step(j, i) % steps_per_group, 0, 0))
    tokens = tiles_per_step * ATTN_TB
    per_group = lambda width: pl.BlockSpec(
        (None, tokens, width), lambda j, i: (step(j, i) // steps_per_group, step(j, i) % steps_per_group, 0))
    return pl.pallas_call(
        functools.partial(_rest_attn_kernel, tn=tn, n_gate_blocks=n_gate_blocks,
                          steps_per_group=steps_per_group, tiles_per_sub=tiles_per_sub),
        grid=(n_blocks, rows),
        in_specs=[pl.BlockSpec((tm, dm), lambda j, i: (i, 0)),
                  pl.BlockSpec((dm, tn), lambda j, i: (0, w_block_of(j))),
                  pl.BlockSpec((1, tn), lambda j, i: (0, jnp.minimum(j, n_gate_blocks - 1))),
                  pl.BlockSpec((2, BLK, 2 * BLK), lambda j, i: (0, 0, 0), pipeline_mode=pl.Buffered(1)),
                  tile(0), tile(1), tile(2)],
        out_specs=[pl.BlockSpec((tm, tn), lambda j, i: (i, j)),
                   per_group(ATTN_OUT), per_group(LANES)],
        out_shape=[jax.ShapeDtypeStruct((t, n_blocks * tn), BF16),
                   jax.ShapeDtypeStruct((n_groups, t, ATTN_OUT), BF16),
                   jax.ShapeDtypeStruct((n_groups, t, LANES), F32)],
        scratch_shapes=[pltpu.VMEM((dm, tn), BF16),
                        pltpu.VMEM((ATTN_OUT, BLK), BF16), pltpu.VMEM((ATTN_OUT, BLK), BF16)],
        compiler_params=pltpu.CompilerParams(
            dimension_semantics=("arbitrary", "arbitrary"),
            vmem_limit_bytes=VMEM_LIMIT_BYTES),
        name="proj_rest_attention",
    )(h, w_in, half_bias, jnp.asarray(_band_bias()), qkv_t, qkv_t, qkv_t)


def _rows(ref):
    return jnp.concatenate([ref[r] for r in range(ref.shape[0])], axis=0)


def _unpermute_f32(q, x):
    hi = x.astype(BF16).astype(F32)
    r1 = x - hi
    mid = r1.astype(BF16).astype(F32)
    lo = r1 - mid
    packed = hi + pltpu.roll(mid, HEADS_PER_GROUP, 1) + pltpu.roll(lo, 2 * HEADS_PER_GROUP, 1)
    y = jnp.dot(q, packed.astype(BF16), preferred_element_type=F32)
    return (y + pltpu.roll(y, LANES - HEADS_PER_GROUP, 1)) + pltpu.roll(y, LANES - 2 * HEADS_PER_GROUP, 1)


def _tail_kernel(o0_ref, o1_ref, o2_ref, l0_ref, l1_ref, l2_ref, q4_ref, q16_ref, ga_ref, gp_ref,
                 za_ref, zp_ref, u_ref, uh_ref, wba_ref, wbp_ref, pm_ref, ps_ref,
                 out_ref, a_ref, b_ref, *, tm, seq, n_tiles):
    step = pl.program_id(0)

    @pl.when(step == 0)
    def _():
        a_ref[...] = jnp.zeros_like(a_ref)
        b_ref[...] = jnp.zeros_like(b_ref)

    body = functools.partial(
        _tail_step, o0_ref, o1_ref, o2_ref, l0_ref, l1_ref, l2_ref, q4_ref, q16_ref, ga_ref, gp_ref,
        za_ref, zp_ref, u_ref, uh_ref, wba_ref, wbp_ref, pm_ref, ps_ref, out_ref,
        pos0=(jnp.minimum(step, n_tiles - 1) * tm) % seq, tm=tm)
    for parity in range(2):
        pl.when(step % 2 == parity)(
            functools.partial(body, a_ref.at[1 - parity], b_ref.at[1 - parity],
                              a_ref.at[parity], b_ref.at[parity]))


def _tail_step(o0_ref, o1_ref, o2_ref, l0_ref, l1_ref, l2_ref, q4_ref, q16_ref, ga_ref, gp_ref,
               za_ref, zp_ref, u_ref, uh_ref, wba_ref, wbp_ref, pm_ref, ps_ref, out_ref,
               a_in_ref, b_in_ref, a_ref, b_ref, *, pos0, tm):
    q4, q16 = q4_ref[...], q16_ref[...]
    o0 = o0_ref[...]
    o1 = jnp.dot(q4, _rows(o1_ref), preferred_element_type=F32)
    o2 = jnp.dot(q16, _rows(o2_ref), preferred_element_type=F32)
    l0 = l0_ref[...]
    l1 = _unpermute_f32(q4, _rows(l1_ref))
    l2 = _unpermute_f32(q16, _rows(l2_ref))

    y_attn = jnp.dot(a_in_ref[...], wba_ref[...], preferred_element_type=F32)
    y_pool = jnp.dot(b_in_ref[...], wbp_ref[...], preferred_element_type=F32)
    merged = ga_ref[...].astype(F32) * y_attn + gp_ref[...].astype(F32) * y_pool
    out_ref[...] = merged.astype(out_ref.dtype)

    mx = jnp.maximum(jnp.maximum(l0, l1), l2)
    e0, e1, e2 = jnp.exp(l0 - mx), jnp.exp(l1 - mx), jnp.exp(l2 - mx)
    inv = 1.0 / (e0 + e1 + e2)
    w0, w1, w2 = e0 * inv, e1 * inv, e2 * inv
    for h in range(HEADS_PER_GROUP):
        cs = slice(h * HEAD_DIM, (h + 1) * HEAD_DIM)
        a = (w0[:, h:h + 1] * o0[:, cs].astype(F32)
             + w1[:, h:h + 1] * o1[:, cs]
             + w2[:, h:h + 1] * o2[:, cs])
        a_ref[:, cs] = (a * za_ref[:, cs].astype(F32)).astype(BF16)

    halo = uh_ref[...]
    halo = jnp.where(pos0 == 0, jnp.zeros_like(halo), halo)
    pos = pos0 + lax.broadcasted_iota(jnp.int32, (tm, 1), 0)
    pooled = []
    for g, ksz in enumerate(POOL_SIZES):
        cs = slice(g * POOL_GROUP, (g + 1) * POOL_GROUP)
        ext = jnp.concatenate([halo[:, cs], u_ref[:, cs]], axis=0)
        win, shift = ext, 1
        while shift < ksz:
            win = win + pltpu.roll(win, shift, 0)
            shift *= 2
        u = ext[POOL_HALO:]
        cnt = jnp.minimum(pos + 1, ksz).astype(F32)
        dlt = win[POOL_HALO:] / cnt - u
        pooled.append(jnp.dot(dlt.astype(BF16), pm_ref[g], preferred_element_type=F32))
    pool = jnp.concatenate(pooled, axis=1) * ps_ref[...]
    b_ref[...] = (pool * zp_ref[...].astype(F32)).astype(BF16)


def _residual_epilogue(acc, cs, x_ref):
    return acc + x_ref[:, cs]


def _tail(os_, lses, rest, u, wba, wbp, pm, ps, batch, seq, *, gate_block0, za_block, zp_block):
    t, dm = u.shape[0], D_MODEL
    tm = PERM_TM
    tiles = seq // tm
    n_tiles = t // tm
    halo_per_tile = tm // POOL_HALO
    resident = functools.partial(pl.BlockSpec, pipeline_mode=pl.Buffered(1))
    cur = lambda i: jnp.minimum(i, n_tiles - 1)
    prv = lambda i: jnp.maximum(i - 1, 0)
    row = lambda blk: (lambda i: (cur(i), blk))
    row_prv = lambda blk: (lambda i: (prv(i), blk))
    gate_blocks = D_MODEL // ATTN_OUT
    unperms = [jnp.asarray(_deinterleave_matrix(d).T, BF16) for _, d in DILATED_GROUPS[1:]]
    o_in, l_in, o_specs, l_specs = [], [], [], []
    for g, (_, d) in enumerate(DILATED_GROUPS):
        for arr, width, ins, specs in ((os_, ATTN_OUT, o_in, o_specs), (lses, LANES, l_in, l_specs)):
            if d == 1:
                ins.append(arr)
                specs.append(pl.BlockSpec((None, tm, width), lambda i, g=g: (g, cur(i), 0)))
            else:
                ins.append(arr.reshape(arr.shape[0], batch, d, seq // d, width))
                specs.append(pl.BlockSpec((None, None, d, tm // d, width),
                                          lambda i, g=g: (g, cur(i) // tiles, 0, cur(i) % tiles, 0)))
    in_specs = (
        o_specs + l_specs
        + [resident((tm, tm), lambda i: (0, 0)), resident((tm, tm), lambda i: (0, 0)),
           pl.BlockSpec((tm, D_MODEL), row_prv(gate_block0 // gate_blocks)),
           pl.BlockSpec((tm, D_MODEL), row_prv(gate_block0 // gate_blocks + 1)),
           pl.BlockSpec((tm, ATTN_OUT), row(za_block)),
           pl.BlockSpec((tm, POOL_WIDTH), row(zp_block)),
           pl.BlockSpec((tm, POOL_WIDTH), row(0)),
           pl.BlockSpec((POOL_HALO, POOL_WIDTH),
                        lambda i: (jnp.maximum(cur(i) * halo_per_tile - 1, 0), 0)),
           resident(wba.shape, lambda i: (0, 0)),
           resident(wbp.shape, lambda i: (0, 0)),
           resident(pm.shape, lambda i: (0, 0, 0)),
           resident(ps.shape, lambda i: (0, 0))])
    return pl.pallas_call(
        functools.partial(_tail_kernel, tm=tm, seq=seq, n_tiles=n_tiles),
        grid=(n_tiles + 1,),
        in_specs=in_specs,
        out_specs=pl.BlockSpec((tm, dm), row_prv(0)),
        out_shape=jax.ShapeDtypeStruct((t, dm), BF16),
        scratch_shapes=[pltpu.VMEM((2, tm, ATTN_OUT), BF16), pltpu.VMEM((2, tm, POOL_WIDTH), BF16)],
        compiler_params=pltpu.CompilerParams(
            dimension_semantics=("arbitrary",),
            vmem_limit_bytes=VMEM_LIMIT_BYTES),
        name="merge_pool",
    )(*o_in, *l_in, *unperms, rest, rest, rest, rest, u, u, wba, wbp, pm, ps)


def kernel(x, norm_gain, w_in, b_gates, q_norm_gain, k_norm_gain, pool_maps, pool_scale,
           w_branch_attn, w_branch_pool, w_out):
    batch, seq, dm = x.shape
    t = batch * seq
    x2 = x.reshape(t, dm)
    hs = _rmsnorm(x2, norm_gain, batch, seq)

    lane_rep = lambda g: jnp.broadcast_to(g.astype(F32)[:, None], (HEAD_DIM, LANES))
    gains = jnp.stack([lane_rep(q_norm_gain * HEAD_DIM ** -0.5), lane_rep(k_norm_gain)])
    tables = [_rope_tables_t(seq, d) for _, d in DILATED_GROUPS]
    qkv_t = _project_qkv(hs, w_in, gains, jnp.stack([c for c, _ in tables]),
                         jnp.stack([s for _, s in tables]), seq)

    u = _project(hs[0], w_in, n_blocks=1, w_block_of=lambda j: j + 10,
                 epilogues=((0, _identity_epilogue),), out_dtype=F32, name="proj_u")

    n_gate_blocks = 2 * D_MODEL // PROJ_TN
    half_bias = (0.5 * b_gates.astype(F32)).reshape(1, 2 * D_MODEL)
    rest_w_block = lambda j: jnp.where(j < 4, j + 12, jnp.where(j == 4, 9, 11))
    rest, o_all, lse_all = _project_rest_with_attention(
        hs[0], w_in, half_bias, qkv_t, seq, n_gate_blocks=n_gate_blocks, w_block_of=rest_w_block)

    merged = _tail(o_all, lse_all, rest, u,
                   w_branch_attn.astype(BF16), w_branch_pool.astype(BF16), pool_maps.astype(BF16),
                   pool_scale.astype(F32).reshape(1, POOL_WIDTH), batch, seq,
                   gate_block0=0, za_block=4, zp_block=5)
    out = _project(merged, w_out, n_blocks=dm // PROJ_TN, w_block_of=lambda j: j,
                   epilogues=((0, _residual_epilogue),), out_dtype=F32, aux=(x2,),
                   aux_specs=(pl.BlockSpec((PROJ_TM, PROJ_TN), lambda j, i: (i, j)),),
                   name="proj_out")
    return out.reshape(batch, seq, dm)
```

```python
import functools

import numpy as np
import jax
import jax.numpy as jnp
from jax import lax
from jax.experimental import pallas as pl
from jax.experimental.pallas import tpu as pltpu

D_MODEL = 2048
HEAD_DIM = 128
HEADS_PER_GROUP = 8
DILATED_GROUPS = ((128, 1), (512, 4), (2048, 16))
N_ATTN_GROUPS = len(DILATED_GROUPS)
ATTN_OUT = HEADS_PER_GROUP * HEAD_DIM
BLK = 128
ROPE_THETA = 500000.0
ROT_DIM = HEAD_DIM // 4
ROT_HALF = ROT_DIM // 2
POOL_SIZES = (2, 4, 8, 16)
POOL_WIDTH = D_MODEL // 2
POOL_GROUP = POOL_WIDTH // len(POOL_SIZES)
NORM_EPS = 1e-6

LANES = 128
MXU_WIDTH = 256
VMEM_LIMIT_BYTES = 56 * 1024 * 1024

PROJ_TM = 1024
PROJ_TN = 1024
PERM_TM = 256
NORM_TM = 512
ATTN_TB = 256
POOL_HALO = 16

F32 = jnp.float32
BF16 = jnp.bfloat16

_NT = (((1,), (1,)), ((), ()))
_TN = (((0,), (0,)), ((), ()))


def _deinterleave_matrix(d):
    n = PERM_TM // d
    i = np.arange(PERM_TM)
    p = np.zeros((PERM_TM, PERM_TM), np.float32)
    p[i, (i % n) * d + i // n] = 1.0
    return p


def _norm_u_kernel(x_ref, g_ref, p4_ref, p16_ref, w_ref, h1_ref, h4_ref, h16_ref, u_ref,
                   hbuf_ref, wbf_ref):
    step = pl.program_id(0)

    @pl.when(step == 0)
    def _():
        wbf_ref[...] = w_ref[...].astype(BF16)
        hbuf_ref[...] = jnp.zeros_like(hbuf_ref)

    def body(h_in_ref, h_out_ref):
        for c in range(u_ref.shape[1] // MXU_WIDTH):
            cs = slice(c * MXU_WIDTH, (c + 1) * MXU_WIDTH)
            u_ref[:, cs] = jnp.dot(h_in_ref[...], wbf_ref[:, cs], preferred_element_type=F32)
        for sb in range(x_ref.shape[0] // PERM_TM):
            rs = slice(sb * PERM_TM, (sb + 1) * PERM_TM)
            x = x_ref[rs, :]
            ms = jnp.mean(x * x, axis=-1, keepdims=True)
            h = (x * lax.rsqrt(ms + NORM_EPS) * g_ref[...]).astype(BF16)
            h1_ref[rs, :] = h
            h_out_ref[rs, :] = h
            for p_ref, out_ref in ((p4_ref, h4_ref), (p16_ref, h16_ref)):
                d = out_ref.shape[0]
                n = PERM_TM // d
                hp = jnp.dot(p_ref[...], h, preferred_element_type=F32).astype(BF16)
                for r in range(d):
                    out_ref[r, sb * n:(sb + 1) * n, :] = hp[r * n:(r + 1) * n, :]

    for parity in range(2):
        pl.when(step % 2 == parity)(
            functools.partial(body, hbuf_ref.at[1 - parity], hbuf_ref.at[parity]))


def _rmsnorm_and_project_u(x2, gain, w_in, batch, seq, *, w_block):
    t, dm = x2.shape
    tm, tn = NORM_TM, PROJ_TN
    tiles = seq // tm
    n_tiles = t // tm
    perms = [jnp.asarray(_deinterleave_matrix(d), BF16) for _, d in DILATED_GROUPS[1:]]
    cur = lambda i: jnp.minimum(i, n_tiles - 1)
    prv = lambda i: jnp.maximum(i - 1, 0)
    out_shape = [jax.ShapeDtypeStruct((t, dm), BF16)]
    out_specs = [pl.BlockSpec((tm, dm), lambda i: (cur(i), 0))]
    for _, d in DILATED_GROUPS[1:]:
        out_shape.append(jax.ShapeDtypeStruct((batch, d, seq // d, dm), BF16))
        out_specs.append(pl.BlockSpec((None, d, tm // d, dm),
                                      lambda i: (cur(i) // tiles, 0, cur(i) % tiles, 0)))
    out_shape.append(jax.ShapeDtypeStruct((t, tn), F32))
    out_specs.append(pl.BlockSpec((tm, tn), lambda i: (prv(i), 0)))
    const = lambda i: (0, 0)
    resident = functools.partial(pl.BlockSpec, pipeline_mode=pl.Buffered(1))
    h1, h4, h16, u = pl.pallas_call(
        _norm_u_kernel,
        grid=(n_tiles + 1,),
        in_specs=[pl.BlockSpec((tm, dm), lambda i: (cur(i), 0)),
                  pl.BlockSpec((1, dm), const),
                  pl.BlockSpec((PERM_TM, PERM_TM), const),
                  pl.BlockSpec((PERM_TM, PERM_TM), const),
                  resident((dm, tn), lambda i: (0, w_block))],
        out_specs=out_specs,
        out_shape=out_shape,
        scratch_shapes=[pltpu.VMEM((2, tm, dm), BF16), pltpu.VMEM((dm, tn), BF16)],
        compiler_params=pltpu.CompilerParams(dimension_semantics=("arbitrary",),
                                             vmem_limit_bytes=VMEM_LIMIT_BYTES),
        name="rmsnorm_proj_u",
    )(x2, gain.reshape(1, dm), *perms, w_in)
    return (h1, h4.reshape(t, dm), h16.reshape(t, dm)), u


def _qkv_kernel(*refs, tm, tn):
    h_refs = refs[:N_ATTN_GROUPS]
    w_ref, gain_ref, cos_ref, sin_ref, out_ref, wt_ref = refs[N_ATTN_GROUPS:]
    group = pl.program_id(0) // 3
    part = pl.program_id(0) % 3

    @pl.when(pl.program_id(1) == 0)
    def _():
        for c in range(tn // MXU_WIDTH):
            cs = slice(c * MXU_WIDTH, (c + 1) * MXU_WIDTH)
            wt_ref[cs, :] = w_ref[:, cs].T.astype(BF16)

    for g, h_ref in enumerate(h_refs):
        body = functools.partial(_qkv_body, h_ref, gain_ref, cos_ref, sin_ref, out_ref, wt_ref, tm=tm, tn=tn)
        pl.when((group == g) & (part < 2))(functools.partial(body, normed=True))
        pl.when((group == g) & (part == 2))(functools.partial(body, normed=False))


def _qkv_body(h_ref, gain_ref, cos_ref, sin_ref, out_ref, wt_ref, *, tm, tn, normed):
    def chunk(c):
        ts = slice(c * ATTN_TB, (c + 1) * ATTN_TB)
        return lax.dot_general(wt_ref[...], h_ref[ts, :], _NT, preferred_element_type=F32)

    if not normed:
        for c in range(tm // ATTN_TB):
            out_ref[c] = chunk(c).astype(out_ref.dtype)
    else:
        for c in range(tm // ATTN_TB):
            acc = chunk(c)
            for hh in range(tn // HEAD_DIM):
                hs = slice(hh * HEAD_DIM, (hh + 1) * HEAD_DIM)
                for lt in range(ATTN_TB // LANES):
                    ls = slice(lt * LANES, (lt + 1) * LANES)
                    pos = slice(c * ATTN_TB + lt * LANES, c * ATTN_TB + (lt + 1) * LANES)
                    a = acc[hs, ls]
                    ss = jnp.sum(a * a, axis=0, keepdims=True)
                    r = lax.rsqrt(ss * (1.0 / HEAD_DIM) + NORM_EPS)
                    ag = a * gain_ref[...]
                    lo, hi = ag[0:ROT_HALF], ag[ROT_HALF:ROT_DIM]
                    cos, sin = cos_ref[:, pos], sin_ref[:, pos]
                    y = jnp.concatenate([lo * cos - hi * sin, hi * cos + lo * sin, ag[ROT_DIM:]], axis=0)
                    out_ref[c, hs, ls] = (y * r).astype(out_ref.dtype)


def _project_qkv(hs, w_in, gains, cos_t, sin_t, seq):
    t, dm = hs[0].shape
    tm, tn = PROJ_TM, PROJ_TN
    seq_tiles = seq // tm
    group = lambda gj: gj // 3
    part = lambda gj: gj % 3
    h_spec = lambda g: pl.BlockSpec((tm, dm), lambda gj, i: (jnp.where(group(gj) == g, i, 0), 0))
    table = pl.BlockSpec((None, ROT_HALF, tm), lambda gj, i: (group(gj), 0, i % seq_tiles))
    return pl.pallas_call(
        functools.partial(_qkv_kernel, tm=tm, tn=tn),
        grid=(3 * N_ATTN_GROUPS, t // tm),
        in_specs=[*(h_spec(g) for g in range(N_ATTN_GROUPS)),
                  pl.BlockSpec((dm, tn), lambda gj, i: (0, part(gj) * N_ATTN_GROUPS + group(gj))),
                  pl.BlockSpec((None, HEAD_DIM, LANES), lambda gj, i: (jnp.minimum(part(gj), 1), 0, 0)),
                  table, table],
        out_specs=pl.BlockSpec((None, None, tm // ATTN_TB, tn, ATTN_TB),
                               lambda gj, i: (group(gj), part(gj), i, 0, 0)),
        out_shape=jax.ShapeDtypeStruct((N_ATTN_GROUPS, 3, t // ATTN_TB, tn, ATTN_TB), BF16),
        scratch_shapes=[pltpu.VMEM((tn, dm), BF16)],
        compiler_params=pltpu.CompilerParams(
            dimension_semantics=("arbitrary", "arbitrary"),
            vmem_limit_bytes=VMEM_LIMIT_BYTES),
        name="proj_qkv",
    )(*hs, w_in, gains, cos_t, sin_t)


def _rope_tables_t(seq, d):
    inv_freq = ROPE_THETA ** (-jnp.arange(0, ROT_DIM, 2, dtype=F32) / ROT_DIM)
    pos = jnp.arange(seq, dtype=jnp.int32).reshape(seq // d, d).T.reshape(seq)
    ang = pos.astype(F32)[None, :] * inv_freq[:, None]
    return jnp.cos(ang), jnp.sin(ang)


def _identity_epilogue(acc, cs):
    del cs
    return acc


def _gate_epilogue(acc, cs, half_bias_ref):
    return 0.5 + 0.5 * jnp.tanh(0.5 * acc + half_bias_ref[:, cs])


def _silu_epilogue(acc, cs, half_bias_ref):
    del cs, half_bias_ref
    half = 0.5 * acc
    return half + half * jnp.tanh(half)


def _proj_kernel(*refs, epilogues, n_aux, tn):
    h_ref, w_ref = refs[0], refs[1]
    aux = refs[2:2 + n_aux]
    out_ref = refs[2 + n_aux]
    wbf_ref = refs[3 + n_aux]
    j = pl.program_id(0)

    @pl.when(pl.program_id(1) == 0)
    def _():
        wbf_ref[...] = w_ref[...].astype(BF16)

    def body(epilogue):
        for c in range(tn // MXU_WIDTH):
            cs = slice(c * MXU_WIDTH, (c + 1) * MXU_WIDTH)
            acc = jnp.dot(h_ref[...], wbf_ref[:, cs], preferred_element_type=F32)
            out_ref[:, cs] = epilogue(acc, cs, *aux).astype(out_ref.dtype)

    if len(epilogues) == 1:
        body(epilogues[0][1])
    else:
        bounds = [first for first, _ in epilogues[1:]] + [None]
        for (first, fn), last in zip(epilogues, bounds):
            cond = (j >= first) if last is None else ((j >= first) & (j < last))
            pl.when(cond)(functools.partial(body, fn))


def _project(h, w_in, *, n_blocks, w_block_of, epilogues, out_dtype, aux=(), aux_specs=(), name):
    t, d = h.shape
    tm, tn = PROJ_TM, PROJ_TN
    kern = functools.partial(_proj_kernel, epilogues=epilogues, n_aux=len(aux), tn=tn)
    return pl.pallas_call(
        kern,
        grid=(n_blocks, t // tm),
        in_specs=[pl.BlockSpec((tm, d), lambda j, i: (i, 0)),
                  pl.BlockSpec((d, tn), lambda j, i: (0, w_block_of(j))),
                  *aux_specs],
        out_specs=pl.BlockSpec((tm, tn), lambda j, i: (i, j)),
        out_shape=jax.ShapeDtypeStruct((t, n_blocks * tn), out_dtype),
        scratch_shapes=[pltpu.VMEM((d, tn), BF16)],
        compiler_params=pltpu.CompilerParams(
            dimension_semantics=("arbitrary", "arbitrary"),
            vmem_limit_bytes=VMEM_LIMIT_BYTES),
        name=name,
    )(h, w_in, *aux)


def _band_bias():
    i = np.arange(BLK)[:, None]
    j = np.arange(2 * BLK)[None, :]
    band = (j >= i) & (j <= i + BLK)
    first = band & (j >= BLK)
    return np.where(np.stack([band, first]), 0.0, -np.inf).astype(np.float32)


class _AttentionBlocks:
    def __init__(self, bias_ref, q_ref, k_ref, v_ref, o_ref, lse_ref, kprev_ref, vprev_ref, firsts):
        self.bias_ref, self.q_ref, self.k_ref, self.v_ref = bias_ref, q_ref, k_ref, v_ref
        self.o_ref, self.lse_ref, self.kprev_ref, self.vprev_ref = o_ref, lse_ref, kprev_ref, vprev_ref
        self.firsts = firsts
        self.per = ATTN_TB // BLK
        self.scores = {}

    def _block(self, ref, n, cs):
        lo = (n % self.per) * BLK
        return ref[n // self.per, cs, lo:lo + BLK]

    def _with_previous(self, ref, prev_ref, n, cs):
        prev = prev_ref[cs, :] if n == 0 else self._block(ref, n - 1, cs)
        return jnp.concatenate([prev, self._block(ref, n, cs)], axis=1)

    def score_phase(self, n):
        first = self.firsts[n]
        bias = self.bias_ref[int(first)] if isinstance(first, bool) else self.bias_ref[jnp.where(first, 1, 0)]
        for h in range(HEADS_PER_GROUP):
            cs = slice(h * HEAD_DIM, (h + 1) * HEAD_DIM)
            kk = self._with_previous(self.k_ref, self.kprev_ref, n, cs)
            s = lax.dot_general(self._block(self.q_ref, n, cs), kk, _TN, preferred_element_type=F32)
            self.scores[n, h] = s + bias

    def value_phase(self, n):
        ts = slice(n * BLK, (n + 1) * BLK)
        lane = lax.broadcasted_iota(jnp.int32, (BLK, LANES), 1)
        m_tile = jnp.zeros((BLK, LANES), F32)
        l_tile = jnp.ones((BLK, LANES), F32)
        for h in range(HEADS_PER_GROUP):
            cs = slice(h * HEAD_DIM, (h + 1) * HEAD_DIM)
            s = self.scores.pop((n, h))
            m = jnp.max(s, axis=-1, keepdims=True)
            p = jnp.exp(s - m)
            l = jnp.sum(p, axis=-1, keepdims=True)
            vv = self._with_previous(self.v_ref, self.vprev_ref, n, cs)
            o = lax.dot_general(p.astype(BF16), vv, _NT, preferred_element_type=F32)
            self.o_ref[ts, cs] = (o / l).astype(self.o_ref.dtype)
            m_tile = jnp.where(lane == h, m, m_tile)
            l_tile = jnp.where(lane == h, l, l_tile)
        self.lse_ref[ts, :] = m_tile + jnp.log(l_tile)

    def carry(self, n_blocks):
        self.kprev_ref[...] = self._block(self.k_ref, n_blocks - 1, slice(None))
        self.vprev_ref[...] = self._block(self.v_ref, n_blocks - 1, slice(None))


def _rest_attn_kernel(h_ref, w_ref, half_bias_ref, bias_ref, q_ref, k_ref, v_ref,
                      out_ref, o_ref, lse_ref, wbf_ref, kprev_ref, vprev_ref,
                      *, tn, n_gate_blocks, steps_per_group, tiles_per_sub):
    j, i = pl.program_id(0), pl.program_id(1)
    step = j * pl.num_programs(1) + i

    @pl.when(i == 0)
    def _():
        wbf_ref[...] = w_ref[...].astype(BF16)

    @pl.when(step == 0)
    def _():
        kprev_ref[...] = jnp.zeros_like(kprev_ref)
        vprev_ref[...] = jnp.zeros_like(vprev_ref)

    n_blocks = tn // MXU_WIDTH
    per = ATTN_TB // BLK
    tiles_per_step = n_blocks // per
    group = step // steps_per_group
    tps = functools.reduce(lambda acc, g: jnp.where(group == g, tiles_per_sub[g], acc),
                           range(len(tiles_per_sub)), tiles_per_sub[0])
    tile0 = (step % steps_per_group) * tiles_per_step
    firsts = [((tile0 + n // per) & (tps - 1)) == 0 if n % per == 0 else False for n in range(n_blocks)]

    def body(epilogue):
        attn = _AttentionBlocks(bias_ref, q_ref, k_ref, v_ref, o_ref, lse_ref, kprev_ref, vprev_ref, firsts)
        for c in range(n_blocks):
            cs = slice(c * MXU_WIDTH, (c + 1) * MXU_WIDTH)
            attn.score_phase(c)
            acc = jnp.dot(h_ref[...], wbf_ref[:, cs], preferred_element_type=F32)
            attn.value_phase(c)
            out_ref[:, cs] = epilogue(acc, cs, half_bias_ref).astype(out_ref.dtype)
        attn.carry(n_blocks)

    pl.when(j < n_gate_blocks)(functools.partial(body, _gate_epilogue))
    pl.when(j >= n_gate_blocks)(functools.partial(body, _silu_epilogue))


def _project_rest_with_attention(h, w_in, half_bias, qkv_t, seq, *, n_gate_blocks, w_block_of):
    t, dm = h.shape
    tm, tn = PROJ_TM, PROJ_TN
    n_blocks = n_gate_blocks + 2
    rows = t // tm
    n_groups, _, n_tiles = qkv_t.shape[:3]
    blocks_per_step = tn // MXU_WIDTH
    tiles_per_step = blocks_per_step * BLK // ATTN_TB
    steps_per_group = n_tiles // tiles_per_step
    assert n_groups * steps_per_group == n_blocks * rows, "attention tiles must fill the projection steps"
    tiles_per_sub = tuple(seq // d // ATTN_TB for _, d in DILATED_GROUPS)
    step = lambda j, i: j * rows + i
    tile = lambda part: pl.BlockSpec(
        (None, None, tiles_per_step, ATTN_OUT, ATTN_TB),
        lambda j, i: (step(j, i) // steps_per_group, part, step(j, i) % steps_per_group, 0, 0))
    tokens = tiles_per_step * ATTN_TB
    per_group = lambda width: pl.BlockSpec(
        (None, tokens, width), lambda j, i: (step(j, i) // steps_per_group, step(j, i) % steps_per_group, 0))
    return pl.pallas_call(
        functools.partial(_rest_attn_kernel, tn=tn, n_gate_blocks=n_gate_blocks,
                          steps_per_group=steps_per_group, tiles_per_sub=tiles_per_sub),
        grid=(n_blocks, rows),
        in_specs=[pl.BlockSpec((tm, dm), lambda j, i: (i, 0)),
                  pl.BlockSpec((dm, tn), lambda j, i: (0, w_block_of(j))),
                  pl.BlockSpec((1, tn), lambda j, i: (0, jnp.minimum(j, n_gate_blocks - 1))),
                  pl.BlockSpec((2, BLK, 2 * BLK), lambda j, i: (0, 0, 0), pipeline_mode=pl.Buffered(1)),
                  tile(0), tile(1), tile(2)],
        out_specs=[pl.BlockSpec((tm, tn), lambda j, i: (i, j)),
                   per_group(ATTN_OUT), per_group(LANES)],
        out_shape=[jax.ShapeDtypeStruct((t, n_blocks * tn), BF16),
                   jax.ShapeDtypeStruct((n_groups, t, ATTN_OUT), BF16),
                   jax.ShapeDtypeStruct((n_groups, t, LANES), F32)],
        scratch_shapes=[pltpu.VMEM((dm, tn), BF16),
                        pltpu.VMEM((ATTN_OUT, BLK), BF16), pltpu.VMEM((ATTN_OUT, BLK), BF16)],
        compiler_params=pltpu.CompilerParams(
            dimension_semantics=("arbitrary", "arbitrary"),
            vmem_limit_bytes=VMEM_LIMIT_BYTES),
        name="proj_rest_attention",
    )(h, w_in, half_bias, jnp.asarray(_band_bias()), qkv_t, qkv_t, qkv_t)


def _rows(ref):
    return jnp.concatenate([ref[r] for r in range(ref.shape[0])], axis=0)


def _unpermute_f32(q, x):
    hi = x.astype(BF16).astype(F32)
    r1 = x - hi
    mid = r1.astype(BF16).astype(F32)
    lo = r1 - mid
    packed = hi + pltpu.roll(mid, HEADS_PER_GROUP, 1) + pltpu.roll(lo, 2 * HEADS_PER_GROUP, 1)
    y = jnp.dot(q, packed.astype(BF16), preferred_element_type=F32)
    return (y + pltpu.roll(y, LANES - HEADS_PER_GROUP, 1)) + pltpu.roll(y, LANES - 2 * HEADS_PER_GROUP, 1)


def _tail_kernel(o0_ref, o1_ref, o2_ref, l0_ref, l1_ref, l2_ref, q4_ref, q16_ref, ga_ref, gp_ref,
                 za_ref, zp_ref, u_ref, uh_ref, wba_ref, wbp_ref, pm_ref, ps_ref,
                 out_ref, a_ref, b_ref, *, tm, seq, n_tiles):
    step = pl.program_id(0)

    @pl.when(step == 0)
    def _():
        a_ref[...] = jnp.zeros_like(a_ref)
        b_ref[...] = jnp.zeros_like(b_ref)

    body = functools.partial(
        _tail_step, o0_ref, o1_ref, o2_ref, l0_ref, l1_ref, l2_ref, q4_ref, q16_ref, ga_ref, gp_ref,
        za_ref, zp_ref, u_ref, uh_ref, wba_ref, wbp_ref, pm_ref, ps_ref, out_ref,
        pos0=(jnp.minimum(step, n_tiles - 1) * tm) % seq, tm=tm)
    for parity in range(2):
        pl.when(step % 2 == parity)(
            functools.partial(body, a_ref.at[1 - parity], b_ref.at[1 - parity],
                              a_ref.at[parity], b_ref.at[parity]))


def _tail_step(o0_ref, o1_ref, o2_ref, l0_ref, l1_ref, l2_ref, q4_ref, q16_ref, ga_ref, gp_ref,
               za_ref, zp_ref, u_ref, uh_ref, wba_ref, wbp_ref, pm_ref, ps_ref, out_ref,
               a_in_ref, b_in_ref, a_ref, b_ref, *, pos0, tm):
    q4, q16 = q4_ref[...], q16_ref[...]
    o0 = o0_ref[...]
    o1 = jnp.dot(q4, _rows(o1_ref), preferred_element_type=F32)
    o2 = jnp.dot(q16, _rows(o2_ref), preferred_element_type=F32)
    l0 = l0_ref[...]
    l1 = _unpermute_f32(q4, _rows(l1_ref))
    l2 = _unpermute_f32(q16, _rows(l2_ref))

    y_attn = jnp.dot(a_in_ref[...], wba_ref[...], preferred_element_type=F32)
    y_pool = jnp.dot(b_in_ref[...], wbp_ref[...], preferred_element_type=F32)
    merged = ga_ref[...].astype(F32) * y_attn + gp_ref[...].astype(F32) * y_pool
    out_ref[...] = merged.astype(out_ref.dtype)

    mx = jnp.maximum(jnp.maximum(l0, l1), l2)
    e0, e1, e2 = jnp.exp(l0 - mx), jnp.exp(l1 - mx), jnp.exp(l2 - mx)
    inv = 1.0 / (e0 + e1 + e2)
    w0, w1, w2 = e0 * inv, e1 * inv, e2 * inv
    for h in range(HEADS_PER_GROUP):
        cs = slice(h * HEAD_DIM, (h + 1) * HEAD_DIM)
        a = (w0[:, h:h + 1] * o0[:, cs].astype(F32)
             + w1[:, h:h + 1] * o1[:, cs]
             + w2[:, h:h + 1] * o2[:, cs])
        a_ref[:, cs] = (a * za_ref[:, cs].astype(F32)).astype(BF16)

    halo = uh_ref[...]
    halo = jnp.where(pos0 == 0, jnp.zeros_like(halo), halo)
    pos = pos0 + lax.broadcasted_iota(jnp.int32, (tm, 1), 0)
    pooled = []
    for g, ksz in enumerate(POOL_SIZES):
        cs = slice(g * POOL_GROUP, (g + 1) * POOL_GROUP)
        ext = jnp.concatenate([halo[:, cs], u_ref[:, cs]], axis=0)
        win, shift = ext, 1
        while shift < ksz:
            win = win + pltpu.roll(win, shift, 0)
            shift *= 2
        u = ext[POOL_HALO:]
        cnt = jnp.minimum(pos + 1, ksz).astype(F32)
        dlt = win[POOL_HALO:] / cnt - u
        pooled.append(jnp.dot(dlt.astype(BF16), pm_ref[g], preferred_element_type=F32))
    pool = jnp.concatenate(pooled, axis=1) * ps_ref[...]
    b_ref[...] = (pool * zp_ref[...].astype(F32)).astype(BF16)


def _residual_epilogue(acc, cs, x_ref):
    return acc + x_ref[:, cs]


def _tail(os_, lses, rest, u, wba, wbp, pm, ps, batch, seq, *, gate_block0, za_block, zp_block):
    t, dm = u.shape[0], D_MODEL
    tm = PERM_TM
    tiles = seq // tm
    n_tiles = t // tm
    halo_per_tile = tm // POOL_HALO
    resident = functools.partial(pl.BlockSpec, pipeline_mode=pl.Buffered(1))
    cur = lambda i: jnp.minimum(i, n_tiles - 1)
    prv = lambda i: jnp.maximum(i - 1, 0)
    row = lambda blk: (lambda i: (cur(i), blk))
    row_prv = lambda blk: (lambda i: (prv(i), blk))
    gate_blocks = D_MODEL // ATTN_OUT
    unperms = [jnp.asarray(_deinterleave_matrix(d).T, BF16) for _, d in DILATED_GROUPS[1:]]
    o_in, l_in, o_specs, l_specs = [], [], [], []
    for g, (_, d) in enumerate(DILATED_GROUPS):
        for arr, width, ins, specs in ((os_, ATTN_OUT, o_in, o_specs), (lses, LANES, l_in, l_specs)):
            if d == 1:
                ins.append(arr)
                specs.append(pl.BlockSpec((None, tm, width), lambda i, g=g: (g, cur(i), 0)))
            else:
                ins.append(arr.reshape(arr.shape[0], batch, d, seq // d, width))
                specs.append(pl.BlockSpec((None, None, d, tm // d, width),
                                          lambda i, g=g: (g, cur(i) // tiles, 0, cur(i) % tiles, 0)))
    in_specs = (
        o_specs + l_specs
        + [resident((tm, tm), lambda i: (0, 0)), resident((tm, tm), lambda i: (0, 0)),
           pl.BlockSpec((tm, D_MODEL), row_prv(gate_block0 // gate_blocks)),
           pl.BlockSpec((tm, D_MODEL), row_prv(gate_block0 // gate_blocks + 1)),
           pl.BlockSpec((tm, ATTN_OUT), row(za_block)),
           pl.BlockSpec((tm, POOL_WIDTH), row(zp_block)),
           pl.BlockSpec((tm, POOL_WIDTH), row(0)),
           pl.BlockSpec((POOL_HALO, POOL_WIDTH),
                        lambda i: (jnp.maximum(cur(i) * halo_per_tile - 1, 0), 0)),
           resident(wba.shape, lambda i: (0, 0)),
           resident(wbp.shape, lambda i: (0, 0)),
           resident(pm.shape, lambda i: (0, 0, 0)),
           resident(ps.shape, lambda i: (0, 0))])
    return pl.pallas_call(
        functools.partial(_tail_kernel, tm=tm, seq=seq, n_tiles=n_tiles),
        grid=(n_tiles + 1,),
        in_specs=in_specs,
        out_specs=pl.BlockSpec((tm, dm), row_prv(0)),
        out_shape=jax.ShapeDtypeStruct((t, dm), BF16),
        scratch_shapes=[pltpu.VMEM((2, tm, ATTN_OUT), BF16), pltpu.VMEM((2, tm, POOL_WIDTH), BF16)],
        compiler_params=pltpu.CompilerParams(
            dimension_semantics=("arbitrary",),
            vmem_limit_bytes=VMEM_LIMIT_BYTES),
        name="merge_pool",
    )(*o_in, *l_in, *unperms, rest, rest, rest, rest, u, u, wba, wbp, pm, ps)


def kernel(x, norm_gain, w_in, b_gates, q_norm_gain, k_norm_gain, pool_maps, pool_scale,
           w_branch_attn, w_branch_pool, w_out):
    batch, seq, dm = x.shape
    t = batch * seq
    x2 = x.reshape(t, dm)
    hs, u = _rmsnorm_and_project_u(x2, norm_gain, w_in, batch, seq, w_block=10)

    lane_rep = lambda g: jnp.broadcast_to(g.astype(F32)[:, None], (HEAD_DIM, LANES))
    gains = jnp.stack([lane_rep(q_norm_gain * HEAD_DIM ** -0.5), lane_rep(k_norm_gain)])
    tables = [_rope_tables_t(seq, d) for _, d in DILATED_GROUPS]
    qkv_t = _project_qkv(hs, w_in, gains, jnp.stack([c for c, _ in tables]),
                         jnp.stack([s for _, s in tables]), seq)

    n_gate_blocks = 2 * D_MODEL // PROJ_TN
    half_bias = (0.5 * b_gates.astype(F32)).reshape(1, 2 * D_MODEL)
    rest_w_block = lambda j: jnp.where(j < 4, j + 12, jnp.where(j == 4, 9, 11))
    rest, o_all, lse_all = _project_rest_with_attention(
        hs[0], w_in, half_bias, qkv_t, seq, n_gate_blocks=n_gate_blocks, w_block_of=rest_w_block)

    merged = _tail(o_all, lse_all, rest, u,
                   w_branch_attn.astype(BF16), w_branch_pool.astype(BF16), pool_maps.astype(BF16),
                   pool_scale.astype(F32).reshape(1, POOL_WIDTH), batch, seq,
                   gate_block0=0, za_block=4, zp_block=5)
    out = _project(merged, w_out, n_blocks=dm // PROJ_TN, w_block_of=lambda j: j,
                   epilogues=((0, _residual_epilogue),), out_dtype=F32, aux=(x2,),
                   aux_specs=(pl.BlockSpec((PROJ_TM, PROJ_TN), lambda j, i: (i, j)),),
                   name="proj_out")
    return out.reshape(batch, seq, dm)
```

```python
import functools

import numpy as np
import jax
import jax.numpy as jnp
from jax import lax
from jax.experimental import pallas as pl
from jax.experimental.pallas import tpu as pltpu

D_MODEL = 2048
HEAD_DIM = 128
HEADS_PER_GROUP = 8
DILATED_GROUPS = ((128, 1), (512, 4), (2048, 16))
N_ATTN_GROUPS = len(DILATED_GROUPS)
ATTN_OUT = HEADS_PER_GROUP * HEAD_DIM
BLK = 128
ROPE_THETA = 500000.0
ROT_DIM = HEAD_DIM // 4
ROT_HALF = ROT_DIM // 2
POOL_SIZES = (2, 4, 8, 16)
POOL_WIDTH = D_MODEL // 2
POOL_GROUP = POOL_WIDTH // len(POOL_SIZES)
NORM_EPS = 1e-6

LANES = 128
MXU_WIDTH = 256
VMEM_LIMIT_BYTES = 56 * 1024 * 1024

PROJ_TM = 1024
PROJ_TN = 1024
PERM_TM = 256
NORM_TM = 512
ATTN_TB = 256
POOL_HALO = 16

F32 = jnp.float32
BF16 = jnp.bfloat16

_NT = (((1,), (1,)), ((), ()))
_TN = (((0,), (0,)), ((), ()))


def _deinterleave_matrix(d):
    n = PERM_TM // d
    i = np.arange(PERM_TM)
    p = np.zeros((PERM_TM, PERM_TM), np.float32)
    p[i, (i % n) * d + i // n] = 1.0
    return p


def _norm_u_kernel(x_ref, g_ref, p4_ref, p16_ref, w_ref, h1_ref, h4_ref, h16_ref, u_ref,
                   hbuf_ref, wbf_ref):
    step = pl.program_id(0)

    @pl.when(step == 0)
    def _():
        wbf_ref[...] = w_ref[...].astype(BF16)
        hbuf_ref[...] = jnp.zeros_like(hbuf_ref)

    def body(h_in_ref, h_out_ref):
        for c in range(u_ref.shape[1] // MXU_WIDTH):
            cs = slice(c * MXU_WIDTH, (c + 1) * MXU_WIDTH)
            u_ref[:, cs] = jnp.dot(h_in_ref[...], wbf_ref[:, cs], preferred_element_type=F32)
        for sb in range(x_ref.shape[0] // PERM_TM):
            rs = slice(sb * PERM_TM, (sb + 1) * PERM_TM)
            x = x_ref[rs, :]
            ms = jnp.mean(x * x, axis=-1, keepdims=True)
            h = (x * lax.rsqrt(ms + NORM_EPS) * g_ref[...]).astype(BF16)
            h1_ref[rs, :] = h
            h_out_ref[rs, :] = h
            for p_ref, out_ref in ((p4_ref, h4_ref), (p16_ref, h16_ref)):
                d = out_ref.shape[0]
                n = PERM_TM // d
                hp = jnp.dot(p_ref[...], h, preferred_element_type=F32).astype(BF16)
                for r in range(d):
                    out_ref[r, sb * n:(sb + 1) * n, :] = hp[r * n:(r + 1) * n, :]

    for parity in range(2):
        pl.when(step % 2 == parity)(
            functools.partial(body, hbuf_ref.at[1 - parity], hbuf_ref.at[parity]))


def _rmsnorm_and_project_u(x2, gain, w_in, batch, seq, *, w_block):
    t, dm = x2.shape
    tm, tn = NORM_TM, PROJ_TN
    tiles = seq // tm
    n_tiles = t // tm
    perms = [jnp.asarray(_deinterleave_matrix(d), BF16) for _, d in DILATED_GROUPS[1:]]
    cur = lambda i: jnp.minimum(i, n_tiles - 1)
    prv = lambda i: jnp.maximum(i - 1, 0)
    out_shape = [jax.ShapeDtypeStruct((t, dm), BF16)]
    out_specs = [pl.BlockSpec((tm, dm), lambda i: (cur(i), 0))]
    for _, d in DILATED_GROUPS[1:]:
        out_shape.append(jax.ShapeDtypeStruct((batch, d, seq // d, dm), BF16))
        out_specs.append(pl.BlockSpec((None, d, tm // d, dm),
                                      lambda i: (cur(i) // tiles, 0, cur(i) % tiles, 0)))
    out_shape.append(jax.ShapeDtypeStruct((t, tn), F32))
    out_specs.append(pl.BlockSpec((tm, tn), lambda i: (prv(i), 0)))
    const = lambda i: (0, 0)
    resident = functools.partial(pl.BlockSpec, pipeline_mode=pl.Buffered(1))
    h1, h4, h16, u = pl.pallas_call(
        _norm_u_kernel,
        grid=(n_tiles + 1,),
        in_specs=[pl.BlockSpec((tm, dm), lambda i: (cur(i), 0)),
                  pl.BlockSpec((1, dm), const),
                  pl.BlockSpec((PERM_TM, PERM_TM), const),
                  pl.BlockSpec((PERM_TM, PERM_TM), const),
                  resident((dm, tn), lambda i: (0, w_block))],
        out_specs=out_specs,
        out_shape=out_shape,
        scratch_shapes=[pltpu.VMEM((2, tm, dm), BF16), pltpu.VMEM((dm, tn), BF16)],
        compiler_params=pltpu.CompilerParams(dimension_semantics=("arbitrary",),
                                             vmem_limit_bytes=VMEM_LIMIT_BYTES),
        name="rmsnorm_proj_u",
    )(x2, gain.reshape(1, dm), *perms, w_in)
    return (h1, h4.reshape(t, dm), h16.reshape(t, dm)), u


def _qkv_kernel(*refs, tm, tn):
    h_refs = refs[:N_ATTN_GROUPS]
    w_ref, gain_ref, cos_ref, sin_ref, out_ref, wt_ref = refs[N_ATTN_GROUPS:]
    group = pl.program_id(0) // 3
    part = pl.program_id(0) % 3

    @pl.when(pl.program_id(1) == 0)
    def _():
        for c in range(tn // MXU_WIDTH):
            cs = slice(c * MXU_WIDTH, (c + 1) * MXU_WIDTH)
            wt_ref[cs, :] = w_ref[:, cs].T.astype(BF16)

    for g, h_ref in enumerate(h_refs):
        body = functools.partial(_qkv_body, h_ref, gain_ref, cos_ref, sin_ref, out_ref, wt_ref, tm=tm, tn=tn)
        pl.when((group == g) & (part < 2))(functools.partial(body, normed=True))
        pl.when((group == g) & (part == 2))(functools.partial(body, normed=False))


def _qkv_body(h_ref, gain_ref, cos_ref, sin_ref, out_ref, wt_ref, *, tm, tn, normed):
    def chunk(c):
        ts = slice(c * ATTN_TB, (c + 1) * ATTN_TB)
        return lax.dot_general(wt_ref[...], h_ref[ts, :], _NT, preferred_element_type=F32)

    if not normed:
        for c in range(tm // ATTN_TB):
            out_ref[c] = chunk(c).astype(out_ref.dtype)
    else:
        for c in range(tm // ATTN_TB):
            acc = chunk(c)
            for hh in range(tn // HEAD_DIM):
                hs = slice(hh * HEAD_DIM, (hh + 1) * HEAD_DIM)
                for lt in range(ATTN_TB // LANES):
                    ls = slice(lt * LANES, (lt + 1) * LANES)
                    pos = slice(c * ATTN_TB + lt * LANES, c * ATTN_TB + (lt + 1) * LANES)
                    a = acc[hs, ls]
                    ss = jnp.sum(a * a, axis=0, keepdims=True)
                    r = lax.rsqrt(ss * (1.0 / HEAD_DIM) + NORM_EPS)
                    ag = a * gain_ref[...]
                    lo, hi = ag[0:ROT_HALF], ag[ROT_HALF:ROT_DIM]
                    cos, sin = cos_ref[:, pos], sin_ref[:, pos]
                    y = jnp.concatenate([lo * cos - hi * sin, hi * cos + lo * sin, ag[ROT_DIM:]], axis=0)
                    out_ref[c, hs, ls] = (y * r).astype(out_ref.dtype)


def _project_qkv(hs, w_in, gains, cos_t, sin_t, seq):
    t, dm = hs[0].shape
    tm, tn = PROJ_TM, PROJ_TN
    seq_tiles = seq // tm
    group = lambda gj: gj // 3
    part = lambda gj: gj % 3
    h_spec = lambda g: pl.BlockSpec((tm, dm), lambda gj, i: (jnp.where(group(gj) == g, i, 0), 0))
    table = pl.BlockSpec((None, ROT_HALF, tm), lambda gj, i: (group(gj), 0, i % seq_tiles))
    return pl.pallas_call(
        functools.partial(_qkv_kernel, tm=tm, tn=tn),
        grid=(3 * N_ATTN_GROUPS, t // tm),
        in_specs=[*(h_spec(g) for g in range(N_ATTN_GROUPS)),
                  pl.BlockSpec((dm, tn), lambda gj, i: (0, part(gj) * N_ATTN_GROUPS + group(gj))),
                  pl.BlockSpec((None, HEAD_DIM, LANES), lambda gj, i: (jnp.minimum(part(gj), 1), 0, 0)),
                  table, table],
        out_specs=pl.BlockSpec((None, None, tm // ATTN_TB, tn, ATTN_TB),
                               lambda gj, i: (group(gj), part(gj), i, 0, 0)),
        out_shape=jax.ShapeDtypeStruct((N_ATTN_GROUPS, 3, t // ATTN_TB, tn, ATTN_TB), BF16),
        scratch_shapes=[pltpu.VMEM((tn, dm), BF16)],
        compiler_params=pltpu.CompilerParams(
            dimension_semantics=("arbitrary", "arbitrary"),
            vmem_limit_bytes=VMEM_LIMIT_BYTES),
        name="proj_qkv",
    )(*hs, w_in, gains, cos_t, sin_t)


def _rope_tables_t(seq, d):
    inv_freq = ROPE_THETA ** (-jnp.arange(0, ROT_DIM, 2, dtype=F32) / ROT_DIM)
    pos = jnp.arange(seq, dtype=jnp.int32).reshape(seq // d, d).T.reshape(seq)
    ang = pos.astype(F32)[None, :] * inv_freq[:, None]
    return jnp.cos(ang), jnp.sin(ang)


def _identity_epilogue(acc, cs):
    del cs
    return acc


def _gate_epilogue(acc, cs, half_bias_ref):
    return 0.5 + 0.5 * jnp.tanh(0.5 * acc + half_bias_ref[:, cs])


def _silu_epilogue(acc, cs, half_bias_ref):
    del cs, half_bias_ref
    half = 0.5 * acc
    return half + half * jnp.tanh(half)


def _proj_kernel(*refs, epilogues, n_aux, tn):
    h_ref, w_ref = refs[0], refs[1]
    aux = refs[2:2 + n_aux]
    out_ref = refs[2 + n_aux]
    wbf_ref = refs[3 + n_aux]
    j = pl.program_id(0)

    @pl.when(pl.program_id(1) == 0)
    def _():
        wbf_ref[...] = w_ref[...].astype(BF16)

    def body(epilogue):
        for c in range(tn // MXU_WIDTH):
            cs = slice(c * MXU_WIDTH, (c + 1) * MXU_WIDTH)
            acc = jnp.dot(h_ref[...], wbf_ref[:, cs], preferred_element_type=F32)
            out_ref[:, cs] = epilogue(acc, cs, *aux).astype(out_ref.dtype)

    if len(epilogues) == 1:
        body(epilogues[0][1])
    else:
        bounds = [first for first, _ in epilogues[1:]] + [None]
        for (first, fn), last in zip(epilogues, bounds):
            cond = (j >= first) if last is None else ((j >= first) & (j < last))
            pl.when(cond)(functools.partial(body, fn))


def _project(h, w_in, *, n_blocks, w_block_of, epilogues, out_dtype, aux=(), aux_specs=(), name):
    t, d = h.shape
    tm, tn = PROJ_TM, PROJ_TN
    kern = functools.partial(_proj_kernel, epilogues=epilogues, n_aux=len(aux), tn=tn)
    return pl.pallas_call(
        kern,
        grid=(n_blocks, t // tm),
        in_specs=[pl.BlockSpec((tm, d), lambda j, i: (i, 0)),
                  pl.BlockSpec((d, tn), lambda j, i: (0, w_block_of(j))),
                  *aux_specs],
        out_specs=pl.BlockSpec((tm, tn), lambda j, i: (i, j)),
        out_shape=jax.ShapeDtypeStruct((t, n_blocks * tn), out_dtype),
        scratch_shapes=[pltpu.VMEM((d, tn), BF16)],
        compiler_params=pltpu.CompilerParams(
            dimension_semantics=("arbitrary", "arbitrary"),
            vmem_limit_bytes=VMEM_LIMIT_BYTES),
        name=name,
    )(h, w_in, *aux)


def _band_bias():
    i = np.arange(BLK)[:, None]
    j = np.arange(2 * BLK)[None, :]
    band = (j >= i) & (j <= i + BLK)
    first = band & (j >= BLK)
    return np.where(np.stack([band, first]), 0.0, -np.inf).astype(np.float32)


class _AttentionBlocks:
    def __init__(self, bias_ref, q_ref, k_ref, v_ref, o_ref, lse_ref, kprev_ref, vprev_ref, firsts):
        self.bias_ref, self.q_ref, self.k_ref, self.v_ref = bias_ref, q_ref, k_ref, v_ref
        self.o_ref, self.lse_ref, self.kprev_ref, self.vprev_ref = o_ref, lse_ref, kprev_ref, vprev_ref
        self.firsts = firsts
        self.per = ATTN_TB // BLK
        self.scores = {}

    def _block(self, ref, n, cs):
        lo = (n % self.per) * BLK
        return ref[n // self.per, cs, lo:lo + BLK]

    def _with_previous(self, ref, prev_ref, n, cs):
        prev = prev_ref[cs, :] if n == 0 else self._block(ref, n - 1, cs)
        return jnp.concatenate([prev, self._block(ref, n, cs)], axis=1)

    def score_phase(self, n):
        first = self.firsts[n]
        bias = self.bias_ref[int(first)] if isinstance(first, bool) else self.bias_ref[jnp.where(first, 1, 0)]
        for h in range(HEADS_PER_GROUP):
            cs = slice(h * HEAD_DIM, (h + 1) * HEAD_DIM)
            kk = self._with_previous(self.k_ref, self.kprev_ref, n, cs)
            s = lax.dot_general(self._block(self.q_ref, n, cs), kk, _TN, preferred_element_type=F32)
            self.scores[n, h] = s + bias

    def value_phase(self, n):
        ts = slice(n * BLK, (n + 1) * BLK)
        lane = lax.broadcasted_iota(jnp.int32, (BLK, LANES), 1)
        m_tile = jnp.zeros((BLK, LANES), F32)
        l_tile = jnp.ones((BLK, LANES), F32)
        for h in range(HEADS_PER_GROUP):
            cs = slice(h * HEAD_DIM, (h + 1) * HEAD_DIM)
            s = self.scores.pop((n, h))
            m = jnp.max(s, axis=-1, keepdims=True)
            p = jnp.exp(s - m)
            l = jnp.sum(p, axis=-1, keepdims=True)
            vv = self._with_previous(self.v_ref, self.vprev_ref, n, cs)
            o = lax.dot_general(p.astype(BF16), vv, _NT, preferred_element_type=F32)
            self.o_ref[ts, cs] = (o / l).astype(self.o_ref.dtype)
            m_tile = jnp.where(lane == h, m, m_tile)
            l_tile = jnp.where(lane == h, l, l_tile)
        self.lse_ref[ts, :] = m_tile + jnp.log(l_tile)

    def carry(self, n_blocks):
        self.kprev_ref[...] = self._block(self.k_ref, n_blocks - 1, slice(None))
        self.vprev_ref[...] = self._block(self.v_ref, n_blocks - 1, slice(None))


def _rest_attn_kernel(h_ref, w_ref, half_bias_ref, bias_ref, q_ref, k_ref, v_ref,
                      out_ref, o_ref, lse_ref, wbf_ref, kprev_ref, vprev_ref,
                      *, tn, n_gate_blocks, steps_per_group, tiles_per_sub):
    j, i = pl.program_id(0), pl.program_id(1)
    step = j * pl.num_programs(1) + i

    @pl.when(i == 0)
    def _():
        wbf_ref[...] = w_ref[...].astype(BF16)

    @pl.when(step == 0)
    def _():
        kprev_ref[...] = jnp.zeros_like(kprev_ref)
        vprev_ref[...] = jnp.zeros_like(vprev_ref)

    n_blocks = tn // MXU_WIDTH
    per = ATTN_TB // BLK
    tiles_per_step = n_blocks // per
    group = step // steps_per_group
    tps = functools.reduce(lambda acc, g: jnp.where(group == g, tiles_per_sub[g], acc),
                           range(len(tiles_per_sub)), tiles_per_sub[0])
    tile0 = (step % steps_per_group) * tiles_per_step
    firsts = [((tile0 + n // per) & (tps - 1)) == 0 if n % per == 0 else False for n in range(n_blocks)]

    def body(epilogue):
        attn = _AttentionBlocks(bias_ref, q_ref, k_ref, v_ref, o_ref, lse_ref, kprev_ref, vprev_ref, firsts)
        for c in range(n_blocks):
            cs = slice(c * MXU_WIDTH, (c + 1) * MXU_WIDTH)
            attn.score_phase(c)
            acc = jnp.dot(h_ref[...], wbf_ref[:, cs], preferred_element_type=F32)
            if c > 0:
                attn.value_phase(c - 1)
            out_ref[:, cs] = epilogue(acc, cs, half_bias_ref).astype(out_ref.dtype)
        attn.value_phase(n_blocks - 1)
        attn.carry(n_blocks)

    pl.when(j < n_gate_blocks)(functools.partial(body, _gate_epilogue))
    pl.when(j >= n_gate_blocks)(functools.partial(body, _silu_epilogue))


def _project_rest_with_attention(h, w_in, half_bias, qkv_t, seq, *, n_gate_blocks, w_block_of):
    t, dm = h.shape
    tm, tn = PROJ_TM, PROJ_TN
    n_blocks = n_gate_blocks + 2
    rows = t // tm
    n_groups, _, n_tiles = qkv_t.shape[:3]
    blocks_per_step = tn // MXU_WIDTH
    tiles_per_step = blocks_per_step * BLK // ATTN_TB
    steps_per_group = n_tiles // tiles_per_step
    assert n_groups * steps_per_group == n_blocks * rows, "attention tiles must fill the projection steps"
    tiles_per_sub = tuple(seq // d // ATTN_TB for _, d in DILATED_GROUPS)
    step = lambda j, i: j * rows + i
    tile = lambda part: pl.BlockSpec(
        (None, None, tiles_per_step, ATTN_OUT, ATTN_TB),
        lambda j, i: (step(j, i) // steps_per_group, part, step(j, i) % steps_per_group, 0, 0))
    tokens = tiles_per_step * ATTN_TB
    per_group = lambda width: pl.BlockSpec(
        (None, tokens, width), lambda j, i: (step(j, i) // steps_per_group, step(j, i) % steps_per_group, 0))
    return pl.pallas_call(
        functools.partial(_rest_attn_kernel, tn=tn, n_gate_blocks=n_gate_blocks,
                          steps_per_group=steps_per_group, tiles_per_sub=tiles_per_sub),
        grid=(n_blocks, rows),
        in_specs=[pl.BlockSpec((tm, dm), lambda j, i: (i, 0)),
                  pl.BlockSpec((dm, tn), lambda j, i: (0, w_block_of(j))),
                  pl.BlockSpec((1, tn), lambda j, i: (0, jnp.minimum(j, n_gate_blocks - 1))),
                  pl.BlockSpec((2, BLK, 2 * BLK), lambda j, i: (0, 0, 0), pipeline_mode=pl.Buffered(1)),
                  tile(0), tile(1), tile(2)],
        out_specs=[pl.BlockSpec((tm, tn), lambda j, i: (i, j)),
                   per_group(ATTN_OUT), per_group(LANES)],
        out_shape=[jax.ShapeDtypeStruct((t, n_blocks * tn), BF16),
                   jax.ShapeDtypeStruct((n_groups, t, ATTN_OUT), BF16),
                   jax.ShapeDtypeStruct((n_groups, t, LANES), F32)],
        scratch_shapes=[pltpu.VMEM((dm, tn), BF16),
                        pltpu.VMEM((ATTN_OUT, BLK), BF16), pltpu.VMEM((ATTN_OUT, BLK), BF16)],
        compiler_params=pltpu.CompilerParams(
            dimension_semantics=("arbitrary", "arbitrary"),
            vmem_limit_bytes=VMEM_LIMIT_BYTES),
        name="proj_rest_attention",
    )(h, w_in, half_bias, jnp.asarray(_band_bias()), qkv_t, qkv_t, qkv_t)


def _rows(ref):
    return jnp.concatenate([ref[r] for r in range(ref.shape[0])], axis=0)


def _unpermute_f32(q, x):
    hi = x.astype(BF16).astype(F32)
    r1 = x - hi
    mid = r1.astype(BF16).astype(F32)
    lo = r1 - mid
    packed = hi + pltpu.roll(mid, HEADS_PER_GROUP, 1) + pltpu.roll(lo, 2 * HEADS_PER_GROUP, 1)
    y = jnp.dot(q, packed.astype(BF16), preferred_element_type=F32)
    return (y + pltpu.roll(y, LANES - HEADS_PER_GROUP, 1)) + pltpu.roll(y, LANES - 2 * HEADS_PER_GROUP, 1)


def _tail_kernel(o0_ref, o1_ref, o2_ref, l0_ref, l1_ref, l2_ref, q4_ref, q16_ref, ga_ref, gp_ref,
                 za_ref, zp_ref, u_ref, uh_ref, wba32_ref, wbp32_ref, pm32_ref, ps_ref,
                 out_ref, a_ref, b_ref, wba_ref, wbp_ref, pm_ref, *, tm, seq, n_tiles):
    step = pl.program_id(0)

    @pl.when(step == 0)
    def _():
        a_ref[...] = jnp.zeros_like(a_ref)
        b_ref[...] = jnp.zeros_like(b_ref)
        wba_ref[...] = wba32_ref[...].astype(BF16)
        wbp_ref[...] = wbp32_ref[...].astype(BF16)
        pm_ref[...] = pm32_ref[...].astype(BF16)

    body = functools.partial(
        _tail_step, o0_ref, o1_ref, o2_ref, l0_ref, l1_ref, l2_ref, q4_ref, q16_ref, ga_ref, gp_ref,
        za_ref, zp_ref, u_ref, uh_ref, wba_ref, wbp_ref, pm_ref, ps_ref, out_ref,
        pos0=(jnp.minimum(step, n_tiles - 1) * tm) % seq, tm=tm)
    for parity in range(2):
        pl.when(step % 2 == parity)(
            functools.partial(body, a_ref.at[1 - parity], b_ref.at[1 - parity],
                              a_ref.at[parity], b_ref.at[parity]))


def _tail_step(o0_ref, o1_ref, o2_ref, l0_ref, l1_ref, l2_ref, q4_ref, q16_ref, ga_ref, gp_ref,
               za_ref, zp_ref, u_ref, uh_ref, wba_ref, wbp_ref, pm_ref, ps_ref, out_ref,
               a_in_ref, b_in_ref, a_ref, b_ref, *, pos0, tm):
    q4, q16 = q4_ref[...], q16_ref[...]
    o0 = o0_ref[...]
    o1 = jnp.dot(q4, _rows(o1_ref), preferred_element_type=F32)
    o2 = jnp.dot(q16, _rows(o2_ref), preferred_element_type=F32)
    l0 = l0_ref[...]
    l1 = _unpermute_f32(q4, _rows(l1_ref))
    l2 = _unpermute_f32(q16, _rows(l2_ref))

    y_attn = jnp.dot(a_in_ref[...], wba_ref[...], preferred_element_type=F32)
    y_pool = jnp.dot(b_in_ref[...], wbp_ref[...], preferred_element_type=F32)
    merged = ga_ref[...].astype(F32) * y_attn + gp_ref[...].astype(F32) * y_pool
    out_ref[...] = merged.astype(out_ref.dtype)

    mx = jnp.maximum(jnp.maximum(l0, l1), l2)
    e0, e1, e2 = jnp.exp(l0 - mx), jnp.exp(l1 - mx), jnp.exp(l2 - mx)
    inv = 1.0 / (e0 + e1 + e2)
    w0, w1, w2 = e0 * inv, e1 * inv, e2 * inv
    for h in range(HEADS_PER_GROUP):
        cs = slice(h * HEAD_DIM, (h + 1) * HEAD_DIM)
        a = (w0[:, h:h + 1] * o0[:, cs].astype(F32)
             + w1[:, h:h + 1] * o1[:, cs]
             + w2[:, h:h + 1] * o2[:, cs])
        a_ref[:, cs] = (a * za_ref[:, cs].astype(F32)).astype(BF16)

    halo = uh_ref[...]
    halo = jnp.where(pos0 == 0, jnp.zeros_like(halo), halo)
    pos = pos0 + lax.broadcasted_iota(jnp.int32, (tm, 1), 0)
    pooled = []
    for g, ksz in enumerate(POOL_SIZES):
        cs = slice(g * POOL_GROUP, (g + 1) * POOL_GROUP)
        ext = jnp.concatenate([halo[:, cs], u_ref[:, cs]], axis=0)
        win, shift = ext, 1
        while shift < ksz:
            win = win + pltpu.roll(win, shift, 0)
            shift *= 2
        u = ext[POOL_HALO:]
        cnt = jnp.minimum(pos + 1, ksz).astype(F32)
        dlt = win[POOL_HALO:] / cnt - u
        pooled.append(jnp.dot(dlt.astype(BF16), pm_ref[g], preferred_element_type=F32))
    pool = jnp.concatenate(pooled, axis=1) * ps_ref[...]
    b_ref[...] = (pool * zp_ref[...].astype(F32)).astype(BF16)


def _residual_epilogue(acc, cs, x_ref):
    return acc + x_ref[:, cs]


def _tail(os_, lses, rest, u, wba, wbp, pm, ps, batch, seq, *, gate_block0, za_block, zp_block):
    t, dm = u.shape[0], D_MODEL
    tm = PERM_TM
    tiles = seq // tm
    n_tiles = t // tm
    halo_per_tile = tm // POOL_HALO
    resident = functools.partial(pl.BlockSpec, pipeline_mode=pl.Buffered(1))
    cur = lambda i: jnp.minimum(i, n_tiles - 1)
    prv = lambda i: jnp.maximum(i - 1, 0)
    row = lambda blk: (lambda i: (cur(i), blk))
    row_prv = lambda blk: (lambda i: (prv(i), blk))
    gate_blocks = D_MODEL // ATTN_OUT
    unperms = [jnp.asarray(_deinterleave_matrix(d).T, BF16) for _, d in DILATED_GROUPS[1:]]
    o_in, l_in, o_specs, l_specs = [], [], [], []
    for g, (_, d) in enumerate(DILATED_GROUPS):
        for arr, width, ins, specs in ((os_, ATTN_OUT, o_in, o_specs), (lses, LANES, l_in, l_specs)):
            if d == 1:
                ins.append(arr)
                specs.append(pl.BlockSpec((None, tm, width), lambda i, g=g: (g, cur(i), 0)))
            else:
                ins.append(arr.reshape(arr.shape[0], batch, d, seq // d, width))
                specs.append(pl.BlockSpec((None, None, d, tm // d, width),
                                          lambda i, g=g: (g, cur(i) // tiles, 0, cur(i) % tiles, 0)))
    in_specs = (
        o_specs + l_specs
        + [resident((tm, tm), lambda i: (0, 0)), resident((tm, tm), lambda i: (0, 0)),
           pl.BlockSpec((tm, D_MODEL), row_prv(gate_block0 // gate_blocks)),
           pl.BlockSpec((tm, D_MODEL), row_prv(gate_block0 // gate_blocks + 1)),
           pl.BlockSpec((tm, ATTN_OUT), row(za_block)),
           pl.BlockSpec((tm, POOL_WIDTH), row(zp_block)),
           pl.BlockSpec((tm, POOL_WIDTH), row(0)),
           pl.BlockSpec((POOL_HALO, POOL_WIDTH),
                        lambda i: (jnp.maximum(cur(i) * halo_per_tile - 1, 0), 0)),
           resident(wba.shape, lambda i: (0, 0)),
           resident(wbp.shape, lambda i: (0, 0)),
           resident(pm.shape, lambda i: (0, 0, 0)),
           resident(ps.shape, lambda i: (0, 0))])
    return pl.pallas_call(
        functools.partial(_tail_kernel, tm=tm, seq=seq, n_tiles=n_tiles),
        grid=(n_tiles + 1,),
        in_specs=in_specs,
        out_specs=pl.BlockSpec((tm, dm), row_prv(0)),
        out_shape=jax.ShapeDtypeStruct((t, dm), BF16),
        scratch_shapes=[pltpu.VMEM((2, tm, ATTN_OUT), BF16), pltpu.VMEM((2, tm, POOL_WIDTH), BF16),
                        pltpu.VMEM(wba.shape, BF16), pltpu.VMEM(wbp.shape, BF16), pltpu.VMEM(pm.shape, BF16)],
        compiler_params=pltpu.CompilerParams(
            dimension_semantics=("arbitrary",),
            vmem_limit_bytes=VMEM_LIMIT_BYTES),
        name="merge_pool",
    )(*o_in, *l_in, *unperms, rest, rest, rest, rest, u, u, wba, wbp, pm, ps)


def kernel(x, norm_gain, w_in, b_gates, q_norm_gain, k_norm_gain, pool_maps, pool_scale,
           w_branch_attn, w_branch_pool, w_out):
    batch, seq, dm = x.shape
    t = batch * seq
    x2 = x.reshape(t, dm)
    hs, u = _rmsnorm_and_project_u(x2, norm_gain, w_in, batch, seq, w_block=10)

    lane_rep = lambda g: jnp.broadcast_to(g.astype(F32)[:, None], (HEAD_DIM, LANES))
    gains = jnp.stack([lane_rep(q_norm_gain * HEAD_DIM ** -0.5), lane_rep(k_norm_gain)])
    tables = [_rope_tables_t(seq, d) for _, d in DILATED_GROUPS]
    qkv_t = _project_qkv(hs, w_in, gains, jnp.stack([c for c, _ in tables]),
                         jnp.stack([s for _, s in tables]), seq)

    n_gate_blocks = 2 * D_MODEL // PROJ_TN
    half_bias = (0.5 * b_gates.astype(F32)).reshape(1, 2 * D_MODEL)
    rest_w_block = lambda j: jnp.where(j < 4, j + 12, jnp.where(j == 4, 9, 11))
    rest, o_all, lse_all = _project_rest_with_attention(
        hs[0], w_in, half_bias, qkv_t, seq, n_gate_blocks=n_gate_blocks, w_block_of=rest_w_block)

    merged = _tail(o_all, lse_all, rest, u,
                   w_branch_attn, w_branch_pool, pool_maps,
                   pool_scale.astype(F32).reshape(1, POOL_WIDTH), batch, seq,
                   gate_block0=0, za_block=4, zp_block=5)
    out = _project(merged, w_out, n_blocks=dm // PROJ_TN, w_block_of=lambda j: j,
                   epilogues=((0, _residual_epilogue),), out_dtype=F32, aux=(x2,),
                   aux_specs=(pl.BlockSpec((PROJ_TM, PROJ_TN), lambda j, i: (i, j)),),
                   name="proj_out")
    return out.reshape(batch, seq, dm)
```

```python
import functools

import numpy as np
import jax
import jax.numpy as jnp
from jax import lax
from jax.experimental import pallas as pl
from jax.experimental.pallas import tpu as pltpu

D_MODEL = 2048
HEAD_DIM = 128
HEADS_PER_GROUP = 8
DILATED_GROUPS = ((128, 1), (512, 4), (2048, 16))
N_ATTN_GROUPS = len(DILATED_GROUPS)
ATTN_OUT = HEADS_PER_GROUP * HEAD_DIM
BLK = 128
ROPE_THETA = 500000.0
ROT_DIM = HEAD_DIM // 4
ROT_HALF = ROT_DIM // 2
POOL_SIZES = (2, 4, 8, 16)
POOL_WIDTH = D_MODEL // 2
POOL_GROUP = POOL_WIDTH // len(POOL_SIZES)
NORM_EPS = 1e-6

LANES = 128
MXU_WIDTH = 256
VMEM_LIMIT_BYTES = 56 * 1024 * 1024

PROJ_TM = 1024
PROJ_TN = 1024
QKV_TM = 2048
PERM_TM = 256
NORM_TM = 512
ATTN_TB = 256
MERGE_TN = 512
POOL_HALO = 16

F32 = jnp.float32
BF16 = jnp.bfloat16

_NT = (((1,), (1,)), ((), ()))
_TN = (((0,), (0,)), ((), ()))


def _deinterleave_matrix(d):
    n = PERM_TM // d
    i = np.arange(PERM_TM)
    p = np.zeros((PERM_TM, PERM_TM), np.float32)
    p[i, (i % n) * d + i // n] = 1.0
    return p


def _norm_u_kernel(x_ref, g_ref, p4_ref, p16_ref, w_ref, h1_ref, h4_ref, h16_ref, u_ref,
                   hbuf_ref, wbf_ref):
    step = pl.program_id(0)

    @pl.when(step == 0)
    def _():
        wbf_ref[...] = w_ref[...].astype(BF16)
        hbuf_ref[...] = jnp.zeros_like(hbuf_ref)

    def body(h_in_ref, h_out_ref):
        for c in range(u_ref.shape[1] // MXU_WIDTH):
            cs = slice(c * MXU_WIDTH, (c + 1) * MXU_WIDTH)
            u_ref[:, cs] = jnp.dot(h_in_ref[...], wbf_ref[:, cs], preferred_element_type=F32)
        for sb in range(x_ref.shape[0] // PERM_TM):
            rs = slice(sb * PERM_TM, (sb + 1) * PERM_TM)
            x = x_ref[rs, :]
            ms = jnp.mean(x * x, axis=-1, keepdims=True)
            h = (x * lax.rsqrt(ms + NORM_EPS) * g_ref[...]).astype(BF16)
            h1_ref[rs, :] = h
            h_out_ref[rs, :] = h
            for p_ref, out_ref in ((p4_ref, h4_ref), (p16_ref, h16_ref)):
                d = out_ref.shape[0]
                n = PERM_TM // d
                hp = jnp.dot(p_ref[...], h, preferred_element_type=F32).astype(BF16)
                for r in range(d):
                    out_ref[r, sb * n:(sb + 1) * n, :] = hp[r * n:(r + 1) * n, :]

    for parity in range(2):
        pl.when(step % 2 == parity)(
            functools.partial(body, hbuf_ref.at[1 - parity], hbuf_ref.at[parity]))


def _rmsnorm_and_project_u(x2, gain, w_in, batch, seq, *, w_block):
    t, dm = x2.shape
    tm, tn = NORM_TM, PROJ_TN
    tiles = seq // tm
    n_tiles = t // tm
    perms = [jnp.asarray(_deinterleave_matrix(d), BF16) for _, d in DILATED_GROUPS[1:]]
    cur = lambda i: jnp.minimum(i, n_tiles - 1)
    prv = lambda i: jnp.maximum(i - 1, 0)
    out_shape = [jax.ShapeDtypeStruct((t, dm), BF16)]
    out_specs = [pl.BlockSpec((tm, dm), lambda i: (cur(i), 0))]
    for _, d in DILATED_GROUPS[1:]:
        out_shape.append(jax.ShapeDtypeStruct((batch, d, seq // d, dm), BF16))
        out_specs.append(pl.BlockSpec((None, d, tm // d, dm),
                                      lambda i: (cur(i) // tiles, 0, cur(i) % tiles, 0)))
    out_shape.append(jax.ShapeDtypeStruct((t, tn), F32))
    out_specs.append(pl.BlockSpec((tm, tn), lambda i: (prv(i), 0)))
    const = lambda i: (0, 0)
    resident = functools.partial(pl.BlockSpec, pipeline_mode=pl.Buffered(1))
    h1, h4, h16, u = pl.pallas_call(
        _norm_u_kernel,
        grid=(n_tiles + 1,),
        in_specs=[pl.BlockSpec((tm, dm), lambda i: (cur(i), 0)),
                  pl.BlockSpec((1, dm), const),
                  pl.BlockSpec((PERM_TM, PERM_TM), const),
                  pl.BlockSpec((PERM_TM, PERM_TM), const),
                  resident((dm, tn), lambda i: (0, w_block))],
        out_specs=out_specs,
        out_shape=out_shape,
        scratch_shapes=[pltpu.VMEM((2, tm, dm), BF16), pltpu.VMEM((dm, tn), BF16)],
        compiler_params=pltpu.CompilerParams(dimension_semantics=("arbitrary",),
                                             vmem_limit_bytes=VMEM_LIMIT_BYTES),
        name="rmsnorm_proj_u",
    )(x2, gain.reshape(1, dm), *perms, w_in)
    return (h1, h4.reshape(t, dm), h16.reshape(t, dm)), u


def _qkv_kernel(*refs, tm, tn):
    h_refs = refs[:N_ATTN_GROUPS]
    w_ref, gain_ref, cos_ref, sin_ref, out_ref, wt_ref, hbuf_ref, sem_ref = refs[N_ATTN_GROUPS:]
    rows = pl.num_programs(1)
    part = pl.program_id(0) % 3
    step = pl.program_id(0) * rows + pl.program_id(1)
    slot = step % 2

    def tile_copy(g, row_block, into):
        return pltpu.make_async_copy(h_refs[g].at[pl.ds(row_block * tm, tm), :],
                                     hbuf_ref.at[into], sem_ref.at[into])

    def start_fetch(of_step, into):
        for g in range(N_ATTN_GROUPS):
            pl.when(of_step // (3 * rows) == g)(lambda g=g: tile_copy(g, of_step % rows, into).start())

    pl.when(step == 0)(lambda: start_fetch(step, slot))
    pl.when(step + 1 < pl.num_programs(0) * rows)(lambda: start_fetch(step + 1, 1 - slot))
    tile_copy(0, 0, slot).wait()

    @pl.when(pl.program_id(1) == 0)
    def _():
        for c in range(tn // MXU_WIDTH):
            cs = slice(c * MXU_WIDTH, (c + 1) * MXU_WIDTH)
            wt_ref[cs, :] = w_ref[:, cs].T.astype(BF16)

    body = functools.partial(_qkv_body, hbuf_ref.at[slot], gain_ref, cos_ref, sin_ref, out_ref, wt_ref,
                             tm=tm, tn=tn)
    pl.when(part < 2)(functools.partial(body, normed=True))
    pl.when(part == 2)(functools.partial(body, normed=False))


def _qkv_body(h_ref, gain_ref, cos_ref, sin_ref, out_ref, wt_ref, *, tm, tn, normed):
    def chunk(c):
        ts = slice(c * ATTN_TB, (c + 1) * ATTN_TB)
        return lax.dot_general(wt_ref[...], h_ref[ts, :], _NT, preferred_element_type=F32)

    if not normed:
        for c in range(tm // ATTN_TB):
            out_ref[c] = chunk(c).astype(out_ref.dtype)
    else:
        for c in range(tm // ATTN_TB):
            acc = chunk(c)
            for hh in range(tn // HEAD_DIM):
                hs = slice(hh * HEAD_DIM, (hh + 1) * HEAD_DIM)
                for lt in range(ATTN_TB // LANES):
                    ls = slice(lt * LANES, (lt + 1) * LANES)
                    pos = slice(c * ATTN_TB + lt * LANES, c * ATTN_TB + (lt + 1) * LANES)
                    a = acc[hs, ls]
                    ss = jnp.sum(a * a, axis=0, keepdims=True)
                    r = lax.rsqrt(ss * (1.0 / HEAD_DIM) + NORM_EPS)
                    ag = a * gain_ref[...]
                    lo, hi = ag[0:ROT_HALF], ag[ROT_HALF:ROT_DIM]
                    cos, sin = cos_ref[:, pos], sin_ref[:, pos]
                    y = jnp.concatenate([lo * cos - hi * sin, hi * cos + lo * sin, ag[ROT_DIM:]], axis=0)
                    out_ref[c, hs, ls] = (y * r).astype(out_ref.dtype)


def _project_qkv(hs, w_in, gains, cos_t, sin_t, seq):
    t, dm = hs[0].shape
    tm, tn = QKV_TM, PROJ_TN
    seq_tiles = seq // tm
    group = lambda gj: gj // 3
    part = lambda gj: gj % 3
    table = pl.BlockSpec((None, ROT_HALF, tm), lambda gj, i: (group(gj), 0, i % seq_tiles))
    return pl.pallas_call(
        functools.partial(_qkv_kernel, tm=tm, tn=tn),
        grid=(3 * N_ATTN_GROUPS, t // tm),
        in_specs=[*(pl.BlockSpec(memory_space=pl.ANY) for _ in range(N_ATTN_GROUPS)),
                  pl.BlockSpec((dm, tn), lambda gj, i: (0, part(gj) * N_ATTN_GROUPS + group(gj))),
                  pl.BlockSpec((None, HEAD_DIM, LANES), lambda gj, i: (jnp.minimum(part(gj), 1), 0, 0)),
                  table, table],
        out_specs=pl.BlockSpec((None, None, tm // ATTN_TB, tn, ATTN_TB),
                               lambda gj, i: (group(gj), part(gj), i, 0, 0)),
        out_shape=jax.ShapeDtypeStruct((N_ATTN_GROUPS, 3, t // ATTN_TB, tn, ATTN_TB), BF16),
        scratch_shapes=[pltpu.VMEM((tn, dm), BF16), pltpu.VMEM((2, tm, dm), BF16),
                        pltpu.SemaphoreType.DMA((2,))],
        compiler_params=pltpu.CompilerParams(
            dimension_semantics=("arbitrary", "arbitrary"),
            vmem_limit_bytes=VMEM_LIMIT_BYTES),
        name="proj_qkv",
    )(*hs, w_in, gains, cos_t, sin_t)


def _rope_tables_t(seq, d):
    inv_freq = ROPE_THETA ** (-jnp.arange(0, ROT_DIM, 2, dtype=F32) / ROT_DIM)
    pos = jnp.arange(seq, dtype=jnp.int32).reshape(seq // d, d).T.reshape(seq)
    ang = pos.astype(F32)[None, :] * inv_freq[:, None]
    return jnp.cos(ang), jnp.sin(ang)


def _identity_epilogue(acc, cs):
    del cs
    return acc


def _gate_epilogue(acc, cs, half_bias_ref):
    return 0.5 + 0.5 * jnp.tanh(0.5 * acc + half_bias_ref[:, cs])


def _silu_epilogue(acc, cs, half_bias_ref):
    del cs, half_bias_ref
    half = 0.5 * acc
    return half + half * jnp.tanh(half)


def _proj_kernel(*refs, epilogues, n_aux, tn):
    h_ref, w_ref = refs[0], refs[1]
    aux = refs[2:2 + n_aux]
    out_ref = refs[2 + n_aux]
    wbf_ref = refs[3 + n_aux]
    j = pl.program_id(0)

    @pl.when(pl.program_id(1) == 0)
    def _():
        wbf_ref[...] = w_ref[...].astype(BF16)

    def body(epilogue):
        for c in range(tn // MXU_WIDTH):
            cs = slice(c * MXU_WIDTH, (c + 1) * MXU_WIDTH)
            acc = jnp.dot(h_ref[...], wbf_ref[:, cs], preferred_element_type=F32)
            out_ref[:, cs] = epilogue(acc, cs, *aux).astype(out_ref.dtype)

    if len(epilogues) == 1:
        body(epilogues[0][1])
    else:
        bounds = [first for first, _ in epilogues[1:]] + [None]
        for (first, fn), last in zip(epilogues, bounds):
            cond = (j >= first) if last is None else ((j >= first) & (j < last))
            pl.when(cond)(functools.partial(body, fn))


def _project(h, w_in, *, n_blocks, w_block_of, epilogues, out_dtype, aux=(), aux_specs=(), name):
    t, d = h.shape
    tm, tn = PROJ_TM, PROJ_TN
    kern = functools.partial(_proj_kernel, epilogues=epilogues, n_aux=len(aux), tn=tn)
    return pl.pallas_call(
        kern,
        grid=(n_blocks, t // tm),
        in_specs=[pl.BlockSpec((tm, d), lambda j, i: (i, 0)),
                  pl.BlockSpec((d, tn), lambda j, i: (0, w_block_of(j))),
                  *aux_specs],
        out_specs=pl.BlockSpec((tm, tn), lambda j, i: (i, j)),
        out_shape=jax.ShapeDtypeStruct((t, n_blocks * tn), out_dtype),
        scratch_shapes=[pltpu.VMEM((d, tn), BF16)],
        compiler_params=pltpu.CompilerParams(
            dimension_semantics=("arbitrary", "arbitrary"),
            vmem_limit_bytes=VMEM_LIMIT_BYTES),
        name=name,
    )(h, w_in, *aux)


def _band_bias():
    i = np.arange(BLK)[:, None]
    j = np.arange(2 * BLK)[None, :]
    band = (j >= i) & (j <= i + BLK)
    first = band & (j >= BLK)
    return np.where(np.stack([band, first]), 0.0, -np.inf).astype(np.float32)


class _AttentionBlocks:
    def __init__(self, bias_ref, q_ref, k_ref, v_ref, o_ref, lse_ref, kprev_ref, vprev_ref, firsts):
        self.bias_ref, self.q_ref, self.k_ref, self.v_ref = bias_ref, q_ref, k_ref, v_ref
        self.o_ref, self.lse_ref, self.kprev_ref, self.vprev_ref = o_ref, lse_ref, kprev_ref, vprev_ref
        self.firsts = firsts
        self.per = ATTN_TB // BLK
        self.scores = {}

    def _block(self, ref, n, cs):
        lo = (n % self.per) * BLK
        return ref[n // self.per, cs, lo:lo + BLK]

    def _with_previous(self, ref, prev_ref, n, cs):
        prev = prev_ref[cs, :] if n == 0 else self._block(ref, n - 1, cs)
        return jnp.concatenate([prev, self._block(ref, n, cs)], axis=1)

    def score_phase(self, n):
        first = self.firsts[n]
        bias = self.bias_ref[int(first)] if isinstance(first, bool) else self.bias_ref[jnp.where(first, 1, 0)]
        for h in range(HEADS_PER_GROUP):
            cs = slice(h * HEAD_DIM, (h + 1) * HEAD_DIM)
            kk = self._with_previous(self.k_ref, self.kprev_ref, n, cs)
            s = lax.dot_general(self._block(self.q_ref, n, cs), kk, _TN, preferred_element_type=F32)
            self.scores[n, h] = s + bias

    def value_phase(self, n):
        ts = slice(n * BLK, (n + 1) * BLK)
        lane = lax.broadcasted_iota(jnp.int32, (BLK, LANES), 1)
        m_tile = jnp.zeros((BLK, LANES), F32)
        l_tile = jnp.ones((BLK, LANES), F32)
        for h in range(HEADS_PER_GROUP):
            cs = slice(h * HEAD_DIM, (h + 1) * HEAD_DIM)
            s = self.scores.pop((n, h))
            m = jnp.max(s, axis=-1, keepdims=True)
            p = jnp.exp(s - m)
            l = jnp.sum(p, axis=-1, keepdims=True)
            vv = self._with_previous(self.v_ref, self.vprev_ref, n, cs)
            o = lax.dot_general(p.astype(BF16), vv, _NT, preferred_element_type=F32)
            self.o_ref[ts, cs] = (o / l).astype(self.o_ref.dtype)
            m_tile = jnp.where(lane == h, m, m_tile)
            l_tile = jnp.where(lane == h, l, l_tile)
        self.lse_ref[ts, :] = m_tile + jnp.log(l_tile)

    def carry(self, n_blocks):
        self.kprev_ref[...] = self._block(self.k_ref, n_blocks - 1, slice(None))
        self.vprev_ref[...] = self._block(self.v_ref, n_blocks - 1, slice(None))


def _rest_attn_kernel(h_ref, w_ref, half_bias_ref, bias_ref, q_ref, k_ref, v_ref,
                      out_ref, o_ref, lse_ref, wbf_ref, kprev_ref, vprev_ref,
                      *, tn, n_gate_blocks, steps_per_group, tiles_per_sub):
    j, i = pl.program_id(0), pl.program_id(1)
    step = j * pl.num_programs(1) + i

    @pl.when(i == 0)
    def _():
        wbf_ref[...] = w_ref[...].astype(BF16)

    @pl.when(step == 0)
    def _():
        kprev_ref[...] = jnp.zeros_like(kprev_ref)
        vprev_ref[...] = jnp.zeros_like(vprev_ref)

    n_blocks = tn // MXU_WIDTH
    per = ATTN_TB // BLK
    tiles_per_step = n_blocks // per
    group = step // steps_per_group
    tps = functools.reduce(lambda acc, g: jnp.where(group == g, tiles_per_sub[g], acc),
                           range(len(tiles_per_sub)), tiles_per_sub[0])
    tile0 = (step % steps_per_group) * tiles_per_step
    firsts = [((tile0 + n // per) & (tps - 1)) == 0 if n % per == 0 else False for n in range(n_blocks)]

    def body(epilogue):
        attn = _AttentionBlocks(bias_ref, q_ref, k_ref, v_ref, o_ref, lse_ref, kprev_ref, vprev_ref, firsts)
        for c in range(n_blocks):
            cs = slice(c * MXU_WIDTH, (c + 1) * MXU_WIDTH)
            attn.score_phase(c)
            acc = jnp.dot(h_ref[...], wbf_ref[:, cs], preferred_element_type=F32)
            if c > 0:
                attn.value_phase(c - 1)
            out_ref[:, cs] = epilogue(acc, cs, half_bias_ref).astype(out_ref.dtype)
        attn.value_phase(n_blocks - 1)
        attn.carry(n_blocks)

    pl.when(j < n_gate_blocks)(functools.partial(body, _gate_epilogue))
    pl.when(j >= n_gate_blocks)(functools.partial(body, _silu_epilogue))


def _project_rest_with_attention(h, w_in, half_bias, qkv_t, seq, *, n_gate_blocks, w_block_of):
    t, dm = h.shape
    tm, tn = PROJ_TM, PROJ_TN
    n_blocks = n_gate_blocks + 2
    rows = t // tm
    n_groups, _, n_tiles = qkv_t.shape[:3]
    blocks_per_step = tn // MXU_WIDTH
    tiles_per_step = blocks_per_step * BLK // ATTN_TB
    steps_per_group = n_tiles // tiles_per_step
    assert n_groups * steps_per_group == n_blocks * rows, "attention tiles must fill the projection steps"
    tiles_per_sub = tuple(seq // d // ATTN_TB for _, d in DILATED_GROUPS)
    step = lambda j, i: j * rows + i
    tile = lambda part: pl.BlockSpec(
        (None, None, tiles_per_step, ATTN_OUT, ATTN_TB),
        lambda j, i: (step(j, i) // steps_per_group, part, step(j, i) % steps_per_group, 0, 0))
    tokens = tiles_per_step * ATTN_TB
    per_group = lambda width: pl.BlockSpec(
        (None, tokens, width), lambda j, i: (step(j, i) // steps_per_group, step(j, i) % steps_per_group, 0))
    return pl.pallas_call(
        functools.partial(_rest_attn_kernel, tn=tn, n_gate_blocks=n_gate_blocks,
                          steps_per_group=steps_per_group, tiles_per_sub=tiles_per_sub),
        grid=(n_blocks, rows),
        in_specs=[pl.BlockSpec((tm, dm), lambda j, i: (i, 0)),
                  pl.BlockSpec((dm, tn), lambda j, i: (0, w_block_of(j))),
                  pl.BlockSpec((1, tn), lambda j, i: (0, jnp.minimum(j, n_gate_blocks - 1))),
                  pl.BlockSpec((2, BLK, 2 * BLK), lambda j, i: (0, 0, 0), pipeline_mode=pl.Buffered(1)),
                  tile(0), tile(1), tile(2)],
        out_specs=[pl.BlockSpec((tm, tn), lambda j, i: (i, j)),
                   per_group(ATTN_OUT), per_group(LANES)],
        out_shape=[jax.ShapeDtypeStruct((t, n_blocks * tn), BF16),
                   jax.ShapeDtypeStruct((n_groups, t, ATTN_OUT), BF16),
                   jax.ShapeDtypeStruct((n_groups, t, LANES), F32)],
        scratch_shapes=[pltpu.VMEM((dm, tn), BF16),
                        pltpu.VMEM((ATTN_OUT, BLK), BF16), pltpu.VMEM((ATTN_OUT, BLK), BF16)],
        compiler_params=pltpu.CompilerParams(
            dimension_semantics=("arbitrary", "arbitrary"),
            vmem_limit_bytes=VMEM_LIMIT_BYTES),
        name="proj_rest_attention",
    )(h, w_in, half_bias, jnp.asarray(_band_bias()), qkv_t, qkv_t, qkv_t)


def _rows(ref):
    return jnp.concatenate([ref[r] for r in range(ref.shape[0])], axis=0)


def _unpermute_f32(q, x):
    hi = x.astype(BF16).astype(F32)
    r1 = x - hi
    mid = r1.astype(BF16).astype(F32)
    lo = r1 - mid
    packed = hi + pltpu.roll(mid, HEADS_PER_GROUP, 1) + pltpu.roll(lo, 2 * HEADS_PER_GROUP, 1)
    y = jnp.dot(q, packed.astype(BF16), preferred_element_type=F32)
    return (y + pltpu.roll(y, LANES - HEADS_PER_GROUP, 1)) + pltpu.roll(y, LANES - 2 * HEADS_PER_GROUP, 1)


def _tail_kernel(o0_ref, o1_ref, o2_ref, l0_ref, l1_ref, l2_ref, q4_ref, q16_ref, ga_ref, gp_ref,
                 za_ref, zp_ref, u_ref, uh_ref, wba32_ref, wbp32_ref, pm32_ref, ps_ref,
                 out_ref, a_ref, b_ref, wba_ref, wbp_ref, pm_ref, *, tm, seq, n_tiles):
    step = pl.program_id(0)

    @pl.when(step == 0)
    def _():
        a_ref[...] = jnp.zeros_like(a_ref)
        b_ref[...] = jnp.zeros_like(b_ref)
        wba_ref[...] = wba32_ref[...].astype(BF16)
        wbp_ref[...] = wbp32_ref[...].astype(BF16)
        pm_ref[...] = pm32_ref[...].astype(BF16)

    body = functools.partial(
        _tail_step, o0_ref, o1_ref, o2_ref, l0_ref, l1_ref, l2_ref, q4_ref, q16_ref, ga_ref, gp_ref,
        za_ref, zp_ref, u_ref, uh_ref, wba_ref, wbp_ref, pm_ref, ps_ref, out_ref,
        pos0=(jnp.minimum(step, n_tiles - 1) * tm) % seq, tm=tm)
    for parity in range(2):
        pl.when(step % 2 == parity)(
            functools.partial(body, a_ref.at[1 - parity], b_ref.at[1 - parity],
                              a_ref.at[parity], b_ref.at[parity]))


def _tail_step(o0_ref, o1_ref, o2_ref, l0_ref, l1_ref, l2_ref, q4_ref, q16_ref, ga_ref, gp_ref,
               za_ref, zp_ref, u_ref, uh_ref, wba_ref, wbp_ref, pm_ref, ps_ref, out_ref,
               a_in_ref, b_in_ref, a_ref, b_ref, *, pos0, tm):
    q4, q16 = q4_ref[...], q16_ref[...]
    o0 = o0_ref[...]
    o1 = jnp.dot(q4, _rows(o1_ref), preferred_element_type=F32)
    o2 = jnp.dot(q16, _rows(o2_ref), preferred_element_type=F32)
    l0 = l0_ref[...]
    l1 = _unpermute_f32(q4, _rows(l1_ref))
    l2 = _unpermute_f32(q16, _rows(l2_ref))

    def merge_chunk(c):
        cs = slice(c * MERGE_TN, (c + 1) * MERGE_TN)
        y_attn = jnp.dot(a_in_ref[...], wba_ref[:, cs], preferred_element_type=F32)
        y_pool = jnp.dot(b_in_ref[...], wbp_ref[:, cs], preferred_element_type=F32)
        merged = ga_ref[:, cs].astype(F32) * y_attn + gp_ref[:, cs].astype(F32) * y_pool
        out_ref[:, cs] = merged.astype(out_ref.dtype)

    halo = uh_ref[...]
    halo = jnp.where(pos0 == 0, jnp.zeros_like(halo), halo)
    pos = pos0 + lax.broadcasted_iota(jnp.int32, (tm, 1), 0)
    pooled = []
    for g, ksz in enumerate(POOL_SIZES):
        cs = slice(g * POOL_GROUP, (g + 1) * POOL_GROUP)
        ext = jnp.concatenate([halo[:, cs], u_ref[:, cs]], axis=0)
        win, shift = ext, 1
        while shift < ksz:
            win = win + pltpu.roll(win, shift, 0)
            shift *= 2
        u = ext[POOL_HALO:]
        cnt = jnp.minimum(pos + 1, ksz).astype(F32)
        dlt = win[POOL_HALO:] / cnt - u
        pooled.append(jnp.dot(dlt.astype(BF16), pm_ref[g], preferred_element_type=F32))
    pool = jnp.concatenate(pooled, axis=1) * ps_ref[...]
    b_ref[...] = (pool * zp_ref[...].astype(F32)).astype(BF16)

    for c in range(D_MODEL // MERGE_TN):
        merge_chunk(c)

    mx = jnp.maximum(jnp.maximum(l0, l1), l2)
    e0, e1, e2 = jnp.exp(l0 - mx), jnp.exp(l1 - mx), jnp.exp(l2 - mx)
    inv = 1.0 / (e0 + e1 + e2)
    w0, w1, w2 = e0 * inv, e1 * inv, e2 * inv
    for h in range(HEADS_PER_GROUP):
        cs = slice(h * HEAD_DIM, (h + 1) * HEAD_DIM)
        a = (w0[:, h:h + 1] * o0[:, cs].astype(F32)
             + w1[:, h:h + 1] * o1[:, cs]
             + w2[:, h:h + 1] * o2[:, cs])
        a_ref[:, cs] = (a * za_ref[:, cs].astype(F32)).astype(BF16)


def _residual_epilogue(acc, cs, x_ref):
    return acc + x_ref[:, cs]


def _tail(os_, lses, rest, u, wba, wbp, pm, ps, batch, seq, *, gate_block0, za_block, zp_block):
    t, dm = u.shape[0], D_MODEL
    tm = PERM_TM
    tiles = seq // tm
    n_tiles = t // tm
    halo_per_tile = tm // POOL_HALO
    resident = functools.partial(pl.BlockSpec, pipeline_mode=pl.Buffered(1))
    cur = lambda i: jnp.minimum(i, n_tiles - 1)
    prv = lambda i: jnp.maximum(i - 1, 0)
    row = lambda blk: (lambda i: (cur(i), blk))
    row_prv = lambda blk: (lambda i: (prv(i), blk))
    gate_blocks = D_MODEL // ATTN_OUT
    unperms = [jnp.asarray(_deinterleave_matrix(d).T, BF16) for _, d in DILATED_GROUPS[1:]]
    o_in, l_in, o_specs, l_specs = [], [], [], []
    for g, (_, d) in enumerate(DILATED_GROUPS):
        for arr, width, ins, specs in ((os_, ATTN_OUT, o_in, o_specs), (lses, LANES, l_in, l_specs)):
            if d == 1:
                ins.append(arr)
                specs.append(pl.BlockSpec((None, tm, width), lambda i, g=g: (g, cur(i), 0)))
            else:
                ins.append(arr.reshape(arr.shape[0], batch, d, seq // d, width))
                specs.append(pl.BlockSpec((None, None, d, tm // d, width),
                                          lambda i, g=g: (g, cur(i) // tiles, 0, cur(i) % tiles, 0)))
    in_specs = (
        o_specs + l_specs
        + [resident((tm, tm), lambda i: (0, 0)), resident((tm, tm), lambda i: (0, 0)),
           pl.BlockSpec((tm, D_MODEL), row_prv(gate_block0 // gate_blocks)),
           pl.BlockSpec((tm, D_MODEL), row_prv(gate_block0 // gate_blocks + 1)),
           pl.BlockSpec((tm, ATTN_OUT), row(za_block)),
           pl.BlockSpec((tm, POOL_WIDTH), row(zp_block)),
           pl.BlockSpec((tm, POOL_WIDTH), row(0)),
           pl.BlockSpec((POOL_HALO, POOL_WIDTH),
                        lambda i: (jnp.maximum(cur(i) * halo_per_tile - 1, 0), 0)),
           resident(wba.shape, lambda i: (0, 0)),
           resident(wbp.shape, lambda i: (0, 0)),
           resident(pm.shape, lambda i: (0, 0, 0)),
           resident(ps.shape, lambda i: (0, 0))])
    return pl.pallas_call(
        functools.partial(_tail_kernel, tm=tm, seq=seq, n_tiles=n_tiles),
        grid=(n_tiles + 1,),
        in_specs=in_specs,
        out_specs=pl.BlockSpec((tm, dm), row_prv(0)),
        out_shape=jax.ShapeDtypeStruct((t, dm), BF16),
        scratch_shapes=[pltpu.VMEM((2, tm, ATTN_OUT), BF16), pltpu.VMEM((2, tm, POOL_WIDTH), BF16),
                        pltpu.VMEM(wba.shape, BF16), pltpu.VMEM(wbp.shape, BF16), pltpu.VMEM(pm.shape, BF16)],
        compiler_params=pltpu.CompilerParams(
            dimension_semantics=("arbitrary",),
            vmem_limit_bytes=VMEM_LIMIT_BYTES),
        name="merge_pool",
    )(*o_in, *l_in, *unperms, rest, rest, rest, rest, u, u, wba, wbp, pm, ps)


def kernel(x, norm_gain, w_in, b_gates, q_norm_gain, k_norm_gain, pool_maps, pool_scale,
           w_branch_attn, w_branch_pool, w_out):
    batch, seq, dm = x.shape
    t = batch * seq
    x2 = x.reshape(t, dm)
    hs, u = _rmsnorm_and_project_u(x2, norm_gain, w_in, batch, seq, w_block=10)

    lane_rep = lambda g: jnp.broadcast_to(g.astype(F32)[:, None], (HEAD_DIM, LANES))
    gains = jnp.stack([lane_rep(q_norm_gain * HEAD_DIM ** -0.5), lane_rep(k_norm_gain)])
    tables = [_rope_tables_t(seq, d) for _, d in DILATED_GROUPS]
    qkv_t = _project_qkv(hs, w_in, gains, jnp.stack([c for c, _ in tables]),
                         jnp.stack([s for _, s in tables]), seq)

    n_gate_blocks = 2 * D_MODEL // PROJ_TN
    half_bias = (0.5 * b_gates.astype(F32)).reshape(1, 2 * D_MODEL)
    rest_w_block = lambda j: jnp.where(j < 4, j + 12, jnp.where(j == 4, 9, 11))
    rest, o_all, lse_all = _project_rest_with_attention(
        hs[0], w_in, half_bias, qkv_t, seq, n_gate_blocks=n_gate_blocks, w_block_of=rest_w_block)

    merged = _tail(o_all, lse_all, rest, u,
                   w_branch_attn, w_branch_pool, pool_maps,
                   pool_scale.astype(F32).reshape(1, POOL_WIDTH), batch, seq,
                   gate_block0=0, za_block=4, zp_block=5)
    out = _project(merged, w_out, n_blocks=dm // PROJ_TN, w_block_of=lambda j: j,
                   epilogues=((0, _residual_epilogue),), out_dtype=F32, aux=(x2,),
                   aux_specs=(pl.BlockSpec((PROJ_TM, PROJ_TN), lambda j, i: (i, j)),),
                   name="proj_out")
    return out.reshape(batch, seq, dm)
```

```python
import functools

import numpy as np
import jax
import jax.numpy as jnp
from jax import lax
from jax.experimental import pallas as pl
from jax.experimental.pallas import tpu as pltpu

D_MODEL = 2048
HEAD_DIM = 128
HEADS_PER_GROUP = 8
DILATED_GROUPS = ((128, 1), (512, 4), (2048, 16))
N_ATTN_GROUPS = len(DILATED_GROUPS)
ATTN_OUT = HEADS_PER_GROUP * HEAD_DIM
BLK = 128
ROPE_THETA = 500000.0
ROT_DIM = HEAD_DIM // 4
ROT_HALF = ROT_DIM // 2
POOL_SIZES = (2, 4, 8, 16)
POOL_WIDTH = D_MODEL // 2
POOL_GROUP = POOL_WIDTH // len(POOL_SIZES)
NORM_EPS = 1e-6

LANES = 128
MXU_WIDTH = 256
VMEM_LIMIT_BYTES = 56 * 1024 * 1024

PROJ_TM = 1024
PROJ_TN = 1024
QKV_TM = 2048
PERM_TM = 256
NORM_TM = 512
ATTN_TB = 256
MERGE_TN = 512
WEIGHT_CHUNK_ROWS = 256
POOL_HALO = 16

F32 = jnp.float32
BF16 = jnp.bfloat16

_NT = (((1,), (1,)), ((), ()))
_TN = (((0,), (0,)), ((), ()))


def _deinterleave_matrix(d):
    n = PERM_TM // d
    i = np.arange(PERM_TM)
    p = np.zeros((PERM_TM, PERM_TM), np.float32)
    p[i, (i % n) * d + i // n] = 1.0
    return p


def _norm_u_kernel(x_ref, g_ref, p4_ref, p16_ref, w_ref, h1_ref, h4_ref, h16_ref, u_ref,
                   hbuf_ref, wbf_ref):
    step = pl.program_id(0)

    @pl.when(step == 0)
    def _():
        wbf_ref[...] = w_ref[...].astype(BF16)
        hbuf_ref[...] = jnp.zeros_like(hbuf_ref)

    def body(h_in_ref, h_out_ref):
        for c in range(u_ref.shape[1] // MXU_WIDTH):
            cs = slice(c * MXU_WIDTH, (c + 1) * MXU_WIDTH)
            u_ref[:, cs] = jnp.dot(h_in_ref[...], wbf_ref[:, cs], preferred_element_type=F32)
        for sb in range(x_ref.shape[0] // PERM_TM):
            rs = slice(sb * PERM_TM, (sb + 1) * PERM_TM)
            x = x_ref[rs, :]
            ms = jnp.mean(x * x, axis=-1, keepdims=True)
            h = (x * lax.rsqrt(ms + NORM_EPS) * g_ref[...]).astype(BF16)
            h1_ref[rs, :] = h
            h_out_ref[rs, :] = h
            for p_ref, out_ref in ((p4_ref, h4_ref), (p16_ref, h16_ref)):
                d = out_ref.shape[0]
                n = PERM_TM // d
                hp = jnp.dot(p_ref[...], h, preferred_element_type=F32).astype(BF16)
                for r in range(d):
                    out_ref[r, sb * n:(sb + 1) * n, :] = hp[r * n:(r + 1) * n, :]

    for parity in range(2):
        pl.when(step % 2 == parity)(
            functools.partial(body, hbuf_ref.at[1 - parity], hbuf_ref.at[parity]))


def _rmsnorm_and_project_u(x2, gain, w_in, batch, seq, *, w_block):
    t, dm = x2.shape
    tm, tn = NORM_TM, PROJ_TN
    tiles = seq // tm
    n_tiles = t // tm
    perms = [jnp.asarray(_deinterleave_matrix(d), BF16) for _, d in DILATED_GROUPS[1:]]
    cur = lambda i: jnp.minimum(i, n_tiles - 1)
    prv = lambda i: jnp.maximum(i - 1, 0)
    out_shape = [jax.ShapeDtypeStruct((t, dm), BF16)]
    out_specs = [pl.BlockSpec((tm, dm), lambda i: (cur(i), 0))]
    for _, d in DILATED_GROUPS[1:]:
        out_shape.append(jax.ShapeDtypeStruct((batch, d, seq // d, dm), BF16))
        out_specs.append(pl.BlockSpec((None, d, tm // d, dm),
                                      lambda i: (cur(i) // tiles, 0, cur(i) % tiles, 0)))
    out_shape.append(jax.ShapeDtypeStruct((t, tn), F32))
    out_specs.append(pl.BlockSpec((tm, tn), lambda i: (prv(i), 0)))
    const = lambda i: (0, 0)
    resident = functools.partial(pl.BlockSpec, pipeline_mode=pl.Buffered(1))
    h1, h4, h16, u = pl.pallas_call(
        _norm_u_kernel,
        grid=(n_tiles + 1,),
        in_specs=[pl.BlockSpec((tm, dm), lambda i: (cur(i), 0)),
                  pl.BlockSpec((1, dm), const),
                  pl.BlockSpec((PERM_TM, PERM_TM), const),
                  pl.BlockSpec((PERM_TM, PERM_TM), const),
                  resident((dm, tn), lambda i: (0, w_block))],
        out_specs=out_specs,
        out_shape=out_shape,
        scratch_shapes=[pltpu.VMEM((2, tm, dm), BF16), pltpu.VMEM((dm, tn), BF16)],
        compiler_params=pltpu.CompilerParams(dimension_semantics=("arbitrary",),
                                             vmem_limit_bytes=VMEM_LIMIT_BYTES),
        name="rmsnorm_proj_u",
    )(x2, gain.reshape(1, dm), *perms, w_in)
    return (h1, h4.reshape(t, dm), h16.reshape(t, dm)), u


def _qkv_kernel(*refs, tm, tn):
    h_refs = refs[:N_ATTN_GROUPS]
    w_ref, gain_ref, cos_ref, sin_ref, out_ref, wt_ref, hbuf_ref, sem_ref = refs[N_ATTN_GROUPS:]
    rows = pl.num_programs(1)
    part = pl.program_id(0) % 3
    step = pl.program_id(0) * rows + pl.program_id(1)
    slot = step % 2

    def tile_copy(g, row_block, into):
        return pltpu.make_async_copy(h_refs[g].at[pl.ds(row_block * tm, tm), :],
                                     hbuf_ref.at[into], sem_ref.at[into])

    def start_fetch(of_step, into):
        for g in range(N_ATTN_GROUPS):
            pl.when(of_step // (3 * rows) == g)(lambda g=g: tile_copy(g, of_step % rows, into).start())

    pl.when(step == 0)(lambda: start_fetch(step, slot))
    pl.when(step + 1 < pl.num_programs(0) * rows)(lambda: start_fetch(step + 1, 1 - slot))
    tile_copy(0, 0, slot).wait()

    @pl.when(pl.program_id(1) == 0)
    def _():
        for c in range(tn // MXU_WIDTH):
            cs = slice(c * MXU_WIDTH, (c + 1) * MXU_WIDTH)
            wt_ref[cs, :] = w_ref[:, cs].T.astype(BF16)

    body = functools.partial(_qkv_body, hbuf_ref.at[slot], gain_ref, cos_ref, sin_ref, out_ref, wt_ref,
                             tm=tm, tn=tn)
    pl.when(part < 2)(functools.partial(body, normed=True))
    pl.when(part == 2)(functools.partial(body, normed=False))


def _qkv_body(h_ref, gain_ref, cos_ref, sin_ref, out_ref, wt_ref, *, tm, tn, normed):
    def chunk(c):
        ts = slice(c * ATTN_TB, (c + 1) * ATTN_TB)
        return lax.dot_general(wt_ref[...], h_ref[ts, :], _NT, preferred_element_type=F32)

    if not normed:
        for c in range(tm // ATTN_TB):
            out_ref[c] = chunk(c).astype(out_ref.dtype)
    else:
        for c in range(tm // ATTN_TB):
            acc = chunk(c)
            for hh in range(tn // HEAD_DIM):
                hs = slice(hh * HEAD_DIM, (hh + 1) * HEAD_DIM)
                for lt in range(ATTN_TB // LANES):
                    ls = slice(lt * LANES, (lt + 1) * LANES)
                    pos = slice(c * ATTN_TB + lt * LANES, c * ATTN_TB + (lt + 1) * LANES)
                    a = acc[hs, ls]
                    ss = jnp.sum(a * a, axis=0, keepdims=True)
                    r = lax.rsqrt(ss * (1.0 / HEAD_DIM) + NORM_EPS)
                    ag = a * gain_ref[...]
                    lo, hi = ag[0:ROT_HALF], ag[ROT_HALF:ROT_DIM]
                    cos, sin = cos_ref[:, pos], sin_ref[:, pos]
                    y = jnp.concatenate([lo * cos - hi * sin, hi * cos + lo * sin, ag[ROT_DIM:]], axis=0)
                    out_ref[c, hs, ls] = (y * r).astype(out_ref.dtype)


def _project_qkv(hs, w_in, gains, cos_t, sin_t, seq):
    t, dm = hs[0].shape
    tm, tn = QKV_TM, PROJ_TN
    seq_tiles = seq // tm
    group = lambda gj: gj // 3
    part = lambda gj: gj % 3
    table = pl.BlockSpec((None, ROT_HALF, tm), lambda gj, i: (group(gj), 0, i % seq_tiles))
    return pl.pallas_call(
        functools.partial(_qkv_kernel, tm=tm, tn=tn),
        grid=(3 * N_ATTN_GROUPS, t // tm),
        in_specs=[*(pl.BlockSpec(memory_space=pl.ANY) for _ in range(N_ATTN_GROUPS)),
                  pl.BlockSpec((dm, tn), lambda gj, i: (0, part(gj) * N_ATTN_GROUPS + group(gj))),
                  pl.BlockSpec((None, HEAD_DIM, LANES), lambda gj, i: (jnp.minimum(part(gj), 1), 0, 0)),
                  table, table],
        out_specs=pl.BlockSpec((None, None, tm // ATTN_TB, tn, ATTN_TB),
                               lambda gj, i: (group(gj), part(gj), i, 0, 0)),
        out_shape=jax.ShapeDtypeStruct((N_ATTN_GROUPS, 3, t // ATTN_TB, tn, ATTN_TB), BF16),
        scratch_shapes=[pltpu.VMEM((tn, dm), BF16), pltpu.VMEM((2, tm, dm), BF16),
                        pltpu.SemaphoreType.DMA((2,))],
        compiler_params=pltpu.CompilerParams(
            dimension_semantics=("arbitrary", "arbitrary"),
            vmem_limit_bytes=VMEM_LIMIT_BYTES),
        name="proj_qkv",
    )(*hs, w_in, gains, cos_t, sin_t)


def _rope_tables_t(seq, d):
    inv_freq = ROPE_THETA ** (-jnp.arange(0, ROT_DIM, 2, dtype=F32) / ROT_DIM)
    pos = jnp.arange(seq, dtype=jnp.int32).reshape(seq // d, d).T.reshape(seq)
    ang = pos.astype(F32)[None, :] * inv_freq[:, None]
    return jnp.cos(ang), jnp.sin(ang)


def _gate_epilogue(acc, cs, half_bias_ref):
    return 0.5 + 0.5 * jnp.tanh(0.5 * acc + half_bias_ref[:, cs])


def _silu_epilogue(acc, cs, half_bias_ref):
    del cs, half_bias_ref
    half = 0.5 * acc
    return half + half * jnp.tanh(half)


def _band_bias():
    i = np.arange(BLK)[:, None]
    j = np.arange(2 * BLK)[None, :]
    band = (j >= i) & (j <= i + BLK)
    first = band & (j >= BLK)
    return np.where(np.stack([band, first]), 0.0, -np.inf).astype(np.float32)


class _AttentionBlocks:
    def __init__(self, bias_ref, q_ref, k_ref, v_ref, o_ref, lse_ref, kprev_ref, vprev_ref, firsts):
        self.bias_ref, self.q_ref, self.k_ref, self.v_ref = bias_ref, q_ref, k_ref, v_ref
        self.o_ref, self.lse_ref, self.kprev_ref, self.vprev_ref = o_ref, lse_ref, kprev_ref, vprev_ref
        self.firsts = firsts
        self.per = ATTN_TB // BLK
        self.scores = {}

    def _block(self, ref, n, cs):
        lo = (n % self.per) * BLK
        return ref[n // self.per, cs, lo:lo + BLK]

    def _with_previous(self, ref, prev_ref, n, cs):
        prev = prev_ref[cs, :] if n == 0 else self._block(ref, n - 1, cs)
        return jnp.concatenate([prev, self._block(ref, n, cs)], axis=1)

    def score_phase(self, n):
        first = self.firsts[n]
        bias = self.bias_ref[int(first)] if isinstance(first, bool) else self.bias_ref[jnp.where(first, 1, 0)]
        for h in range(HEADS_PER_GROUP):
            cs = slice(h * HEAD_DIM, (h + 1) * HEAD_DIM)
            kk = self._with_previous(self.k_ref, self.kprev_ref, n, cs)
            s = lax.dot_general(self._block(self.q_ref, n, cs), kk, _TN, preferred_element_type=F32)
            self.scores[n, h] = s + bias

    def value_phase(self, n):
        ts = slice(n * BLK, (n + 1) * BLK)
        lane = lax.broadcasted_iota(jnp.int32, (BLK, LANES), 1)
        m_tile = jnp.zeros((BLK, LANES), F32)
        l_tile = jnp.ones((BLK, LANES), F32)
        for h in range(HEADS_PER_GROUP):
            cs = slice(h * HEAD_DIM, (h + 1) * HEAD_DIM)
            s = self.scores.pop((n, h))
            m = jnp.max(s, axis=-1, keepdims=True)
            p = jnp.exp(s - m)
            l = jnp.sum(p, axis=-1, keepdims=True)
            vv = self._with_previous(self.v_ref, self.vprev_ref, n, cs)
            o = lax.dot_general(p.astype(BF16), vv, _NT, preferred_element_type=F32)
            self.o_ref[ts, cs] = (o / l).astype(self.o_ref.dtype)
            m_tile = jnp.where(lane == h, m, m_tile)
            l_tile = jnp.where(lane == h, l, l_tile)
        self.lse_ref[ts, :] = m_tile + jnp.log(l_tile)

    def carry(self, n_blocks):
        self.kprev_ref[...] = self._block(self.k_ref, n_blocks - 1, slice(None))
        self.vprev_ref[...] = self._block(self.v_ref, n_blocks - 1, slice(None))


def _rest_attn_kernel(h_ref, w_ref, half_bias_ref, bias_ref, q_ref, k_ref, v_ref,
                      out_ref, o_ref, lse_ref, wbf_ref, kprev_ref, vprev_ref,
                      *, tn, n_gate_blocks, steps_per_group, tiles_per_sub):
    j, i = pl.program_id(0), pl.program_id(1)
    step = j * pl.num_programs(1) + i

    @pl.when(i == 0)
    def _():
        wbf_ref[...] = w_ref[...].astype(BF16)

    @pl.when(step == 0)
    def _():
        kprev_ref[...] = jnp.zeros_like(kprev_ref)
        vprev_ref[...] = jnp.zeros_like(vprev_ref)

    n_blocks = tn // MXU_WIDTH
    per = ATTN_TB // BLK
    tiles_per_step = n_blocks // per
    group = step // steps_per_group
    tps = functools.reduce(lambda acc, g: jnp.where(group == g, tiles_per_sub[g], acc),
                           range(len(tiles_per_sub)), tiles_per_sub[0])
    tile0 = (step % steps_per_group) * tiles_per_step
    firsts = [((tile0 + n // per) & (tps - 1)) == 0 if n % per == 0 else False for n in range(n_blocks)]

    def body(epilogue):
        attn = _AttentionBlocks(bias_ref, q_ref, k_ref, v_ref, o_ref, lse_ref, kprev_ref, vprev_ref, firsts)
        for c in range(n_blocks):
            cs = slice(c * MXU_WIDTH, (c + 1) * MXU_WIDTH)
            attn.score_phase(c)
            acc = jnp.dot(h_ref[...], wbf_ref[:, cs], preferred_element_type=F32)
            if c > 0:
                attn.value_phase(c - 1)
            out_ref[:, cs] = epilogue(acc, cs, half_bias_ref).astype(out_ref.dtype)
        attn.value_phase(n_blocks - 1)
        attn.carry(n_blocks)

    pl.when(j < n_gate_blocks)(functools.partial(body, _gate_epilogue))
    pl.when(j >= n_gate_blocks)(functools.partial(body, _silu_epilogue))


def _project_rest_with_attention(h, w_in, half_bias, qkv_t, seq, *, n_gate_blocks, w_block_of):
    t, dm = h.shape
    tm, tn = PROJ_TM, PROJ_TN
    n_blocks = n_gate_blocks + 2
    rows = t // tm
    n_groups, _, n_tiles = qkv_t.shape[:3]
    blocks_per_step = tn // MXU_WIDTH
    tiles_per_step = blocks_per_step * BLK // ATTN_TB
    steps_per_group = n_tiles // tiles_per_step
    assert n_groups * steps_per_group == n_blocks * rows, "attention tiles must fill the projection steps"
    tiles_per_sub = tuple(seq // d // ATTN_TB for _, d in DILATED_GROUPS)
    step = lambda j, i: j * rows + i
    tile = lambda part: pl.BlockSpec(
        (None, None, tiles_per_step, ATTN_OUT, ATTN_TB),
        lambda j, i: (step(j, i) // steps_per_group, part, step(j, i) % steps_per_group, 0, 0))
    tokens = tiles_per_step * ATTN_TB
    per_group = lambda width: pl.BlockSpec(
        (None, tokens, width), lambda j, i: (step(j, i) // steps_per_group, step(j, i) % steps_per_group, 0))
    return pl.pallas_call(
        functools.partial(_rest_attn_kernel, tn=tn, n_gate_blocks=n_gate_blocks,
                          steps_per_group=steps_per_group, tiles_per_sub=tiles_per_sub),
        grid=(n_blocks, rows),
        in_specs=[pl.BlockSpec((tm, dm), lambda j, i: (i, 0)),
                  pl.BlockSpec((dm, tn), lambda j, i: (0, w_block_of(j))),
                  pl.BlockSpec((1, tn), lambda j, i: (0, jnp.minimum(j, n_gate_blocks - 1))),
                  pl.BlockSpec((2, BLK, 2 * BLK), lambda j, i: (0, 0, 0), pipeline_mode=pl.Buffered(1)),
                  tile(0), tile(1), tile(2)],
        out_specs=[pl.BlockSpec((tm, tn), lambda j, i: (i, j)),
                   per_group(ATTN_OUT), per_group(LANES)],
        out_shape=[jax.ShapeDtypeStruct((t, n_blocks * tn), BF16),
                   jax.ShapeDtypeStruct((n_groups, t, ATTN_OUT), BF16),
                   jax.ShapeDtypeStruct((n_groups, t, LANES), F32)],
        scratch_shapes=[pltpu.VMEM((dm, tn), BF16),
                        pltpu.VMEM((ATTN_OUT, BLK), BF16), pltpu.VMEM((ATTN_OUT, BLK), BF16)],
        compiler_params=pltpu.CompilerParams(
            dimension_semantics=("arbitrary", "arbitrary"),
            vmem_limit_bytes=VMEM_LIMIT_BYTES),
        name="proj_rest_attention",
    )(h, w_in, half_bias, jnp.asarray(_band_bias()), qkv_t, qkv_t, qkv_t)


def _rows(ref):
    return jnp.concatenate([ref[r] for r in range(ref.shape[0])], axis=0)


def _unpermute_f32(q, x):
    hi = x.astype(BF16).astype(F32)
    r1 = x - hi
    mid = r1.astype(BF16).astype(F32)
    lo = r1 - mid
    packed = hi + pltpu.roll(mid, HEADS_PER_GROUP, 1) + pltpu.roll(lo, 2 * HEADS_PER_GROUP, 1)
    y = jnp.dot(q, packed.astype(BF16), preferred_element_type=F32)
    return (y + pltpu.roll(y, LANES - HEADS_PER_GROUP, 1)) + pltpu.roll(y, LANES - 2 * HEADS_PER_GROUP, 1)


def _load_weight(w_hbm_ref, w_ref, stage_ref, sem_ref):
    rows = stage_ref.shape[1]
    n = w_hbm_ref.shape[0] // rows

    def chunk_copy(k):
        return pltpu.make_async_copy(w_hbm_ref.at[pl.ds(k * rows, rows), :],
                                     stage_ref.at[k % 2], sem_ref.at[k % 2])

    chunk_copy(0).start()
    for k in range(n):
        if k + 1 < n:
            chunk_copy(k + 1).start()
        chunk_copy(k).wait()
        w_ref[k * rows:(k + 1) * rows, :] = stage_ref[k % 2].astype(BF16)


def _tail_kernel(o0_ref, o1_ref, o2_ref, l0_ref, l1_ref, l2_ref, q4_ref, q16_ref, ga_ref, gp_ref,
                 za_ref, zp_ref, u_ref, uh_ref, x_ref, wba_hbm_ref, wbp_hbm_ref, wo_hbm_ref, pm32_ref,
                 ps_ref, out_ref, a_ref, b_ref, m_ref, wba_ref, wbp_ref, wo_ref, pm_ref, stage_ref,
                 sem_ref, *, tm, seq, n_tiles):
    step = pl.program_id(0)

    @pl.when(step == 0)
    def _():
        a_ref[...] = jnp.zeros_like(a_ref)
        b_ref[...] = jnp.zeros_like(b_ref)
        m_ref[...] = jnp.zeros_like(m_ref)
        pm_ref[...] = pm32_ref[...].astype(BF16)
        for w_hbm_ref, w_ref in ((wba_hbm_ref, wba_ref), (wbp_hbm_ref, wbp_ref), (wo_hbm_ref, wo_ref)):
            _load_weight(w_hbm_ref, w_ref, stage_ref, sem_ref)

    body = functools.partial(
        _tail_step, o0_ref, o1_ref, o2_ref, l0_ref, l1_ref, l2_ref, q4_ref, q16_ref, ga_ref, gp_ref,
        za_ref, zp_ref, u_ref, uh_ref, x_ref, wba_ref, wbp_ref, wo_ref, pm_ref, ps_ref, out_ref,
        pos0=(jnp.minimum(step, n_tiles - 1) * tm) % seq, tm=tm)
    for parity in range(2):
        pl.when(step % 2 == parity)(
            functools.partial(body, a_ref.at[1 - parity], b_ref.at[1 - parity], m_ref.at[1 - parity],
                              a_ref.at[parity], b_ref.at[parity], m_ref.at[parity]))


def _tail_step(o0_ref, o1_ref, o2_ref, l0_ref, l1_ref, l2_ref, q4_ref, q16_ref, ga_ref, gp_ref,
               za_ref, zp_ref, u_ref, uh_ref, x_ref, wba_ref, wbp_ref, wo_ref, pm_ref, ps_ref, out_ref,
               a_in_ref, b_in_ref, m_in_ref, a_ref, b_ref, m_ref, *, pos0, tm):
    q4, q16 = q4_ref[...], q16_ref[...]
    o0 = o0_ref[...]
    o1 = jnp.dot(q4, _rows(o1_ref), preferred_element_type=F32)
    o2 = jnp.dot(q16, _rows(o2_ref), preferred_element_type=F32)
    l0 = l0_ref[...]
    l1 = _unpermute_f32(q4, _rows(l1_ref))
    l2 = _unpermute_f32(q16, _rows(l2_ref))

    halo = uh_ref[...]
    halo = jnp.where(pos0 == 0, jnp.zeros_like(halo), halo)
    pos = pos0 + lax.broadcasted_iota(jnp.int32, (tm, 1), 0)
    pooled = []
    for g, ksz in enumerate(POOL_SIZES):
        cs = slice(g * POOL_GROUP, (g + 1) * POOL_GROUP)
        ext = jnp.concatenate([halo[:, cs], u_ref[:, cs]], axis=0)
        win, shift = ext, 1
        while shift < ksz:
            win = win + pltpu.roll(win, shift, 0)
            shift *= 2
        u = ext[POOL_HALO:]
        cnt = jnp.minimum(pos + 1, ksz).astype(F32)
        dlt = win[POOL_HALO:] / cnt - u
        pooled.append(jnp.dot(dlt.astype(BF16), pm_ref[g], preferred_element_type=F32))
    pool = jnp.concatenate(pooled, axis=1) * ps_ref[...]
    b_ref[...] = (pool * zp_ref[...].astype(F32)).astype(BF16)

    for c in range(D_MODEL // MERGE_TN):
        cs = slice(c * MERGE_TN, (c + 1) * MERGE_TN)
        y_attn = jnp.dot(a_in_ref[...], wba_ref[:, cs], preferred_element_type=F32)
        y_pool = jnp.dot(b_in_ref[...], wbp_ref[:, cs], preferred_element_type=F32)
        merged = ga_ref[:, cs].astype(F32) * y_attn + gp_ref[:, cs].astype(F32) * y_pool
        m_ref[:, cs] = merged.astype(m_ref.dtype)

    for c in range(D_MODEL // MERGE_TN):
        cs = slice(c * MERGE_TN, (c + 1) * MERGE_TN)
        out_ref[:, cs] = x_ref[:, cs] + jnp.dot(m_in_ref[...], wo_ref[:, cs], preferred_element_type=F32)

    mx = jnp.maximum(jnp.maximum(l0, l1), l2)
    e0, e1, e2 = jnp.exp(l0 - mx), jnp.exp(l1 - mx), jnp.exp(l2 - mx)
    inv = 1.0 / (e0 + e1 + e2)
    w0, w1, w2 = e0 * inv, e1 * inv, e2 * inv
    for h in range(HEADS_PER_GROUP):
        cs = slice(h * HEAD_DIM, (h + 1) * HEAD_DIM)
        a = (w0[:, h:h + 1] * o0[:, cs].astype(F32)
             + w1[:, h:h + 1] * o1[:, cs]
             + w2[:, h:h + 1] * o2[:, cs])
        a_ref[:, cs] = (a * za_ref[:, cs].astype(F32)).astype(BF16)


def _tail(os_, lses, rest, u, x2, wba, wbp, wo, pm, ps, batch, seq, *, gate_block0, za_block, zp_block):
    t, dm = x2.shape
    tm = PERM_TM
    tiles = seq // tm
    n_tiles = t // tm
    halo_per_tile = tm // POOL_HALO
    resident = functools.partial(pl.BlockSpec, pipeline_mode=pl.Buffered(1))
    cur = lambda i: jnp.minimum(i, n_tiles - 1)
    prv = lambda i: jnp.clip(i - 1, 0, n_tiles - 1)
    prv2 = lambda i: jnp.maximum(i - 2, 0)
    row = lambda blk: (lambda i: (cur(i), blk))
    row_prv = lambda blk: (lambda i: (prv(i), blk))
    gate_blocks = D_MODEL // ATTN_OUT
    unperms = [jnp.asarray(_deinterleave_matrix(d).T, BF16) for _, d in DILATED_GROUPS[1:]]
    o_in, l_in, o_specs, l_specs = [], [], [], []
    for g, (_, d) in enumerate(DILATED_GROUPS):
        for arr, width, ins, specs in ((os_, ATTN_OUT, o_in, o_specs), (lses, LANES, l_in, l_specs)):
            if d == 1:
                ins.append(arr)
                specs.append(pl.BlockSpec((None, tm, width), lambda i, g=g: (g, cur(i), 0)))
            else:
                ins.append(arr.reshape(arr.shape[0], batch, d, seq // d, width))
                specs.append(pl.BlockSpec((None, None, d, tm // d, width),
                                          lambda i, g=g: (g, cur(i) // tiles, 0, cur(i) % tiles, 0)))
    in_specs = (
        o_specs + l_specs
        + [resident((tm, tm), lambda i: (0, 0)), resident((tm, tm), lambda i: (0, 0)),
           pl.BlockSpec((tm, D_MODEL), row_prv(gate_block0 // gate_blocks)),
           pl.BlockSpec((tm, D_MODEL), row_prv(gate_block0 // gate_blocks + 1)),
           pl.BlockSpec((tm, ATTN_OUT), row(za_block)),
           pl.BlockSpec((tm, POOL_WIDTH), row(zp_block)),
           pl.BlockSpec((tm, POOL_WIDTH), row(0)),
           pl.BlockSpec((POOL_HALO, POOL_WIDTH),
                        lambda i: (jnp.maximum(cur(i) * halo_per_tile - 1, 0), 0)),
           pl.BlockSpec((tm, dm), lambda i: (prv2(i), 0)),
           pl.BlockSpec(memory_space=pl.ANY),
           pl.BlockSpec(memory_space=pl.ANY),
           pl.BlockSpec(memory_space=pl.ANY),
           resident(pm.shape, lambda i: (0, 0, 0)),
           resident(ps.shape, lambda i: (0, 0))])
    return pl.pallas_call(
        functools.partial(_tail_kernel, tm=tm, seq=seq, n_tiles=n_tiles),
        grid=(n_tiles + 2,),
        in_specs=in_specs,
        out_specs=pl.BlockSpec((tm, dm), lambda i: (prv2(i), 0)),
        out_shape=jax.ShapeDtypeStruct((t, dm), F32),
        scratch_shapes=[pltpu.VMEM((2, tm, ATTN_OUT), BF16), pltpu.VMEM((2, tm, POOL_WIDTH), BF16),
                        pltpu.VMEM((2, tm, dm), BF16),
                        pltpu.VMEM(wba.shape, BF16), pltpu.VMEM(wbp.shape, BF16), pltpu.VMEM(wo.shape, BF16),
                        pltpu.VMEM(pm.shape, BF16),
                        pltpu.VMEM((2, WEIGHT_CHUNK_ROWS, dm), F32), pltpu.SemaphoreType.DMA((2,))],
        compiler_params=pltpu.CompilerParams(
            dimension_semantics=("arbitrary",),
            vmem_limit_bytes=VMEM_LIMIT_BYTES),
        name="merge_pool_out",
    )(*o_in, *l_in, *unperms, rest, rest, rest, rest, u, u, x2, wba, wbp, wo, pm, ps)


def kernel(x, norm_gain, w_in, b_gates, q_norm_gain, k_norm_gain, pool_maps, pool_scale,
           w_branch_attn, w_branch_pool, w_out):
    batch, seq, dm = x.shape
    t = batch * seq
    x2 = x.reshape(t, dm)
    hs, u = _rmsnorm_and_project_u(x2, norm_gain, w_in, batch, seq, w_block=10)

    lane_rep = lambda g: jnp.broadcast_to(g.astype(F32)[:, None], (HEAD_DIM, LANES))
    gains = jnp.stack([lane_rep(q_norm_gain * HEAD_DIM ** -0.5), lane_rep(k_norm_gain)])
    tables = [_rope_tables_t(seq, d) for _, d in DILATED_GROUPS]
    qkv_t = _project_qkv(hs, w_in, gains, jnp.stack([c for c, _ in tables]),
                         jnp.stack([s for _, s in tables]), seq)

    n_gate_blocks = 2 * D_MODEL // PROJ_TN
    half_bias = (0.5 * b_gates.astype(F32)).reshape(1, 2 * D_MODEL)
    rest_w_block = lambda j: jnp.where(j < 4, j + 12, jnp.where(j == 4, 9, 11))
    rest, o_all, lse_all = _project_rest_with_attention(
        hs[0], w_in, half_bias, qkv_t, seq, n_gate_blocks=n_gate_blocks, w_block_of=rest_w_block)

    out = _tail(o_all, lse_all, rest, u, x2,
                w_branch_attn, w_branch_pool, w_out, pool_maps,
                pool_scale.astype(F32).reshape(1, POOL_WIDTH), batch, seq,
                gate_block0=0, za_block=4, zp_block=5)
    return out.reshape(batch, seq, dm)
```

```python
import functools

import numpy as np
import jax
import jax.numpy as jnp
from jax import lax
from jax.experimental import pallas as pl
from jax.experimental.pallas import tpu as pltpu

D_MODEL = 2048
HEAD_DIM = 128
HEADS_PER_GROUP = 8
DILATED_GROUPS = ((128, 1), (512, 4), (2048, 16))
N_ATTN_GROUPS = len(DILATED_GROUPS)
ATTN_OUT = HEADS_PER_GROUP * HEAD_DIM
BLK = 128
ROPE_THETA = 500000.0
ROT_DIM = HEAD_DIM // 4
ROT_HALF = ROT_DIM // 2
POOL_SIZES = (2, 4, 8, 16)
POOL_WIDTH = D_MODEL // 2
POOL_GROUP = POOL_WIDTH // len(POOL_SIZES)
NORM_EPS = 1e-6

LANES = 128
MXU_WIDTH = 256
VMEM_LIMIT_BYTES = 56 * 1024 * 1024

PROJ_TM = 1024
PROJ_TN = 1024
QKV_TM = 2048
PERM_TM = 256
NORM_TM = 512
ATTN_TB = 256
MERGE_TN = 512
POOL_HALO = 16

F32 = jnp.float32
BF16 = jnp.bfloat16

_NT = (((1,), (1,)), ((), ()))
_TN = (((0,), (0,)), ((), ()))


def _deinterleave_matrix(d):
    n = PERM_TM // d
    i = np.arange(PERM_TM)
    p = np.zeros((PERM_TM, PERM_TM), np.float32)
    p[i, (i % n) * d + i // n] = 1.0
    return p


def _norm_u_kernel(x_ref, g_ref, p4_ref, p16_ref, w_ref, h1_ref, h4_ref, h16_ref, u_ref,
                   hbuf_ref, wbf_ref):
    step = pl.program_id(0)

    @pl.when(step == 0)
    def _():
        wbf_ref[...] = w_ref[...].astype(BF16)
        hbuf_ref[...] = jnp.zeros_like(hbuf_ref)

    def body(h_in_ref, h_out_ref):
        for c in range(u_ref.shape[1] // MXU_WIDTH):
            cs = slice(c * MXU_WIDTH, (c + 1) * MXU_WIDTH)
            u_ref[:, cs] = jnp.dot(h_in_ref[...], wbf_ref[:, cs], preferred_element_type=F32)
        for sb in range(x_ref.shape[0] // PERM_TM):
            rs = slice(sb * PERM_TM, (sb + 1) * PERM_TM)
            x = x_ref[rs, :]
            ms = jnp.mean(x * x, axis=-1, keepdims=True)
            h = (x * lax.rsqrt(ms + NORM_EPS) * g_ref[...]).astype(BF16)
            h1_ref[rs, :] = h
            h_out_ref[rs, :] = h
            for p_ref, out_ref in ((p4_ref, h4_ref), (p16_ref, h16_ref)):
                d = out_ref.shape[0]
                n = PERM_TM // d
                hp = jnp.dot(p_ref[...], h, preferred_element_type=F32).astype(BF16)
                for r in range(d):
                    out_ref[r, sb * n:(sb + 1) * n, :] = hp[r * n:(r + 1) * n, :]

    for parity in range(2):
        pl.when(step % 2 == parity)(
            functools.partial(body, hbuf_ref.at[1 - parity], hbuf_ref.at[parity]))


def _rmsnorm_and_project_u(x2, gain, w_in, batch, seq, *, w_block):
    t, dm = x2.shape
    tm, tn = NORM_TM, PROJ_TN
    tiles = seq // tm
    n_tiles = t // tm
    perms = [jnp.asarray(_deinterleave_matrix(d), BF16) for _, d in DILATED_GROUPS[1:]]
    cur = lambda i: jnp.minimum(i, n_tiles - 1)
    prv = lambda i: jnp.maximum(i - 1, 0)
    out_shape = [jax.ShapeDtypeStruct((t, dm), BF16)]
    out_specs = [pl.BlockSpec((tm, dm), lambda i: (cur(i), 0))]
    for _, d in DILATED_GROUPS[1:]:
        out_shape.append(jax.ShapeDtypeStruct((batch, d, seq // d, dm), BF16))
        out_specs.append(pl.BlockSpec((None, d, tm // d, dm),
                                      lambda i: (cur(i) // tiles, 0, cur(i) % tiles, 0)))
    out_shape.append(jax.ShapeDtypeStruct((t, tn), F32))
    out_specs.append(pl.BlockSpec((tm, tn), lambda i: (prv(i), 0)))
    const = lambda i: (0, 0)
    resident = functools.partial(pl.BlockSpec, pipeline_mode=pl.Buffered(1))
    h1, h4, h16, u = pl.pallas_call(
        _norm_u_kernel,
        grid=(n_tiles + 1,),
        in_specs=[pl.BlockSpec((tm, dm), lambda i: (cur(i), 0)),
                  pl.BlockSpec((1, dm), const),
                  pl.BlockSpec((PERM_TM, PERM_TM), const),
                  pl.BlockSpec((PERM_TM, PERM_TM), const),
                  resident((dm, tn), lambda i: (0, w_block))],
        out_specs=out_specs,
        out_shape=out_shape,
        scratch_shapes=[pltpu.VMEM((2, tm, dm), BF16), pltpu.VMEM((dm, tn), BF16)],
        compiler_params=pltpu.CompilerParams(dimension_semantics=("arbitrary",),
                                             vmem_limit_bytes=VMEM_LIMIT_BYTES),
        name="rmsnorm_proj_u",
    )(x2, gain.reshape(1, dm), *perms, w_in)
    return (h1, h4.reshape(t, dm), h16.reshape(t, dm)), u


def _qkv_kernel(*refs, tm, tn):
    h_refs = refs[:N_ATTN_GROUPS]
    w_ref, gain_ref, cos_ref, sin_ref, out_ref, wt_ref, hbuf_ref, sem_ref = refs[N_ATTN_GROUPS:]
    rows = pl.num_programs(1)
    part = pl.program_id(0) % 3
    step = pl.program_id(0) * rows + pl.program_id(1)
    slot = step % 2

    def tile_copy(g, row_block, into):
        return pltpu.make_async_copy(h_refs[g].at[pl.ds(row_block * tm, tm), :],
                                     hbuf_ref.at[into], sem_ref.at[into])

    def start_fetch(of_step, into):
        for g in range(N_ATTN_GROUPS):
            pl.when(of_step // (3 * rows) == g)(lambda g=g: tile_copy(g, of_step % rows, into).start())

    pl.when(step == 0)(lambda: start_fetch(step, slot))
    pl.when(step + 1 < pl.num_programs(0) * rows)(lambda: start_fetch(step + 1, 1 - slot))
    tile_copy(0, 0, slot).wait()

    @pl.when(pl.program_id(1) == 0)
    def _():
        for c in range(tn // MXU_WIDTH):
            cs = slice(c * MXU_WIDTH, (c + 1) * MXU_WIDTH)
            wt_ref[cs, :] = w_ref[:, cs].T.astype(BF16)

    body = functools.partial(_qkv_body, hbuf_ref.at[slot], gain_ref, cos_ref, sin_ref, out_ref, wt_ref,
                             tm=tm, tn=tn)
    pl.when(part < 2)(functools.partial(body, normed=True))
    pl.when(part == 2)(functools.partial(body, normed=False))


def _qkv_body(h_ref, gain_ref, cos_ref, sin_ref, out_ref, wt_ref, *, tm, tn, normed):
    def chunk(c):
        ts = slice(c * ATTN_TB, (c + 1) * ATTN_TB)
        return lax.dot_general(wt_ref[...], h_ref[ts, :], _NT, preferred_element_type=F32)

    if not normed:
        for c in range(tm // ATTN_TB):
            out_ref[c] = chunk(c).astype(out_ref.dtype)
    else:
        for c in range(tm // ATTN_TB):
            acc = chunk(c)
            for hh in range(tn // HEAD_DIM):
                hs = slice(hh * HEAD_DIM, (hh + 1) * HEAD_DIM)
                for lt in range(ATTN_TB // LANES):
                    ls = slice(lt * LANES, (lt + 1) * LANES)
                    pos = slice(c * ATTN_TB + lt * LANES, c * ATTN_TB + (lt + 1) * LANES)
                    a = acc[hs, ls]
                    ss = jnp.sum(a * a, axis=0, keepdims=True)
                    r = lax.rsqrt(ss * (1.0 / HEAD_DIM) + NORM_EPS)
                    ag = a * gain_ref[...]
                    lo, hi = ag[0:ROT_HALF], ag[ROT_HALF:ROT_DIM]
                    cos, sin = cos_ref[:, pos], sin_ref[:, pos]
                    y = jnp.concatenate([lo * cos - hi * sin, hi * cos + lo * sin, ag[ROT_DIM:]], axis=0)
                    out_ref[c, hs, ls] = (y * r).astype(out_ref.dtype)


def _project_qkv(hs, w_in, gains, cos_t, sin_t, seq):
    t, dm = hs[0].shape
    tm, tn = QKV_TM, PROJ_TN
    seq_tiles = seq // tm
    group = lambda gj: gj // 3
    part = lambda gj: gj % 3
    table = pl.BlockSpec((None, ROT_HALF, tm), lambda gj, i: (group(gj), 0, i % seq_tiles))
    return pl.pallas_call(
        functools.partial(_qkv_kernel, tm=tm, tn=tn),
        grid=(3 * N_ATTN_GROUPS, t // tm),
        in_specs=[*(pl.BlockSpec(memory_space=pl.ANY) for _ in range(N_ATTN_GROUPS)),
                  pl.BlockSpec((dm, tn), lambda gj, i: (0, part(gj) * N_ATTN_GROUPS + group(gj))),
                  pl.BlockSpec((None, HEAD_DIM, LANES), lambda gj, i: (jnp.minimum(part(gj), 1), 0, 0)),
                  table, table],
        out_specs=pl.BlockSpec((None, None, tm // ATTN_TB, tn, ATTN_TB),
                               lambda gj, i: (group(gj), part(gj), i, 0, 0)),
        out_shape=jax.ShapeDtypeStruct((N_ATTN_GROUPS, 3, t // ATTN_TB, tn, ATTN_TB), BF16),
        scratch_shapes=[pltpu.VMEM((tn, dm), BF16), pltpu.VMEM((2, tm, dm), BF16),
                        pltpu.SemaphoreType.DMA((2,))],
        compiler_params=pltpu.CompilerParams(
            dimension_semantics=("arbitrary", "arbitrary"),
            vmem_limit_bytes=VMEM_LIMIT_BYTES),
        name="proj_qkv",
    )(*hs, w_in, gains, cos_t, sin_t)


def _rope_tables_t(seq, d):
    inv_freq = ROPE_THETA ** (-jnp.arange(0, ROT_DIM, 2, dtype=F32) / ROT_DIM)
    pos = jnp.arange(seq, dtype=jnp.int32).reshape(seq // d, d).T.reshape(seq)
    ang = pos.astype(F32)[None, :] * inv_freq[:, None]
    return jnp.cos(ang), jnp.sin(ang)


def _gate_epilogue(acc, cs, half_bias_ref):
    return 0.5 + 0.5 * jnp.tanh(0.5 * acc + half_bias_ref[:, cs])


def _silu_epilogue(acc, cs, half_bias_ref):
    del cs, half_bias_ref
    half = 0.5 * acc
    return half + half * jnp.tanh(half)


def _proj_kernel(*refs, epilogues, n_aux, tn):
    h_ref, w_ref = refs[0], refs[1]
    aux = refs[2:2 + n_aux]
    out_ref = refs[2 + n_aux]
    wbf_ref = refs[3 + n_aux]
    j = pl.program_id(0)

    @pl.when(pl.program_id(1) == 0)
    def _():
        wbf_ref[...] = w_ref[...].astype(BF16)

    def body(epilogue):
        for c in range(tn // MXU_WIDTH):
            cs = slice(c * MXU_WIDTH, (c + 1) * MXU_WIDTH)
            acc = jnp.dot(h_ref[...], wbf_ref[:, cs], preferred_element_type=F32)
            out_ref[:, cs] = epilogue(acc, cs, *aux).astype(out_ref.dtype)

    if len(epilogues) == 1:
        body(epilogues[0][1])
    else:
        bounds = [first for first, _ in epilogues[1:]] + [None]
        for (first, fn), last in zip(epilogues, bounds):
            cond = (j >= first) if last is None else ((j >= first) & (j < last))
            pl.when(cond)(functools.partial(body, fn))


def _project(h, w_in, *, n_blocks, w_block_of, epilogues, out_dtype, aux=(), aux_specs=(), name):
    t, d = h.shape
    tm, tn = PROJ_TM, PROJ_TN
    kern = functools.partial(_proj_kernel, epilogues=epilogues, n_aux=len(aux), tn=tn)
    return pl.pallas_call(
        kern,
        grid=(n_blocks, t // tm),
        in_specs=[pl.BlockSpec((tm, d), lambda j, i: (i, 0)),
                  pl.BlockSpec((d, tn), lambda j, i: (0, w_block_of(j))),
                  *aux_specs],
        out_specs=pl.BlockSpec((tm, tn), lambda j, i: (i, j)),
        out_shape=jax.ShapeDtypeStruct((t, n_blocks * tn), out_dtype),
        scratch_shapes=[pltpu.VMEM((d, tn), BF16)],
        compiler_params=pltpu.CompilerParams(
            dimension_semantics=("arbitrary", "arbitrary"),
            vmem_limit_bytes=VMEM_LIMIT_BYTES),
        name=name,
    )(h, w_in, *aux)


def _band_bias():
    i = np.arange(BLK)[:, None]
    j = np.arange(2 * BLK)[None, :]
    band = (j >= i) & (j <= i + BLK)
    first = band & (j >= BLK)
    return np.where(np.stack([band, first]), 0.0, -np.inf).astype(np.float32)


class _AttentionBlocks:
    def __init__(self, bias_ref, q_ref, k_ref, v_ref, o_ref, lse_ref, kprev_ref, vprev_ref, firsts):
        self.bias_ref, self.q_ref, self.k_ref, self.v_ref = bias_ref, q_ref, k_ref, v_ref
        self.o_ref, self.lse_ref, self.kprev_ref, self.vprev_ref = o_ref, lse_ref, kprev_ref, vprev_ref
        self.firsts = firsts
        self.per = ATTN_TB // BLK
        self.scores = {}

    def _block(self, ref, n, cs):
        lo = (n % self.per) * BLK
        return ref[n // self.per, cs, lo:lo + BLK]

    def _with_previous(self, ref, prev_ref, n, cs):
        prev = prev_ref[cs, :] if n == 0 else self._block(ref, n - 1, cs)
        return jnp.concatenate([prev, self._block(ref, n, cs)], axis=1)

    def score_phase(self, n):
        first = self.firsts[n]
        bias = self.bias_ref[int(first)] if isinstance(first, bool) else self.bias_ref[jnp.where(first, 1, 0)]
        for h in range(HEADS_PER_GROUP):
            cs = slice(h * HEAD_DIM, (h + 1) * HEAD_DIM)
            kk = self._with_previous(self.k_ref, self.kprev_ref, n, cs)
            s = lax.dot_general(self._block(self.q_ref, n, cs), kk, _TN, preferred_element_type=F32)
            self.scores[n, h] = s + bias

    def value_phase(self, n):
        ts = slice(n * BLK, (n + 1) * BLK)
        lane = lax.broadcasted_iota(jnp.int32, (BLK, LANES), 1)
        m_tile = jnp.zeros((BLK, LANES), F32)
        l_tile = jnp.ones((BLK, LANES), F32)
        for h in range(HEADS_PER_GROUP):
            cs = slice(h * HEAD_DIM, (h + 1) * HEAD_DIM)
            s = self.scores.pop((n, h))
            m = jnp.max(s, axis=-1, keepdims=True)
            p = jnp.exp(s - m)
            l = jnp.sum(p, axis=-1, keepdims=True)
            vv = self._with_previous(self.v_ref, self.vprev_ref, n, cs)
            o = lax.dot_general(p.astype(BF16), vv, _NT, preferred_element_type=F32)
            self.o_ref[ts, cs] = (o / l).astype(self.o_ref.dtype)
            m_tile = jnp.where(lane == h, m, m_tile)
            l_tile = jnp.where(lane == h, l, l_tile)
        self.lse_ref[ts, :] = m_tile + jnp.log(l_tile)

    def carry(self, n_blocks):
        self.kprev_ref[...] = self._block(self.k_ref, n_blocks - 1, slice(None))
        self.vprev_ref[...] = self._block(self.v_ref, n_blocks - 1, slice(None))


def _rest_attn_kernel(h_ref, w_ref, half_bias_ref, bias_ref, q_ref, k_ref, v_ref,
                      out_ref, o_ref, lse_ref, wbf_ref, kprev_ref, vprev_ref,
                      *, tn, n_gate_blocks, steps_per_group, tiles_per_sub):
    j, i = pl.program_id(0), pl.program_id(1)
    step = j * pl.num_programs(1) + i

    @pl.when(i == 0)
    def _():
        wbf_ref[...] = w_ref[...].astype(BF16)

    @pl.when(step == 0)
    def _():
        kprev_ref[...] = jnp.zeros_like(kprev_ref)
        vprev_ref[...] = jnp.zeros_like(vprev_ref)

    n_blocks = tn // MXU_WIDTH
    per = ATTN_TB // BLK
    tiles_per_step = n_blocks // per
    group = step // steps_per_group
    tps = functools.reduce(lambda acc, g: jnp.where(group == g, tiles_per_sub[g], acc),
                           range(len(tiles_per_sub)), tiles_per_sub[0])
    tile0 = (step % steps_per_group) * tiles_per_step
    firsts = [((tile0 + n // per) & (tps - 1)) == 0 if n % per == 0 else False for n in range(n_blocks)]

    def body(epilogue):
        attn = _AttentionBlocks(bias_ref, q_ref, k_ref, v_ref, o_ref, lse_ref, kprev_ref, vprev_ref, firsts)
        for c in range(n_blocks):
            cs = slice(c * MXU_WIDTH, (c + 1) * MXU_WIDTH)
            attn.score_phase(c)
            acc = jnp.dot(h_ref[...], wbf_ref[:, cs], preferred_element_type=F32)
            if c > 0:
                attn.value_phase(c - 1)
            out_ref[:, cs] = epilogue(acc, cs, half_bias_ref).astype(out_ref.dtype)
        attn.value_phase(n_blocks - 1)
        attn.carry(n_blocks)

    pl.when(j < n_gate_blocks)(functools.partial(body, _gate_epilogue))
    pl.when(j >= n_gate_blocks)(functools.partial(body, _silu_epilogue))


def _project_rest_with_attention(h, w_in, half_bias, qkv_t, seq, *, n_gate_blocks, w_block_of):
    t, dm = h.shape
    tm, tn = PROJ_TM, PROJ_TN
    n_blocks = n_gate_blocks + 2
    rows = t // tm
    n_groups, _, n_tiles = qkv_t.shape[:3]
    blocks_per_step = tn // MXU_WIDTH
    tiles_per_step = blocks_per_step * BLK // ATTN_TB
    steps_per_group = n_tiles // tiles_per_step
    assert n_groups * steps_per_group == n_blocks * rows, "attention tiles must fill the projection steps"
    tiles_per_sub = tuple(seq // d // ATTN_TB for _, d in DILATED_GROUPS)
    step = lambda j, i: j * rows + i
    tile = lambda part: pl.BlockSpec(
        (None, None, tiles_per_step, ATTN_OUT, ATTN_TB),
        lambda j, i: (step(j, i) // steps_per_group, part, step(j, i) % steps_per_group, 0, 0))
    tokens = tiles_per_step * ATTN_TB
    per_group = lambda width: pl.BlockSpec(
        (None, tokens, width), lambda j, i: (step(j, i) // steps_per_group, step(j, i) % steps_per_group, 0))
    return pl.pallas_call(
        functools.partial(_rest_attn_kernel, tn=tn, n_gate_blocks=n_gate_blocks,
                          steps_per_group=steps_per_group, tiles_per_sub=tiles_per_sub),
        grid=(n_blocks, rows),
        in_specs=[pl.BlockSpec((tm, dm), lambda j, i: (i, 0)),
                  pl.BlockSpec((dm, tn), lambda j, i: (0, w_block_of(j))),
                  pl.BlockSpec((1, tn), lambda j, i: (0, jnp.minimum(j, n_gate_blocks - 1))),
                  pl.BlockSpec((2, BLK, 2 * BLK), lambda j, i: (0, 0, 0), pipeline_mode=pl.Buffered(1)),
                  tile(0), tile(1), tile(2)],
        out_specs=[pl.BlockSpec((tm, tn), lambda j, i: (i, j)),
                   per_group(ATTN_OUT), per_group(LANES)],
        out_shape=[jax.ShapeDtypeStruct((t, n_blocks * tn), BF16),
                   jax.ShapeDtypeStruct((n_groups, t, ATTN_OUT), BF16),
                   jax.ShapeDtypeStruct((n_groups, t, LANES), F32)],
        scratch_shapes=[pltpu.VMEM((dm, tn), BF16),
                        pltpu.VMEM((ATTN_OUT, BLK), BF16), pltpu.VMEM((ATTN_OUT, BLK), BF16)],
        compiler_params=pltpu.CompilerParams(
            dimension_semantics=("arbitrary", "arbitrary"),
            vmem_limit_bytes=VMEM_LIMIT_BYTES),
        name="proj_rest_attention",
    )(h, w_in, half_bias, jnp.asarray(_band_bias()), qkv_t, qkv_t, qkv_t)


def _rows(ref):
    return jnp.concatenate([ref[r] for r in range(ref.shape[0])], axis=0)


def _unpermute_f32(q, x):
    hi = x.astype(BF16).astype(F32)
    r1 = x - hi
    mid = r1.astype(BF16).astype(F32)
    lo = r1 - mid
    packed = hi + pltpu.roll(mid, HEADS_PER_GROUP, 1) + pltpu.roll(lo, 2 * HEADS_PER_GROUP, 1)
    y = jnp.dot(q, packed.astype(BF16), preferred_element_type=F32)
    return (y + pltpu.roll(y, LANES - HEADS_PER_GROUP, 1)) + pltpu.roll(y, LANES - 2 * HEADS_PER_GROUP, 1)


def _tail_kernel(o0_ref, o1_ref, o2_ref, l0_ref, l1_ref, l2_ref, q4_ref, q16_ref, ga_ref, gp_ref,
                 za_ref, zp_ref, u_ref, uh_ref, wba32_ref, wbp32_ref, pm32_ref, ps_ref,
                 out_ref, a_ref, b_ref, wba_ref, wbp_ref, pm_ref, *, tm, seq, n_tiles):
    step = pl.program_id(0)

    @pl.when(step == 0)
    def _():
        a_ref[...] = jnp.zeros_like(a_ref)
        b_ref[...] = jnp.zeros_like(b_ref)
        wba_ref[...] = wba32_ref[...].astype(BF16)
        wbp_ref[...] = wbp32_ref[...].astype(BF16)
        pm_ref[...] = pm32_ref[...].astype(BF16)

    body = functools.partial(
        _tail_step, o0_ref, o1_ref, o2_ref, l0_ref, l1_ref, l2_ref, q4_ref, q16_ref, ga_ref, gp_ref,
        za_ref, zp_ref, u_ref, uh_ref, wba_ref, wbp_ref, pm_ref, ps_ref, out_ref,
        pos0=(jnp.minimum(step, n_tiles - 1) * tm) % seq, tm=tm)
    for parity in range(2):
        pl.when(step % 2 == parity)(
            functools.partial(body, a_ref.at[1 - parity], b_ref.at[1 - parity],
                              a_ref.at[parity], b_ref.at[parity]))


def _tail_step(o0_ref, o1_ref, o2_ref, l0_ref, l1_ref, l2_ref, q4_ref, q16_ref, ga_ref, gp_ref,
               za_ref, zp_ref, u_ref, uh_ref, wba_ref, wbp_ref, pm_ref, ps_ref, out_ref,
               a_in_ref, b_in_ref, a_ref, b_ref, *, pos0, tm):
    q4, q16 = q4_ref[...], q16_ref[...]
    o0 = o0_ref[...]
    o1 = jnp.dot(q4, _rows(o1_ref), preferred_element_type=F32)
    o2 = jnp.dot(q16, _rows(o2_ref), preferred_element_type=F32)
    l0 = l0_ref[...]
    l1 = _unpermute_f32(q4, _rows(l1_ref))
    l2 = _unpermute_f32(q16, _rows(l2_ref))

    def merge_chunk(c):
        cs = slice(c * MERGE_TN, (c + 1) * MERGE_TN)
        y_attn = jnp.dot(a_in_ref[...], wba_ref[:, cs], preferred_element_type=F32)
        y_pool = jnp.dot(b_in_ref[...], wbp_ref[:, cs], preferred_element_type=F32)
        merged = ga_ref[:, cs].astype(F32) * y_attn + gp_ref[:, cs].astype(F32) * y_pool
        out_ref[:, cs] = merged.astype(out_ref.dtype)

    halo = uh_ref[...]
    halo = jnp.where(pos0 == 0, jnp.zeros_like(halo), halo)
    pos = pos0 + lax.broadcasted_iota(jnp.int32, (tm, 1), 0)
    pooled = []
    for g, ksz in enumerate(POOL_SIZES):
        cs = slice(g * POOL_GROUP, (g + 1) * POOL_GROUP)
        ext = jnp.concatenate([halo[:, cs], u_ref[:, cs]], axis=0)
        win, shift = ext, 1
        while shift < ksz:
            win = win + pltpu.roll(win, shift, 0)
            shift *= 2
        u = ext[POOL_HALO:]
        cnt = jnp.minimum(pos + 1, ksz).astype(F32)
        dlt = win[POOL_HALO:] / cnt - u
        pooled.append(jnp.dot(dlt.astype(BF16), pm_ref[g], preferred_element_type=F32))
    pool = jnp.concatenate(pooled, axis=1) * ps_ref[...]
    b_ref[...] = (pool * zp_ref[...].astype(F32)).astype(BF16)

    for c in range(D_MODEL // MERGE_TN):
        merge_chunk(c)

    mx = jnp.maximum(jnp.maximum(l0, l1), l2)
    e0, e1, e2 = jnp.exp(l0 - mx), jnp.exp(l1 - mx), jnp.exp(l2 - mx)
    inv = 1.0 / (e0 + e1 + e2)
    w0, w1, w2 = e0 * inv, e1 * inv, e2 * inv
    for h in range(HEADS_PER_GROUP):
        cs = slice(h * HEAD_DIM, (h + 1) * HEAD_DIM)
        a = (w0[:, h:h + 1] * o0[:, cs].astype(F32)
             + w1[:, h:h + 1] * o1[:, cs]
             + w2[:, h:h + 1] * o2[:, cs])
        a_ref[:, cs] = (a * za_ref[:, cs].astype(F32)).astype(BF16)


def _residual_epilogue(acc, cs, x_ref):
    return acc + x_ref[:, cs]


def _tail(os_, lses, rest, u, wba, wbp, pm, ps, batch, seq, *, gate_block0, za_block, zp_block):
    t, dm = u.shape[0], D_MODEL
    tm = PERM_TM
    tiles = seq // tm
    n_tiles = t // tm
    halo_per_tile = tm // POOL_HALO
    resident = functools.partial(pl.BlockSpec, pipeline_mode=pl.Buffered(1))
    cur = lambda i: jnp.minimum(i, n_tiles - 1)
    prv = lambda i: jnp.maximum(i - 1, 0)
    row = lambda blk: (lambda i: (cur(i), blk))
    row_prv = lambda blk: (lambda i: (prv(i), blk))
    gate_blocks = D_MODEL // ATTN_OUT
    unperms = [jnp.asarray(_deinterleave_matrix(d).T, BF16) for _, d in DILATED_GROUPS[1:]]
    o_in, l_in, o_specs, l_specs = [], [], [], []
    for g, (_, d) in enumerate(DILATED_GROUPS):
        for arr, width, ins, specs in ((os_, ATTN_OUT, o_in, o_specs), (lses, LANES, l_in, l_specs)):
            if d == 1:
                ins.append(arr)
                specs.append(pl.BlockSpec((None, tm, width), lambda i, g=g: (g, cur(i), 0)))
            else:
                ins.append(arr.reshape(arr.shape[0], batch, d, seq // d, width))
                specs.append(pl.BlockSpec((None, None, d, tm // d, width),
                                          lambda i, g=g: (g, cur(i) // tiles, 0, cur(i) % tiles, 0)))
    in_specs = (
        o_specs + l_specs
        + [resident((tm, tm), lambda i: (0, 0)), resident((tm, tm), lambda i: (0, 0)),
           pl.BlockSpec((tm, D_MODEL), row_prv(gate_block0 // gate_blocks)),
           pl.BlockSpec((tm, D_MODEL), row_prv(gate_block0 // gate_blocks + 1)),
           pl.BlockSpec((tm, ATTN_OUT), row(za_block)),
           pl.BlockSpec((tm, POOL_WIDTH), row(zp_block)),
           pl.BlockSpec((tm, POOL_WIDTH), row(0)),
           pl.BlockSpec((POOL_HALO, POOL_WIDTH),
                        lambda i: (jnp.maximum(cur(i) * halo_per_tile - 1, 0), 0)),
           resident(wba.shape, lambda i: (0, 0)),
           resident(wbp.shape, lambda i: (0, 0)),
           resident(pm.shape, lambda i: (0, 0, 0)),
           resident(ps.shape, lambda i: (0, 0))])
    return pl.pallas_call(
        functools.partial(_tail_kernel, tm=tm, seq=seq, n_tiles=n_tiles),
        grid=(n_tiles + 1,),
        in_specs=in_specs,
        out_specs=pl.BlockSpec((tm, dm), row_prv(0)),
        out_shape=jax.ShapeDtypeStruct((t, dm), BF16),
        scratch_shapes=[pltpu.VMEM((2, tm, ATTN_OUT), BF16), pltpu.VMEM((2, tm, POOL_WIDTH), BF16),
                        pltpu.VMEM(wba.shape, BF16), pltpu.VMEM(wbp.shape, BF16), pltpu.VMEM(pm.shape, BF16)],
        compiler_params=pltpu.CompilerParams(
            dimension_semantics=("arbitrary",),
            vmem_limit_bytes=VMEM_LIMIT_BYTES),
        name="merge_pool",
    )(*o_in, *l_in, *unperms, rest, rest, rest, rest, u, u, wba, wbp, pm, ps)


def kernel(x, norm_gain, w_in, b_gates, q_norm_gain, k_norm_gain, pool_maps, pool_scale,
           w_branch_attn, w_branch_pool, w_out):
    batch, seq, dm = x.shape
    t = batch * seq
    x2 = x.reshape(t, dm)
    hs, u = _rmsnorm_and_project_u(x2, norm_gain, w_in, batch, seq, w_block=10)

    lane_rep = lambda g: jnp.broadcast_to(g.astype(F32)[:, None], (HEAD_DIM, LANES))
    gains = jnp.stack([lane_rep(q_norm_gain * HEAD_DIM ** -0.5), lane_rep(k_norm_gain)])
    tables = [_rope_tables_t(seq, d) for _, d in DILATED_GROUPS]
    qkv_t = _project_qkv(hs, w_in, gains, jnp.stack([c for c, _ in tables]),
                         jnp.stack([s for _, s in tables]), seq)

    n_gate_blocks = 2 * D_MODEL // PROJ_TN
    half_bias = (0.5 * b_gates.astype(F32)).reshape(1, 2 * D_MODEL)
    rest_w_block = lambda j: jnp.where(j < 4, j + 12, jnp.where(j == 4, 9, 11))
    rest, o_all, lse_all = _project_rest_with_attention(
        hs[0], w_in, half_bias, qkv_t, seq, n_gate_blocks=n_gate_blocks, w_block_of=rest_w_block)

    merged = _tail(o_all, lse_all, rest, u,
                   w_branch_attn, w_branch_pool, pool_maps,
                   pool_scale.astype(F32).reshape(1, POOL_WIDTH), batch, seq,
                   gate_block0=0, za_block=4, zp_block=5)
    out = _project(merged, w_out, n_blocks=dm // PROJ_TN, w_block_of=lambda j: j,
                   epilogues=((0, _residual_epilogue),), out_dtype=F32, aux=(x2,),
                   aux_specs=(pl.BlockSpec((PROJ_TM, PROJ_TN), lambda j, i: (i, j)),),
                   name="proj_out")
    return out.reshape(batch, seq, dm)
```

```python
import functools

import numpy as np
import jax
import jax.numpy as jnp
from jax import lax
from jax.experimental import pallas as pl
from jax.experimental.pallas import tpu as pltpu

D_MODEL = 2048
HEAD_DIM = 128
HEADS_PER_GROUP = 8
DILATED_GROUPS = ((128, 1), (512, 4), (2048, 16))
N_ATTN_GROUPS = len(DILATED_GROUPS)
ATTN_OUT = HEADS_PER_GROUP * HEAD_DIM
BLK = 128
ROPE_THETA = 500000.0
ROT_DIM = HEAD_DIM // 4
ROT_HALF = ROT_DIM // 2
POOL_SIZES = (2, 4, 8, 16)
POOL_WIDTH = D_MODEL // 2
POOL_GROUP = POOL_WIDTH // len(POOL_SIZES)
NORM_EPS = 1e-6

LANES = 128
MXU_WIDTH = 256
VMEM_LIMIT_BYTES = 56 * 1024 * 1024

PROJ_TM = 1024
PROJ_TN = 1024
QKV_TM = 2048
PERM_TM = 256
NORM_TM = 512
ATTN_TB = 256
MERGE_TN = 512
POOL_HALO = 16

F32 = jnp.float32
BF16 = jnp.bfloat16

_NT = (((1,), (1,)), ((), ()))
_TN = (((0,), (0,)), ((), ()))


def _deinterleave_matrix(d):
    n = PERM_TM // d
    i = np.arange(PERM_TM)
    p = np.zeros((PERM_TM, PERM_TM), np.float32)
    p[i, (i % n) * d + i // n] = 1.0
    return p


def _norm_u_kernel(x_ref, g_ref, p4_ref, p16_ref, w_ref, h1_ref, h4_ref, h16_ref, dlt_ref,
                   hbuf_ref, wbf_ref, halo_ref, *, seq):
    step = pl.program_id(0)
    tm = x_ref.shape[0]

    @pl.when(step == 0)
    def _():
        wbf_ref[...] = w_ref[...].astype(BF16)
        hbuf_ref[...] = jnp.zeros_like(hbuf_ref)
        halo_ref[...] = jnp.zeros_like(halo_ref)

    pos0 = (jnp.maximum(step - 1, 0) * tm) % seq
    pos = pos0 + lax.broadcasted_iota(jnp.int32, (tm, 1), 0)

    def body(h_in_ref, h_out_ref):
        for g, ksz in enumerate(POOL_SIZES):
            cs = slice(g * POOL_GROUP, (g + 1) * POOL_GROUP)
            u = jnp.dot(h_in_ref[...], wbf_ref[:, cs], preferred_element_type=F32)
            halo = halo_ref[:, cs]
            ext = jnp.concatenate([jnp.where(pos0 == 0, jnp.zeros_like(halo), halo), u], axis=0)
            halo_ref[:, cs] = u[tm - POOL_HALO:]
            win, shift = ext, 1
            while shift < ksz:
                win = win + pltpu.roll(win, shift, 0)
                shift *= 2
            cnt = jnp.minimum(pos + 1, ksz).astype(F32)
            dlt_ref[:, cs] = (win[POOL_HALO:] / cnt - u).astype(dlt_ref.dtype)
        for sb in range(x_ref.shape[0] // PERM_TM):
            rs = slice(sb * PERM_TM, (sb + 1) * PERM_TM)
            x = x_ref[rs, :]
            ms = jnp.mean(x * x, axis=-1, keepdims=True)
            h = (x * lax.rsqrt(ms + NORM_EPS) * g_ref[...]).astype(BF16)
            h1_ref[rs, :] = h
            h_out_ref[rs, :] = h
            for p_ref, out_ref in ((p4_ref, h4_ref), (p16_ref, h16_ref)):
                d = out_ref.shape[0]
                n = PERM_TM // d
                hp = jnp.dot(p_ref[...], h, preferred_element_type=F32).astype(BF16)
                for r in range(d):
                    out_ref[r, sb * n:(sb + 1) * n, :] = hp[r * n:(r + 1) * n, :]

    for parity in range(2):
        pl.when(step % 2 == parity)(
            functools.partial(body, hbuf_ref.at[1 - parity], hbuf_ref.at[parity]))


def _rmsnorm_and_project_u(x2, gain, w_in, batch, seq, *, w_block):
    t, dm = x2.shape
    tm, tn = NORM_TM, PROJ_TN
    assert tn == POOL_WIDTH and POOL_GROUP == MXU_WIDTH and seq % tm == 0
    tiles = seq // tm
    n_tiles = t // tm
    perms = [jnp.asarray(_deinterleave_matrix(d), BF16) for _, d in DILATED_GROUPS[1:]]
    cur = lambda i: jnp.minimum(i, n_tiles - 1)
    prv = lambda i: jnp.maximum(i - 1, 0)
    out_shape = [jax.ShapeDtypeStruct((t, dm), BF16)]
    out_specs = [pl.BlockSpec((tm, dm), lambda i: (cur(i), 0))]
    for _, d in DILATED_GROUPS[1:]:
        out_shape.append(jax.ShapeDtypeStruct((batch, d, seq // d, dm), BF16))
        out_specs.append(pl.BlockSpec((None, d, tm // d, dm),
                                      lambda i: (cur(i) // tiles, 0, cur(i) % tiles, 0)))
    out_shape.append(jax.ShapeDtypeStruct((t, tn), BF16))
    out_specs.append(pl.BlockSpec((tm, tn), lambda i: (prv(i), 0)))
    const = lambda i: (0, 0)
    resident = functools.partial(pl.BlockSpec, pipeline_mode=pl.Buffered(1))
    h1, h4, h16, dlt = pl.pallas_call(
        functools.partial(_norm_u_kernel, seq=seq),
        grid=(n_tiles + 1,),
        in_specs=[pl.BlockSpec((tm, dm), lambda i: (cur(i), 0)),
                  pl.BlockSpec((1, dm), const),
                  pl.BlockSpec((PERM_TM, PERM_TM), const),
                  pl.BlockSpec((PERM_TM, PERM_TM), const),
                  resident((dm, tn), lambda i: (0, w_block))],
        out_specs=out_specs,
        out_shape=out_shape,
        scratch_shapes=[pltpu.VMEM((2, tm, dm), BF16), pltpu.VMEM((dm, tn), BF16),
                        pltpu.VMEM((POOL_HALO, tn), F32)],
        compiler_params=pltpu.CompilerParams(dimension_semantics=("arbitrary",),
                                             vmem_limit_bytes=VMEM_LIMIT_BYTES),
        name="rmsnorm_proj_u",
    )(x2, gain.reshape(1, dm), *perms, w_in)
    return (h1, h4.reshape(t, dm), h16.reshape(t, dm)), dlt


def _qkv_kernel(*refs, tm, tn):
    h_refs = refs[:N_ATTN_GROUPS]
    w_ref, gain_ref, cos_ref, sin_ref, out_ref, wt_ref, hbuf_ref, sem_ref = refs[N_ATTN_GROUPS:]
    rows = pl.num_programs(1)
    part = pl.program_id(0) % 3
    step = pl.program_id(0) * rows + pl.program_id(1)
    slot = step % 2

    def tile_copy(g, row_block, into):
        return pltpu.make_async_copy(h_refs[g].at[pl.ds(row_block * tm, tm), :],
                                     hbuf_ref.at[into], sem_ref.at[into])

    def start_fetch(of_step, into):
        for g in range(N_ATTN_GROUPS):
            pl.when(of_step // (3 * rows) == g)(lambda g=g: tile_copy(g, of_step % rows, into).start())

    pl.when(step == 0)(lambda: start_fetch(step, slot))
    pl.when(step + 1 < pl.num_programs(0) * rows)(lambda: start_fetch(step + 1, 1 - slot))
    tile_copy(0, 0, slot).wait()

    @pl.when(pl.program_id(1) == 0)
    def _():
        for c in range(tn // MXU_WIDTH):
            cs = slice(c * MXU_WIDTH, (c + 1) * MXU_WIDTH)
            wt_ref[cs, :] = w_ref[:, cs].T.astype(BF16)

    body = functools.partial(_qkv_body, hbuf_ref.at[slot], gain_ref, cos_ref, sin_ref, out_ref, wt_ref,
                             tm=tm, tn=tn)
    pl.when(part < 2)(functools.partial(body, normed=True))
    pl.when(part == 2)(functools.partial(body, normed=False))


def _qkv_body(h_ref, gain_ref, cos_ref, sin_ref, out_ref, wt_ref, *, tm, tn, normed):
    def chunk(c):
        ts = slice(c * ATTN_TB, (c + 1) * ATTN_TB)
        return lax.dot_general(wt_ref[...], h_ref[ts, :], _NT, preferred_element_type=F32)

    if not normed:
        for c in range(tm // ATTN_TB):
            out_ref[c] = chunk(c).astype(out_ref.dtype)
    else:
        for c in range(tm // ATTN_TB):
            acc = chunk(c)
            for hh in range(tn // HEAD_DIM):
                hs = slice(hh * HEAD_DIM, (hh + 1) * HEAD_DIM)
                for lt in range(ATTN_TB // LANES):
                    ls = slice(lt * LANES, (lt + 1) * LANES)
                    pos = slice(c * ATTN_TB + lt * LANES, c * ATTN_TB + (lt + 1) * LANES)
                    a = acc[hs, ls]
                    ss = jnp.sum(a * a, axis=0, keepdims=True)
                    r = lax.rsqrt(ss * (1.0 / HEAD_DIM) + NORM_EPS)
                    ag = a * gain_ref[...]
                    lo, hi = ag[0:ROT_HALF], ag[ROT_HALF:ROT_DIM]
                    cos, sin = cos_ref[:, pos], sin_ref[:, pos]
                    y = jnp.concatenate([lo * cos - hi * sin, hi * cos + lo * sin, ag[ROT_DIM:]], axis=0)
                    out_ref[c, hs, ls] = (y * r).astype(out_ref.dtype)


def _project_qkv(hs, w_in, gains, cos_t, sin_t, seq):
    t, dm = hs[0].shape
    tm, tn = QKV_TM, PROJ_TN
    seq_tiles = seq // tm
    group = lambda gj: gj // 3
    part = lambda gj: gj % 3
    table = pl.BlockSpec((None, ROT_HALF, tm), lambda gj, i: (group(gj), 0, i % seq_tiles))
    return pl.pallas_call(
        functools.partial(_qkv_kernel, tm=tm, tn=tn),
        grid=(3 * N_ATTN_GROUPS, t // tm),
        in_specs=[*(pl.BlockSpec(memory_space=pl.ANY) for _ in range(N_ATTN_GROUPS)),
                  pl.BlockSpec((dm, tn), lambda gj, i: (0, part(gj) * N_ATTN_GROUPS + group(gj))),
                  pl.BlockSpec((None, HEAD_DIM, LANES), lambda gj, i: (jnp.minimum(part(gj), 1), 0, 0)),
                  table, table],
        out_specs=pl.BlockSpec((None, None, tm // ATTN_TB, tn, ATTN_TB),
                               lambda gj, i: (group(gj), part(gj), i, 0, 0)),
        out_shape=jax.ShapeDtypeStruct((N_ATTN_GROUPS, 3, t // ATTN_TB, tn, ATTN_TB), BF16),
        scratch_shapes=[pltpu.VMEM((tn, dm), BF16), pltpu.VMEM((2, tm, dm), BF16),
                        pltpu.SemaphoreType.DMA((2,))],
        compiler_params=pltpu.CompilerParams(
            dimension_semantics=("arbitrary", "arbitrary"),
            vmem_limit_bytes=VMEM_LIMIT_BYTES),
        name="proj_qkv",
    )(*hs, w_in, gains, cos_t, sin_t)


def _rope_tables_t(seq, d):
    inv_freq = ROPE_THETA ** (-jnp.arange(0, ROT_DIM, 2, dtype=F32) / ROT_DIM)
    pos = jnp.arange(seq, dtype=jnp.int32).reshape(seq // d, d).T.reshape(seq)
    ang = pos.astype(F32)[None, :] * inv_freq[:, None]
    return jnp.cos(ang), jnp.sin(ang)


def _gate_epilogue(acc, cs, half_bias_ref):
    return 0.5 + 0.5 * jnp.tanh(0.5 * acc + half_bias_ref[:, cs])


def _silu_epilogue(acc, cs, half_bias_ref):
    del cs, half_bias_ref
    half = 0.5 * acc
    return half + half * jnp.tanh(half)


def _proj_kernel(*refs, epilogues, n_aux, tn):
    h_ref, w_ref = refs[0], refs[1]
    aux = refs[2:2 + n_aux]
    out_ref = refs[2 + n_aux]
    wbf_ref = refs[3 + n_aux]
    j = pl.program_id(0)

    @pl.when(pl.program_id(1) == 0)
    def _():
        wbf_ref[...] = w_ref[...].astype(BF16)

    def body(epilogue):
        for c in range(tn // MXU_WIDTH):
            cs = slice(c * MXU_WIDTH, (c + 1) * MXU_WIDTH)
            acc = jnp.dot(h_ref[...], wbf_ref[:, cs], preferred_element_type=F32)
            out_ref[:, cs] = epilogue(acc, cs, *aux).astype(out_ref.dtype)

    if len(epilogues) == 1:
        body(epilogues[0][1])
    else:
        bounds = [first for first, _ in epilogues[1:]] + [None]
        for (first, fn), last in zip(epilogues, bounds):
            cond = (j >= first) if last is None else ((j >= first) & (j < last))
            pl.when(cond)(functools.partial(body, fn))


def _project(h, w_in, *, n_blocks, w_block_of, epilogues, out_dtype, aux=(), aux_specs=(), name):
    t, d = h.shape
    tm, tn = PROJ_TM, PROJ_TN
    kern = functools.partial(_proj_kernel, epilogues=epilogues, n_aux=len(aux), tn=tn)
    return pl.pallas_call(
        kern,
        grid=(n_blocks, t // tm),
        in_specs=[pl.BlockSpec((tm, d), lambda j, i: (i, 0)),
                  pl.BlockSpec((d, tn), lambda j, i: (0, w_block_of(j))),
                  *aux_specs],
        out_specs=pl.BlockSpec((tm, tn), lambda j, i: (i, j)),
        out_shape=jax.ShapeDtypeStruct((t, n_blocks * tn), out_dtype),
        scratch_shapes=[pltpu.VMEM((d, tn), BF16)],
        compiler_params=pltpu.CompilerParams(
            dimension_semantics=("arbitrary", "arbitrary"),
            vmem_limit_bytes=VMEM_LIMIT_BYTES),
        name=name,
    )(h, w_in, *aux)


def _band_bias():
    i = np.arange(BLK)[:, None]
    j = np.arange(2 * BLK)[None, :]
    band = (j >= i) & (j <= i + BLK)
    first = band & (j >= BLK)
    return np.where(np.stack([band, first]), 0.0, -np.inf).astype(np.float32)


class _AttentionBlocks:
    def __init__(self, bias_ref, q_ref, k_ref, v_ref, o_ref, lse_ref, kprev_ref, vprev_ref, firsts):
        self.bias_ref, self.q_ref, self.k_ref, self.v_ref = bias_ref, q_ref, k_ref, v_ref
        self.o_ref, self.lse_ref, self.kprev_ref, self.vprev_ref = o_ref, lse_ref, kprev_ref, vprev_ref
        self.firsts = firsts
        self.per = ATTN_TB // BLK
        self.scores = {}

    def _block(self, ref, n, cs):
        lo = (n % self.per) * BLK
        return ref[n // self.per, cs, lo:lo + BLK]

    def _with_previous(self, ref, prev_ref, n, cs):
        prev = prev_ref[cs, :] if n == 0 else self._block(ref, n - 1, cs)
        return jnp.concatenate([prev, self._block(ref, n, cs)], axis=1)

    def score_phase(self, n):
        first = self.firsts[n]
        bias = self.bias_ref[int(first)] if isinstance(first, bool) else self.bias_ref[jnp.where(first, 1, 0)]
        for h in range(HEADS_PER_GROUP):
            cs = slice(h * HEAD_DIM, (h + 1) * HEAD_DIM)
            kk = self._with_previous(self.k_ref, self.kprev_ref, n, cs)
            s = lax.dot_general(self._block(self.q_ref, n, cs), kk, _TN, preferred_element_type=F32)
            self.scores[n, h] = s + bias

    def value_phase(self, n):
        ts = slice(n * BLK, (n + 1) * BLK)
        lane = lax.broadcasted_iota(jnp.int32, (BLK, LANES), 1)
        m_tile = jnp.zeros((BLK, LANES), F32)
        l_tile = jnp.ones((BLK, LANES), F32)
        for h in range(HEADS_PER_GROUP):
            cs = slice(h * HEAD_DIM, (h + 1) * HEAD_DIM)
            s = self.scores.pop((n, h))
            m = jnp.max(s, axis=-1, keepdims=True)
            p = jnp.exp(s - m)
            l = jnp.sum(p, axis=-1, keepdims=True)
            vv = self._with_previous(self.v_ref, self.vprev_ref, n, cs)
            o = lax.dot_general(p.astype(BF16), vv, _NT, preferred_element_type=F32)
            self.o_ref[ts, cs] = (o / l).astype(self.o_ref.dtype)
            m_tile = jnp.where(lane == h, m, m_tile)
            l_tile = jnp.where(lane == h, l, l_tile)
        self.lse_ref[ts, :] = m_tile + jnp.log(l_tile)

    def carry(self, n_blocks):
        self.kprev_ref[...] = self._block(self.k_ref, n_blocks - 1, slice(None))
        self.vprev_ref[...] = self._block(self.v_ref, n_blocks - 1, slice(None))


def _rest_attn_kernel(h_ref, w_ref, half_bias_ref, bias_ref, q_ref, k_ref, v_ref,
                      out_ref, o_ref, lse_ref, wbf_ref, kprev_ref, vprev_ref,
                      *, tn, n_gate_blocks, steps_per_group, tiles_per_sub):
    j, i = pl.program_id(0), pl.program_id(1)
    step = j * pl.num_programs(1) + i

    @pl.when(i == 0)
    def _():
        wbf_ref[...] = w_ref[...].astype(BF16)

    @pl.when(step == 0)
    def _():
        kprev_ref[...] = jnp.zeros_like(kprev_ref)
        vprev_ref[...] = jnp.zeros_like(vprev_ref)

    n_blocks = tn // MXU_WIDTH
    per = ATTN_TB // BLK
    tiles_per_step = n_blocks // per
    group = step // steps_per_group
    tps = functools.reduce(lambda acc, g: jnp.where(group == g, tiles_per_sub[g], acc),
                           range(len(tiles_per_sub)), tiles_per_sub[0])
    tile0 = (step % steps_per_group) * tiles_per_step
    firsts = [((tile0 + n // per) & (tps - 1)) == 0 if n % per == 0 else False for n in range(n_blocks)]

    def body(epilogue):
        attn = _AttentionBlocks(bias_ref, q_ref, k_ref, v_ref, o_ref, lse_ref, kprev_ref, vprev_ref, firsts)
        for c in range(n_blocks):
            cs = slice(c * MXU_WIDTH, (c + 1) * MXU_WIDTH)
            attn.score_phase(c)
            acc = jnp.dot(h_ref[...], wbf_ref[:, cs], preferred_element_type=F32)
            if c > 0:
                attn.value_phase(c - 1)
            out_ref[:, cs] = epilogue(acc, cs, half_bias_ref).astype(out_ref.dtype)
        attn.value_phase(n_blocks - 1)
        attn.carry(n_blocks)

    pl.when(j < n_gate_blocks)(functools.partial(body, _gate_epilogue))
    pl.when(j >= n_gate_blocks)(functools.partial(body, _silu_epilogue))


def _project_rest_with_attention(h, w_in, half_bias, qkv_t, seq, *, n_gate_blocks, w_block_of):
    t, dm = h.shape
    tm, tn = PROJ_TM, PROJ_TN
    n_blocks = n_gate_blocks + 2
    rows = t // tm
    n_groups, _, n_tiles = qkv_t.shape[:3]
    blocks_per_step = tn // MXU_WIDTH
    tiles_per_step = blocks_per_step * BLK // ATTN_TB
    steps_per_group = n_tiles // tiles_per_step
    assert n_groups * steps_per_group == n_blocks * rows, "attention tiles must fill the projection steps"
    tiles_per_sub = tuple(seq // d // ATTN_TB for _, d in DILATED_GROUPS)
    step = lambda j, i: j * rows + i
    tile = lambda part: pl.BlockSpec(
        (None, None, tiles_per_step, ATTN_OUT, ATTN_TB),
        lambda j, i: (step(j, i) // steps_per_group, part, step(j, i) % steps_per_group, 0, 0))
    tokens = tiles_per_step * ATTN_TB
    per_group = lambda width: pl.BlockSpec(
        (None, tokens, width), lambda j, i: (step(j, i) // steps_per_group, step(j, i) % steps_per_group, 0))
    return pl.pallas_call(
        functools.partial(_rest_attn_kernel, tn=tn, n_gate_blocks=n_gate_blocks,
                          steps_per_group=steps_per_group, tiles_per_sub=tiles_per_sub),
        grid=(n_blocks, rows),
        in_specs=[pl.BlockSpec((tm, dm), lambda j, i: (i, 0)),
                  pl.BlockSpec((dm, tn), lambda j, i: (0, w_block_of(j))),
                  pl.BlockSpec((1, tn), lambda j, i: (0, jnp.minimum(j, n_gate_blocks - 1))),
                  pl.BlockSpec((2, BLK, 2 * BLK), lambda j, i: (0, 0, 0), pipeline_mode=pl.Buffered(1)),
                  tile(0), tile(1), tile(2)],
        out_specs=[pl.BlockSpec((tm, tn), lambda j, i: (i, j)),
                   per_group(ATTN_OUT), per_group(LANES)],
        out_shape=[jax.ShapeDtypeStruct((t, n_blocks * tn), BF16),
                   jax.ShapeDtypeStruct((n_groups, t, ATTN_OUT), BF16),
                   jax.ShapeDtypeStruct((n_groups, t, LANES), F32)],
        scratch_shapes=[pltpu.VMEM((dm, tn), BF16),
                        pltpu.VMEM((ATTN_OUT, BLK), BF16), pltpu.VMEM((ATTN_OUT, BLK), BF16)],
        compiler_params=pltpu.CompilerParams(
            dimension_semantics=("arbitrary", "arbitrary"),
            vmem_limit_bytes=VMEM_LIMIT_BYTES),
        name="proj_rest_attention",
    )(h, w_in, half_bias, jnp.asarray(_band_bias()), qkv_t, qkv_t, qkv_t)


def _rows(ref):
    return jnp.concatenate([ref[r] for r in range(ref.shape[0])], axis=0)


def _unpermute_f32(q, x):
    hi = x.astype(BF16).astype(F32)
    r1 = x - hi
    mid = r1.astype(BF16).astype(F32)
    lo = r1 - mid
    packed = hi + pltpu.roll(mid, HEADS_PER_GROUP, 1) + pltpu.roll(lo, 2 * HEADS_PER_GROUP, 1)
    y = jnp.dot(q, packed.astype(BF16), preferred_element_type=F32)
    return (y + pltpu.roll(y, LANES - HEADS_PER_GROUP, 1)) + pltpu.roll(y, LANES - 2 * HEADS_PER_GROUP, 1)


def _tail_kernel(o0_ref, o1_ref, o2_ref, l0_ref, l1_ref, l2_ref, q4_ref, q16_ref, ga_ref, gp_ref,
                 za_ref, zp_ref, dlt_ref, wba32_ref, wbp32_ref, pm32_ref, ps_ref,
                 out_ref, a_ref, b_ref, wba_ref, wbp_ref, pm_ref):
    step = pl.program_id(0)

    @pl.when(step == 0)
    def _():
        a_ref[...] = jnp.zeros_like(a_ref)
        b_ref[...] = jnp.zeros_like(b_ref)
        wba_ref[...] = wba32_ref[...].astype(BF16)
        wbp_ref[...] = wbp32_ref[...].astype(BF16)
        pm_ref[...] = pm32_ref[...].astype(BF16)

    body = functools.partial(
        _tail_step, o0_ref, o1_ref, o2_ref, l0_ref, l1_ref, l2_ref, q4_ref, q16_ref, ga_ref, gp_ref,
        za_ref, zp_ref, dlt_ref, wba_ref, wbp_ref, pm_ref, ps_ref, out_ref)
    for parity in range(2):
        pl.when(step % 2 == parity)(
            functools.partial(body, a_ref.at[1 - parity], b_ref.at[1 - parity],
                              a_ref.at[parity], b_ref.at[parity]))


def _tail_step(o0_ref, o1_ref, o2_ref, l0_ref, l1_ref, l2_ref, q4_ref, q16_ref, ga_ref, gp_ref,
               za_ref, zp_ref, dlt_ref, wba_ref, wbp_ref, pm_ref, ps_ref, out_ref,
               a_in_ref, b_in_ref, a_ref, b_ref):
    q4, q16 = q4_ref[...], q16_ref[...]
    o0 = o0_ref[...]
    o1 = jnp.dot(q4, _rows(o1_ref), preferred_element_type=F32)
    o2 = jnp.dot(q16, _rows(o2_ref), preferred_element_type=F32)
    l0 = l0_ref[...]
    l1 = _unpermute_f32(q4, _rows(l1_ref))
    l2 = _unpermute_f32(q16, _rows(l2_ref))

    def merge_chunk(c):
        cs = slice(c * MERGE_TN, (c + 1) * MERGE_TN)
        y_attn = jnp.dot(a_in_ref[...], wba_ref[:, cs], preferred_element_type=F32)
        y_pool = jnp.dot(b_in_ref[...], wbp_ref[:, cs], preferred_element_type=F32)
        merged = ga_ref[:, cs].astype(F32) * y_attn + gp_ref[:, cs].astype(F32) * y_pool
        out_ref[:, cs] = merged.astype(out_ref.dtype)

    pooled = []
    for g in range(len(POOL_SIZES)):
        cs = slice(g * POOL_GROUP, (g + 1) * POOL_GROUP)
        pooled.append(jnp.dot(dlt_ref[:, cs], pm_ref[g], preferred_element_type=F32))
    pool = jnp.concatenate(pooled, axis=1) * ps_ref[...]
    b_ref[...] = (pool * zp_ref[...].astype(F32)).astype(BF16)

    for c in range(D_MODEL // MERGE_TN):
        merge_chunk(c)

    mx = jnp.maximum(jnp.maximum(l0, l1), l2)
    e0, e1, e2 = jnp.exp(l0 - mx), jnp.exp(l1 - mx), jnp.exp(l2 - mx)
    inv = 1.0 / (e0 + e1 + e2)
    w0, w1, w2 = e0 * inv, e1 * inv, e2 * inv
    for h in range(HEADS_PER_GROUP):
        cs = slice(h * HEAD_DIM, (h + 1) * HEAD_DIM)
        a = (w0[:, h:h + 1] * o0[:, cs].astype(F32)
             + w1[:, h:h + 1] * o1[:, cs]
             + w2[:, h:h + 1] * o2[:, cs])
        a_ref[:, cs] = (a * za_ref[:, cs].astype(F32)).astype(BF16)


def _residual_epilogue(acc, cs, x_ref):
    return acc + x_ref[:, cs]


def _tail(os_, lses, rest, dlt, wba, wbp, pm, ps, batch, seq, *, gate_block0, za_block, zp_block):
    t, dm = dlt.shape[0], D_MODEL
    tm = PERM_TM
    tiles = seq // tm
    n_tiles = t // tm
    resident = functools.partial(pl.BlockSpec, pipeline_mode=pl.Buffered(1))
    cur = lambda i: jnp.minimum(i, n_tiles - 1)
    prv = lambda i: jnp.maximum(i - 1, 0)
    row = lambda blk: (lambda i: (cur(i), blk))
    row_prv = lambda blk: (lambda i: (prv(i), blk))
    gate_blocks = D_MODEL // ATTN_OUT
    unperms = [jnp.asarray(_deinterleave_matrix(d).T, BF16) for _, d in DILATED_GROUPS[1:]]
    o_in, l_in, o_specs, l_specs = [], [], [], []
    for g, (_, d) in enumerate(DILATED_GROUPS):
        for arr, width, ins, specs in ((os_, ATTN_OUT, o_in, o_specs), (lses, LANES, l_in, l_specs)):
            if d == 1:
                ins.append(arr)
                specs.append(pl.BlockSpec((None, tm, width), lambda i, g=g: (g, cur(i), 0)))
            else:
                ins.append(arr.reshape(arr.shape[0], batch, d, seq // d, width))
                specs.append(pl.BlockSpec((None, None, d, tm // d, width),
                                          lambda i, g=g: (g, cur(i) // tiles, 0, cur(i) % tiles, 0)))
    in_specs = (
        o_specs + l_specs
        + [resident((tm, tm), lambda i: (0, 0)), resident((tm, tm), lambda i: (0, 0)),
           pl.BlockSpec((tm, D_MODEL), row_prv(gate_block0 // gate_blocks)),
           pl.BlockSpec((tm, D_MODEL), row_prv(gate_block0 // gate_blocks + 1)),
           pl.BlockSpec((tm, ATTN_OUT), row(za_block)),
           pl.BlockSpec((tm, POOL_WIDTH), row(zp_block)),
           pl.BlockSpec((tm, POOL_WIDTH), row(0)),
           resident(wba.shape, lambda i: (0, 0)),
           resident(wbp.shape, lambda i: (0, 0)),
           resident(pm.shape, lambda i: (0, 0, 0)),
           resident(ps.shape, lambda i: (0, 0))])
    return pl.pallas_call(
        _tail_kernel,
        grid=(n_tiles + 1,),
        in_specs=in_specs,
        out_specs=pl.BlockSpec((tm, dm), row_prv(0)),
        out_shape=jax.ShapeDtypeStruct((t, dm), BF16),
        scratch_shapes=[pltpu.VMEM((2, tm, ATTN_OUT), BF16), pltpu.VMEM((2, tm, POOL_WIDTH), BF16),
                        pltpu.VMEM(wba.shape, BF16), pltpu.VMEM(wbp.shape, BF16), pltpu.VMEM(pm.shape, BF16)],
        compiler_params=pltpu.CompilerParams(
            dimension_semantics=("arbitrary",),
            vmem_limit_bytes=VMEM_LIMIT_BYTES),
        name="merge_pool",
    )(*o_in, *l_in, *unperms, rest, rest, rest, rest, dlt, wba, wbp, pm, ps)


def kernel(x, norm_gain, w_in, b_gates, q_norm_gain, k_norm_gain, pool_maps, pool_scale,
           w_branch_attn, w_branch_pool, w_out):
    batch, seq, dm = x.shape
    t = batch * seq
    x2 = x.reshape(t, dm)
    hs, dlt = _rmsnorm_and_project_u(x2, norm_gain, w_in, batch, seq, w_block=10)

    lane_rep = lambda g: jnp.broadcast_to(g.astype(F32)[:, None], (HEAD_DIM, LANES))
    gains = jnp.stack([lane_rep(q_norm_gain * HEAD_DIM ** -0.5), lane_rep(k_norm_gain)])
    tables = [_rope_tables_t(seq, d) for _, d in DILATED_GROUPS]
    qkv_t = _project_qkv(hs, w_in, gains, jnp.stack([c for c, _ in tables]),
                         jnp.stack([s for _, s in tables]), seq)

    n_gate_blocks = 2 * D_MODEL // PROJ_TN
    half_bias = (0.5 * b_gates.astype(F32)).reshape(1, 2 * D_MODEL)
    rest_w_block = lambda j: jnp.where(j < 4, j + 12, jnp.where(j == 4, 9, 11))
    rest, o_all, lse_all = _project_rest_with_attention(
        hs[0], w_in, half_bias, qkv_t, seq, n_gate_blocks=n_gate_blocks, w_block_of=rest_w_block)

    merged = _tail(o_all, lse_all, rest, dlt,
                   w_branch_attn, w_branch_pool, pool_maps,
                   pool_scale.astype(F32).reshape(1, POOL_WIDTH), batch, seq,
                   gate_block0=0, za_block=4, zp_block=5)
    out = _project(merged, w_out, n_blocks=dm // PROJ_TN, w_block_of=lambda j: j,
                   epilogues=((0, _residual_epilogue),), out_dtype=F32, aux=(x2,),
                   aux_specs=(pl.BlockSpec((PROJ_TM, PROJ_TN), lambda j, i: (i, j)),),
                   name="proj_out")
    return out.reshape(batch, seq, dm)
```

```python
import functools

import numpy as np
import jax
import jax.numpy as jnp
from jax import lax
from jax.experimental import pallas as pl
from jax.experimental.pallas import tpu as pltpu

D_MODEL = 2048
HEAD_DIM = 128
HEADS_PER_GROUP = 8
DILATED_GROUPS = ((128, 1), (512, 4), (2048, 16))
N_ATTN_GROUPS = len(DILATED_GROUPS)
ATTN_OUT = HEADS_PER_GROUP * HEAD_DIM
BLK = 128
ROPE_THETA = 500000.0
ROT_DIM = HEAD_DIM // 4
ROT_HALF = ROT_DIM // 2
POOL_SIZES = (2, 4, 8, 16)
POOL_WIDTH = D_MODEL // 2
POOL_GROUP = POOL_WIDTH // len(POOL_SIZES)
NORM_EPS = 1e-6

LANES = 128
MXU_WIDTH = 256
VMEM_LIMIT_BYTES = 56 * 1024 * 1024

PROJ_TM = 1024
PROJ_TN = 1024
QKV_TM = 2048
PERM_TM = 256
NORM_TM = 512
ATTN_TB = 256
MERGE_TN = 512
GATE_RING_SLOTS = 3
POOL_HALO = 16

F32 = jnp.float32
BF16 = jnp.bfloat16

_NT = (((1,), (1,)), ((), ()))
_TN = (((0,), (0,)), ((), ()))


def _deinterleave_matrix(d):
    n = PERM_TM // d
    i = np.arange(PERM_TM)
    p = np.zeros((PERM_TM, PERM_TM), np.float32)
    p[i, (i % n) * d + i // n] = 1.0
    return p


def _norm_u_kernel(x_ref, g_ref, p4_ref, p16_ref, w_ref, h1_ref, h4_ref, h16_ref, u_ref,
                   hbuf_ref, wbf_ref):
    step = pl.program_id(0)

    @pl.when(step == 0)
    def _():
        wbf_ref[...] = w_ref[...].astype(BF16)
        hbuf_ref[...] = jnp.zeros_like(hbuf_ref)

    def body(h_in_ref, h_out_ref):
        for c in range(u_ref.shape[1] // MXU_WIDTH):
            cs = slice(c * MXU_WIDTH, (c + 1) * MXU_WIDTH)
            u_ref[:, cs] = jnp.dot(h_in_ref[...], wbf_ref[:, cs], preferred_element_type=F32)
        for sb in range(x_ref.shape[0] // PERM_TM):
            rs = slice(sb * PERM_TM, (sb + 1) * PERM_TM)
            x = x_ref[rs, :]
            ms = jnp.mean(x * x, axis=-1, keepdims=True)
            h = (x * lax.rsqrt(ms + NORM_EPS) * g_ref[...]).astype(BF16)
            h1_ref[rs, :] = h
            h_out_ref[rs, :] = h
            for p_ref, out_ref in ((p4_ref, h4_ref), (p16_ref, h16_ref)):
                d = out_ref.shape[0]
                n = PERM_TM // d
                hp = jnp.dot(p_ref[...], h, preferred_element_type=F32).astype(BF16)
                for r in range(d):
                    out_ref[r, sb * n:(sb + 1) * n, :] = hp[r * n:(r + 1) * n, :]

    for parity in range(2):
        pl.when(step % 2 == parity)(
            functools.partial(body, hbuf_ref.at[1 - parity], hbuf_ref.at[parity]))


def _rmsnorm_and_project_u(x2, gain, w_in, batch, seq, *, w_block):
    t, dm = x2.shape
    tm, tn = NORM_TM, PROJ_TN
    tiles = seq // tm
    n_tiles = t // tm
    perms = [jnp.asarray(_deinterleave_matrix(d), BF16) for _, d in DILATED_GROUPS[1:]]
    cur = lambda i: jnp.minimum(i, n_tiles - 1)
    prv = lambda i: jnp.maximum(i - 1, 0)
    out_shape = [jax.ShapeDtypeStruct((t, dm), BF16)]
    out_specs = [pl.BlockSpec((tm, dm), lambda i: (cur(i), 0))]
    for _, d in DILATED_GROUPS[1:]:
        out_shape.append(jax.ShapeDtypeStruct((batch, d, seq // d, dm), BF16))
        out_specs.append(pl.BlockSpec((None, d, tm // d, dm),
                                      lambda i: (cur(i) // tiles, 0, cur(i) % tiles, 0)))
    out_shape.append(jax.ShapeDtypeStruct((t, tn), F32))
    out_specs.append(pl.BlockSpec((tm, tn), lambda i: (prv(i), 0)))
    const = lambda i: (0, 0)
    resident = functools.partial(pl.BlockSpec, pipeline_mode=pl.Buffered(1))
    h1, h4, h16, u = pl.pallas_call(
        _norm_u_kernel,
        grid=(n_tiles + 1,),
        in_specs=[pl.BlockSpec((tm, dm), lambda i: (cur(i), 0)),
                  pl.BlockSpec((1, dm), const),
                  pl.BlockSpec((PERM_TM, PERM_TM), const),
                  pl.BlockSpec((PERM_TM, PERM_TM), const),
                  resident((dm, tn), lambda i: (0, w_block))],
        out_specs=out_specs,
        out_shape=out_shape,
        scratch_shapes=[pltpu.VMEM((2, tm, dm), BF16), pltpu.VMEM((dm, tn), BF16)],
        compiler_params=pltpu.CompilerParams(dimension_semantics=("arbitrary",),
                                             vmem_limit_bytes=VMEM_LIMIT_BYTES),
        name="rmsnorm_proj_u",
    )(x2, gain.reshape(1, dm), *perms, w_in)
    return (h1, h4.reshape(t, dm), h16.reshape(t, dm)), u


def _qkv_kernel(*refs, tm, tn):
    h_refs = refs[:N_ATTN_GROUPS]
    w_ref, gain_ref, cos_ref, sin_ref, out_ref, wt_ref, hbuf_ref, sem_ref = refs[N_ATTN_GROUPS:]
    rows = pl.num_programs(1)
    part = pl.program_id(0) % 3
    step = pl.program_id(0) * rows + pl.program_id(1)
    slot = step % 2

    def tile_copy(g, row_block, into):
        return pltpu.make_async_copy(h_refs[g].at[pl.ds(row_block * tm, tm), :],
                                     hbuf_ref.at[into], sem_ref.at[into])

    def start_fetch(of_step, into):
        for g in range(N_ATTN_GROUPS):
            pl.when(of_step // (3 * rows) == g)(lambda g=g: tile_copy(g, of_step % rows, into).start())

    pl.when(step == 0)(lambda: start_fetch(step, slot))
    pl.when(step + 1 < pl.num_programs(0) * rows)(lambda: start_fetch(step + 1, 1 - slot))
    tile_copy(0, 0, slot).wait()

    @pl.when(pl.program_id(1) == 0)
    def _():
        for c in range(tn // MXU_WIDTH):
            cs = slice(c * MXU_WIDTH, (c + 1) * MXU_WIDTH)
            wt_ref[cs, :] = w_ref[:, cs].T.astype(BF16)

    body = functools.partial(_qkv_body, hbuf_ref.at[slot], gain_ref, cos_ref, sin_ref, out_ref, wt_ref,
                             tm=tm, tn=tn)
    pl.when(part < 2)(functools.partial(body, normed=True))
    pl.when(part == 2)(functools.partial(body, normed=False))


def _qkv_body(h_ref, gain_ref, cos_ref, sin_ref, out_ref, wt_ref, *, tm, tn, normed):
    def chunk(c):
        ts = slice(c * ATTN_TB, (c + 1) * ATTN_TB)
        return lax.dot_general(wt_ref[...], h_ref[ts, :], _NT, preferred_element_type=F32)

    if not normed:
        for c in range(tm // ATTN_TB):
            out_ref[c] = chunk(c).astype(out_ref.dtype)
    else:
        for c in range(tm // ATTN_TB):
            acc = chunk(c)
            for hh in range(tn // HEAD_DIM):
                hs = slice(hh * HEAD_DIM, (hh + 1) * HEAD_DIM)
                for lt in range(ATTN_TB // LANES):
                    ls = slice(lt * LANES, (lt + 1) * LANES)
                    pos = slice(c * ATTN_TB + lt * LANES, c * ATTN_TB + (lt + 1) * LANES)
                    a = acc[hs, ls]
                    ss = jnp.sum(a * a, axis=0, keepdims=True)
                    r = lax.rsqrt(ss * (1.0 / HEAD_DIM) + NORM_EPS)
                    ag = a * gain_ref[...]
                    lo, hi = ag[0:ROT_HALF], ag[ROT_HALF:ROT_DIM]
                    cos, sin = cos_ref[:, pos], sin_ref[:, pos]
                    y = jnp.concatenate([lo * cos - hi * sin, hi * cos + lo * sin, ag[ROT_DIM:]], axis=0)
                    out_ref[c, hs, ls] = (y * r).astype(out_ref.dtype)


def _project_qkv(hs, w_in, gains, cos_t, sin_t, seq):
    t, dm = hs[0].shape
    tm, tn = QKV_TM, PROJ_TN
    seq_tiles = seq // tm
    group = lambda gj: gj // 3
    part = lambda gj: gj % 3
    table = pl.BlockSpec((None, ROT_HALF, tm), lambda gj, i: (group(gj), 0, i % seq_tiles))
    return pl.pallas_call(
        functools.partial(_qkv_kernel, tm=tm, tn=tn),
        grid=(3 * N_ATTN_GROUPS, t // tm),
        in_specs=[*(pl.BlockSpec(memory_space=pl.ANY) for _ in range(N_ATTN_GROUPS)),
                  pl.BlockSpec((dm, tn), lambda gj, i: (0, part(gj) * N_ATTN_GROUPS + group(gj))),
                  pl.BlockSpec((None, HEAD_DIM, LANES), lambda gj, i: (jnp.minimum(part(gj), 1), 0, 0)),
                  table, table],
        out_specs=pl.BlockSpec((None, None, tm // ATTN_TB, tn, ATTN_TB),
                               lambda gj, i: (group(gj), part(gj), i, 0, 0)),
        out_shape=jax.ShapeDtypeStruct((N_ATTN_GROUPS, 3, t // ATTN_TB, tn, ATTN_TB), BF16),
        scratch_shapes=[pltpu.VMEM((tn, dm), BF16), pltpu.VMEM((2, tm, dm), BF16),
                        pltpu.SemaphoreType.DMA((2,))],
        compiler_params=pltpu.CompilerParams(
            dimension_semantics=("arbitrary", "arbitrary"),
            vmem_limit_bytes=VMEM_LIMIT_BYTES),
        name="proj_qkv",
    )(*hs, w_in, gains, cos_t, sin_t)


def _rope_tables_t(seq, d):
    inv_freq = ROPE_THETA ** (-jnp.arange(0, ROT_DIM, 2, dtype=F32) / ROT_DIM)
    pos = jnp.arange(seq, dtype=jnp.int32).reshape(seq // d, d).T.reshape(seq)
    ang = pos.astype(F32)[None, :] * inv_freq[:, None]
    return jnp.cos(ang), jnp.sin(ang)


def _gate_epilogue(acc, cs, half_bias_ref):
    return 0.5 + 0.5 * jnp.tanh(0.5 * acc + half_bias_ref[:, cs])


def _silu_epilogue(acc, cs, half_bias_ref):
    del cs, half_bias_ref
    half = 0.5 * acc
    return half + half * jnp.tanh(half)


def _proj_kernel(*refs, epilogues, n_aux, tn):
    h_ref, w_ref = refs[0], refs[1]
    aux = refs[2:2 + n_aux]
    out_ref = refs[2 + n_aux]
    wbf_ref = refs[3 + n_aux]
    j = pl.program_id(0)

    @pl.when(pl.program_id(1) == 0)
    def _():
        wbf_ref[...] = w_ref[...].astype(BF16)

    def body(epilogue):
        for c in range(tn // MXU_WIDTH):
            cs = slice(c * MXU_WIDTH, (c + 1) * MXU_WIDTH)
            acc = jnp.dot(h_ref[...], wbf_ref[:, cs], preferred_element_type=F32)
            out_ref[:, cs] = epilogue(acc, cs, *aux).astype(out_ref.dtype)

    if len(epilogues) == 1:
        body(epilogues[0][1])
    else:
        bounds = [first for first, _ in epilogues[1:]] + [None]
        for (first, fn), last in zip(epilogues, bounds):
            cond = (j >= first) if last is None else ((j >= first) & (j < last))
            pl.when(cond)(functools.partial(body, fn))


def _project(h, w_in, *, n_blocks, w_block_of, epilogues, out_dtype, aux=(), aux_specs=(), name):
    t, d = h.shape
    tm, tn = PROJ_TM, PROJ_TN
    kern = functools.partial(_proj_kernel, epilogues=epilogues, n_aux=len(aux), tn=tn)
    return pl.pallas_call(
        kern,
        grid=(n_blocks, t // tm),
        in_specs=[pl.BlockSpec((tm, d), lambda j, i: (i, 0)),
                  pl.BlockSpec((d, tn), lambda j, i: (0, w_block_of(j))),
                  *aux_specs],
        out_specs=pl.BlockSpec((tm, tn), lambda j, i: (i, j)),
        out_shape=jax.ShapeDtypeStruct((t, n_blocks * tn), out_dtype),
        scratch_shapes=[pltpu.VMEM((d, tn), BF16)],
        compiler_params=pltpu.CompilerParams(
            dimension_semantics=("arbitrary", "arbitrary"),
            vmem_limit_bytes=VMEM_LIMIT_BYTES),
        name=name,
    )(h, w_in, *aux)


def _band_bias():
    i = np.arange(BLK)[:, None]
    j = np.arange(2 * BLK)[None, :]
    band = (j >= i) & (j <= i + BLK)
    first = band & (j >= BLK)
    return np.where(np.stack([band, first]), 0.0, -np.inf).astype(np.float32)


class _AttentionBlocks:
    def __init__(self, bias_ref, q_ref, k_ref, v_ref, o_ref, lse_ref, kprev_ref, vprev_ref, firsts):
        self.bias_ref, self.q_ref, self.k_ref, self.v_ref = bias_ref, q_ref, k_ref, v_ref
        self.o_ref, self.lse_ref, self.kprev_ref, self.vprev_ref = o_ref, lse_ref, kprev_ref, vprev_ref
        self.firsts = firsts
        self.per = ATTN_TB // BLK
        self.scores = {}

    def _block(self, ref, n, cs):
        lo = (n % self.per) * BLK
        return ref[n // self.per, cs, lo:lo + BLK]

    def _with_previous(self, ref, prev_ref, n, cs):
        prev = prev_ref[cs, :] if n == 0 else self._block(ref, n - 1, cs)
        return jnp.concatenate([prev, self._block(ref, n, cs)], axis=1)

    def score_phase(self, n):
        first = self.firsts[n]
        bias = self.bias_ref[int(first)] if isinstance(first, bool) else self.bias_ref[jnp.where(first, 1, 0)]
        for h in range(HEADS_PER_GROUP):
            cs = slice(h * HEAD_DIM, (h + 1) * HEAD_DIM)
            kk = self._with_previous(self.k_ref, self.kprev_ref, n, cs)
            s = lax.dot_general(self._block(self.q_ref, n, cs), kk, _TN, preferred_element_type=F32)
            self.scores[n, h] = s + bias

    def value_phase(self, n):
        ts = slice(n * BLK, (n + 1) * BLK)
        lane = lax.broadcasted_iota(jnp.int32, (BLK, LANES), 1)
        m_tile = jnp.zeros((BLK, LANES), F32)
        l_tile = jnp.ones((BLK, LANES), F32)
        for h in range(HEADS_PER_GROUP):
            cs = slice(h * HEAD_DIM, (h + 1) * HEAD_DIM)
            s = self.scores.pop((n, h))
            m = jnp.max(s, axis=-1, keepdims=True)
            p = jnp.exp(s - m)
            l = jnp.sum(p, axis=-1, keepdims=True)
            vv = self._with_previous(self.v_ref, self.vprev_ref, n, cs)
            o = lax.dot_general(p.astype(BF16), vv, _NT, preferred_element_type=F32)
            self.o_ref[ts, cs] = (o / l).astype(self.o_ref.dtype)
            m_tile = jnp.where(lane == h, m, m_tile)
            l_tile = jnp.where(lane == h, l, l_tile)
        self.lse_ref[ts, :] = m_tile + jnp.log(l_tile)

    def carry(self, n_blocks):
        self.kprev_ref[...] = self._block(self.k_ref, n_blocks - 1, slice(None))
        self.vprev_ref[...] = self._block(self.v_ref, n_blocks - 1, slice(None))


def _rest_attn_kernel(h_ref, w_ref, half_bias_ref, bias_ref, q_ref, k_ref, v_ref,
                      out_ref, o_ref, lse_ref, wbf_ref, kprev_ref, vprev_ref,
                      *, tn, n_gate_blocks, steps_per_group, tiles_per_sub):
    j, i = pl.program_id(0), pl.program_id(1)
    step = j * pl.num_programs(1) + i

    @pl.when(i == 0)
    def _():
        wbf_ref[...] = w_ref[...].astype(BF16)

    @pl.when(step == 0)
    def _():
        kprev_ref[...] = jnp.zeros_like(kprev_ref)
        vprev_ref[...] = jnp.zeros_like(vprev_ref)

    n_blocks = tn // MXU_WIDTH
    per = ATTN_TB // BLK
    tiles_per_step = n_blocks // per
    group = step // steps_per_group
    tps = functools.reduce(lambda acc, g: jnp.where(group == g, tiles_per_sub[g], acc),
                           range(len(tiles_per_sub)), tiles_per_sub[0])
    tile0 = (step % steps_per_group) * tiles_per_step
    firsts = [((tile0 + n // per) & (tps - 1)) == 0 if n % per == 0 else False for n in range(n_blocks)]

    def body(epilogue):
        attn = _AttentionBlocks(bias_ref, q_ref, k_ref, v_ref, o_ref, lse_ref, kprev_ref, vprev_ref, firsts)
        for c in range(n_blocks):
            cs = slice(c * MXU_WIDTH, (c + 1) * MXU_WIDTH)
            attn.score_phase(c)
            acc = jnp.dot(h_ref[...], wbf_ref[:, cs], preferred_element_type=F32)
            if c > 0:
                attn.value_phase(c - 1)
            out_ref[:, cs] = epilogue(acc, cs, half_bias_ref).astype(out_ref.dtype)
        attn.value_phase(n_blocks - 1)
        attn.carry(n_blocks)

    pl.when(j < n_gate_blocks)(functools.partial(body, _gate_epilogue))
    pl.when(j >= n_gate_blocks)(functools.partial(body, _silu_epilogue))


def _project_rest_with_attention(h, w_in, half_bias, qkv_t, seq, *, n_gate_blocks, w_block_of):
    t, dm = h.shape
    tm, tn = PROJ_TM, PROJ_TN
    n_blocks = n_gate_blocks + 2
    rows = t // tm
    n_groups, _, n_tiles = qkv_t.shape[:3]
    blocks_per_step = tn // MXU_WIDTH
    tiles_per_step = blocks_per_step * BLK // ATTN_TB
    steps_per_group = n_tiles // tiles_per_step
    assert n_groups * steps_per_group == n_blocks * rows, "attention tiles must fill the projection steps"
    tiles_per_sub = tuple(seq // d // ATTN_TB for _, d in DILATED_GROUPS)
    step = lambda j, i: j * rows + i
    tile = lambda part: pl.BlockSpec(
        (None, None, tiles_per_step, ATTN_OUT, ATTN_TB),
        lambda j, i: (step(j, i) // steps_per_group, part, step(j, i) % steps_per_group, 0, 0))
    tokens = tiles_per_step * ATTN_TB
    per_group = lambda width: pl.BlockSpec(
        (None, tokens, width), lambda j, i: (step(j, i) // steps_per_group, step(j, i) % steps_per_group, 0))
    return pl.pallas_call(
        functools.partial(_rest_attn_kernel, tn=tn, n_gate_blocks=n_gate_blocks,
                          steps_per_group=steps_per_group, tiles_per_sub=tiles_per_sub),
        grid=(n_blocks, rows),
        in_specs=[pl.BlockSpec((tm, dm), lambda j, i: (i, 0)),
                  pl.BlockSpec((dm, tn), lambda j, i: (0, w_block_of(j))),
                  pl.BlockSpec((1, tn), lambda j, i: (0, jnp.minimum(j, n_gate_blocks - 1))),
                  pl.BlockSpec((2, BLK, 2 * BLK), lambda j, i: (0, 0, 0), pipeline_mode=pl.Buffered(1)),
                  tile(0), tile(1), tile(2)],
        out_specs=[pl.BlockSpec((tm, tn), lambda j, i: (i, j)),
                   per_group(ATTN_OUT), per_group(LANES)],
        out_shape=[jax.ShapeDtypeStruct((t, n_blocks * tn), BF16),
                   jax.ShapeDtypeStruct((n_groups, t, ATTN_OUT), BF16),
                   jax.ShapeDtypeStruct((n_groups, t, LANES), F32)],
        scratch_shapes=[pltpu.VMEM((dm, tn), BF16),
                        pltpu.VMEM((ATTN_OUT, BLK), BF16), pltpu.VMEM((ATTN_OUT, BLK), BF16)],
        compiler_params=pltpu.CompilerParams(
            dimension_semantics=("arbitrary", "arbitrary"),
            vmem_limit_bytes=VMEM_LIMIT_BYTES),
        name="proj_rest_attention",
    )(h, w_in, half_bias, jnp.asarray(_band_bias()), qkv_t, qkv_t, qkv_t)


def _rows(ref):
    return jnp.concatenate([ref[r] for r in range(ref.shape[0])], axis=0)


def _unpermute_f32(q, x):
    hi = x.astype(BF16).astype(F32)
    r1 = x - hi
    mid = r1.astype(BF16).astype(F32)
    lo = r1 - mid
    packed = hi + pltpu.roll(mid, HEADS_PER_GROUP, 1) + pltpu.roll(lo, 2 * HEADS_PER_GROUP, 1)
    y = jnp.dot(q, packed.astype(BF16), preferred_element_type=F32)
    return (y + pltpu.roll(y, LANES - HEADS_PER_GROUP, 1)) + pltpu.roll(y, LANES - 2 * HEADS_PER_GROUP, 1)


def _tail_kernel(o0_ref, o1_ref, o2_ref, l0_ref, l1_ref, l2_ref, q4_ref, q16_ref, rest_hbm_ref,
                 za_ref, zp_ref, u_ref, uh_ref, wba32_ref, wbp32_ref, pm32_ref, ps_ref,
                 out_ref, a_ref, b_ref, wba_ref, wbp_ref, pm_ref, gates_ref, gsem_ref,
                 *, tm, seq, n_tiles):
    step = pl.program_id(0)
    n_slots = gates_ref.shape[0]

    def gate_copy(tile):
        return pltpu.make_async_copy(rest_hbm_ref.at[pl.ds(tile * tm, tm), pl.ds(0, 2 * D_MODEL)],
                                     gates_ref.at[tile % n_slots], gsem_ref.at[tile % n_slots])

    @pl.when(step == 0)
    def _():
        gate_copy(0).start()
        gate_copy(1).start()
        gate_copy(0).wait()
        a_ref[...] = jnp.zeros_like(a_ref)
        b_ref[...] = jnp.zeros_like(b_ref)
        wba_ref[...] = wba32_ref[...].astype(BF16)
        wbp_ref[...] = wbp32_ref[...].astype(BF16)
        pm_ref[...] = pm32_ref[...].astype(BF16)

    pl.when((step >= 1) & (step + 1 < n_tiles))(lambda: gate_copy(step + 1).start())
    pl.when(step >= 2)(lambda: gate_copy(step - 1).wait())

    body = functools.partial(
        _tail_step, o0_ref, o1_ref, o2_ref, l0_ref, l1_ref, l2_ref, q4_ref, q16_ref,
        gates_ref.at[jnp.maximum(step - 1, 0) % n_slots],
        za_ref, zp_ref, u_ref, uh_ref, wba_ref, wbp_ref, pm_ref, ps_ref, out_ref,
        pos0=(jnp.minimum(step, n_tiles - 1) * tm) % seq, tm=tm)
    for parity in range(2):
        pl.when(step % 2 == parity)(
            functools.partial(body, a_ref.at[1 - parity], b_ref.at[1 - parity],
                              a_ref.at[parity], b_ref.at[parity]))


def _tail_step(o0_ref, o1_ref, o2_ref, l0_ref, l1_ref, l2_ref, q4_ref, q16_ref, g_ref,
               za_ref, zp_ref, u_ref, uh_ref, wba_ref, wbp_ref, pm_ref, ps_ref, out_ref,
               a_in_ref, b_in_ref, a_ref, b_ref, *, pos0, tm):
    q4, q16 = q4_ref[...], q16_ref[...]
    o0 = o0_ref[...]
    o1 = jnp.dot(q4, _rows(o1_ref), preferred_element_type=F32)
    o2 = jnp.dot(q16, _rows(o2_ref), preferred_element_type=F32)
    l0 = l0_ref[...]
    l1 = _unpermute_f32(q4, _rows(l1_ref))
    l2 = _unpermute_f32(q16, _rows(l2_ref))

    def merge_chunk(c):
        cs = slice(c * MERGE_TN, (c + 1) * MERGE_TN)
        y_attn = jnp.dot(a_in_ref[...], wba_ref[:, cs], preferred_element_type=F32)
        y_pool = jnp.dot(b_in_ref[...], wbp_ref[:, cs], preferred_element_type=F32)
        gate_attn = g_ref[:, cs].astype(F32)
        gate_pool = g_ref[:, D_MODEL + c * MERGE_TN:D_MODEL + (c + 1) * MERGE_TN].astype(F32)
        merged = gate_attn * y_attn + gate_pool * y_pool
        out_ref[:, cs] = merged.astype(out_ref.dtype)

    halo = uh_ref[...]
    halo = jnp.where(pos0 == 0, jnp.zeros_like(halo), halo)
    pos = pos0 + lax.broadcasted_iota(jnp.int32, (tm, 1), 0)
    pooled = []
    for g, ksz in enumerate(POOL_SIZES):
        cs = slice(g * POOL_GROUP, (g + 1) * POOL_GROUP)
        ext = jnp.concatenate([halo[:, cs], u_ref[:, cs]], axis=0)
        win, shift = ext, 1
        while shift < ksz:
            win = win + pltpu.roll(win, shift, 0)
            shift *= 2
        u = ext[POOL_HALO:]
        cnt = jnp.minimum(pos + 1, ksz).astype(F32)
        dlt = win[POOL_HALO:] / cnt - u
        pooled.append(jnp.dot(dlt.astype(BF16), pm_ref[g], preferred_element_type=F32))
    pool = jnp.concatenate(pooled, axis=1) * ps_ref[...]
    b_ref[...] = (pool * zp_ref[...].astype(F32)).astype(BF16)

    for c in range(D_MODEL // MERGE_TN):
        merge_chunk(c)

    mx = jnp.maximum(jnp.maximum(l0, l1), l2)
    e0, e1, e2 = jnp.exp(l0 - mx), jnp.exp(l1 - mx), jnp.exp(l2 - mx)
    inv = 1.0 / (e0 + e1 + e2)
    w0, w1, w2 = e0 * inv, e1 * inv, e2 * inv
    for h in range(HEADS_PER_GROUP):
        cs = slice(h * HEAD_DIM, (h + 1) * HEAD_DIM)
        a = (w0[:, h:h + 1] * o0[:, cs].astype(F32)
             + w1[:, h:h + 1] * o1[:, cs]
             + w2[:, h:h + 1] * o2[:, cs])
        a_ref[:, cs] = (a * za_ref[:, cs].astype(F32)).astype(BF16)


def _residual_epilogue(acc, cs, x_ref):
    return acc + x_ref[:, cs]


def _tail(os_, lses, rest, u, wba, wbp, pm, ps, batch, seq, *, za_block, zp_block):
    t, dm = u.shape[0], D_MODEL
    tm = PERM_TM
    tiles = seq // tm
    n_tiles = t // tm
    halo_per_tile = tm // POOL_HALO
    resident = functools.partial(pl.BlockSpec, pipeline_mode=pl.Buffered(1))
    cur = lambda i: jnp.minimum(i, n_tiles - 1)
    prv = lambda i: jnp.maximum(i - 1, 0)
    row = lambda blk: (lambda i: (cur(i), blk))
    row_prv = lambda blk: (lambda i: (prv(i), blk))
    unperms = [jnp.asarray(_deinterleave_matrix(d).T, BF16) for _, d in DILATED_GROUPS[1:]]
    o_in, l_in, o_specs, l_specs = [], [], [], []
    for g, (_, d) in enumerate(DILATED_GROUPS):
        for arr, width, ins, specs in ((os_, ATTN_OUT, o_in, o_specs), (lses, LANES, l_in, l_specs)):
            if d == 1:
                ins.append(arr)
                specs.append(pl.BlockSpec((None, tm, width), lambda i, g=g: (g, cur(i), 0)))
            else:
                ins.append(arr.reshape(arr.shape[0], batch, d, seq // d, width))
                specs.append(pl.BlockSpec((None, None, d, tm // d, width),
                                          lambda i, g=g: (g, cur(i) // tiles, 0, cur(i) % tiles, 0)))
    in_specs = (
        o_specs + l_specs
        + [resident((tm, tm), lambda i: (0, 0)), resident((tm, tm), lambda i: (0, 0)),
           pl.BlockSpec(memory_space=pl.ANY),
           pl.BlockSpec((tm, ATTN_OUT), row(za_block)),
           pl.BlockSpec((tm, POOL_WIDTH), row(zp_block)),
           pl.BlockSpec((tm, POOL_WIDTH), row(0)),
           pl.BlockSpec((POOL_HALO, POOL_WIDTH),
                        lambda i: (jnp.maximum(cur(i) * halo_per_tile - 1, 0), 0)),
           resident(wba.shape, lambda i: (0, 0)),
           resident(wbp.shape, lambda i: (0, 0)),
           resident(pm.shape, lambda i: (0, 0, 0)),
           resident(ps.shape, lambda i: (0, 0))])
    return pl.pallas_call(
        functools.partial(_tail_kernel, tm=tm, seq=seq, n_tiles=n_tiles),
        grid=(n_tiles + 1,),
        in_specs=in_specs,
        out_specs=pl.BlockSpec((tm, dm), row_prv(0)),
        out_shape=jax.ShapeDtypeStruct((t, dm), BF16),
        scratch_shapes=[pltpu.VMEM((2, tm, ATTN_OUT), BF16), pltpu.VMEM((2, tm, POOL_WIDTH), BF16),
                        pltpu.VMEM(wba.shape, BF16), pltpu.VMEM(wbp.shape, BF16), pltpu.VMEM(pm.shape, BF16),
                        pltpu.VMEM((GATE_RING_SLOTS, tm, 2 * D_MODEL), BF16),
                        pltpu.SemaphoreType.DMA((GATE_RING_SLOTS,))],
        compiler_params=pltpu.CompilerParams(
            dimension_semantics=("arbitrary",),
            vmem_limit_bytes=VMEM_LIMIT_BYTES),
        name="merge_pool",
    )(*o_in, *l_in, *unperms, rest, rest, rest, u, u, wba, wbp, pm, ps)


def kernel(x, norm_gain, w_in, b_gates, q_norm_gain, k_norm_gain, pool_maps, pool_scale,
           w_branch_attn, w_branch_pool, w_out):
    batch, seq, dm = x.shape
    t = batch * seq
    x2 = x.reshape(t, dm)
    hs, u = _rmsnorm_and_project_u(x2, norm_gain, w_in, batch, seq, w_block=10)

    lane_rep = lambda g: jnp.broadcast_to(g.astype(F32)[:, None], (HEAD_DIM, LANES))
    gains = jnp.stack([lane_rep(q_norm_gain * HEAD_DIM ** -0.5), lane_rep(k_norm_gain)])
    tables = [_rope_tables_t(seq, d) for _, d in DILATED_GROUPS]
    qkv_t = _project_qkv(hs, w_in, gains, jnp.stack([c for c, _ in tables]),
                         jnp.stack([s for _, s in tables]), seq)

    n_gate_blocks = 2 * D_MODEL // PROJ_TN
    half_bias = (0.5 * b_gates.astype(F32)).reshape(1, 2 * D_MODEL)
    rest_w_block = lambda j: jnp.where(j < 4, j + 12, jnp.where(j == 4, 9, 11))
    rest, o_all, lse_all = _project_rest_with_attention(
        hs[0], w_in, half_bias, qkv_t, seq, n_gate_blocks=n_gate_blocks, w_block_of=rest_w_block)

    merged = _tail(o_all, lse_all, rest, u,
                   w_branch_attn, w_branch_pool, pool_maps,
                   pool_scale.astype(F32).reshape(1, POOL_WIDTH), batch, seq,
                   za_block=4, zp_block=5)
    out = _project(merged, w_out, n_blocks=dm // PROJ_TN, w_block_of=lambda j: j,
                   epilogues=((0, _residual_epilogue),), out_dtype=F32, aux=(x2,),
                   aux_specs=(pl.BlockSpec((PROJ_TM, PROJ_TN), lambda j, i: (i, j)),),
                   name="proj_out")
    return out.reshape(batch, seq, dm)
```

```python
import functools

import numpy as np
import jax
import jax.numpy as jnp
from jax import lax
from jax.experimental import pallas as pl
from jax.experimental.pallas import tpu as pltpu

D_MODEL = 2048
HEAD_DIM = 128
HEADS_PER_GROUP = 8
DILATED_GROUPS = ((128, 1), (512, 4), (2048, 16))
N_ATTN_GROUPS = len(DILATED_GROUPS)
ATTN_OUT = HEADS_PER_GROUP * HEAD_DIM
BLK = 128
ROPE_THETA = 500000.0
ROT_DIM = HEAD_DIM // 4
ROT_HALF = ROT_DIM // 2
POOL_SIZES = (2, 4, 8, 16)
POOL_WIDTH = D_MODEL // 2
POOL_GROUP = POOL_WIDTH // len(POOL_SIZES)
NORM_EPS = 1e-6

LANES = 128
MXU_WIDTH = 256
VMEM_LIMIT_BYTES = 56 * 1024 * 1024

PROJ_TM = 1024
PROJ_TN = 1024
QKV_TM = 2048
PERM_TM = 256
NORM_TM = 512
ATTN_TB = 256
MERGE_TN = 512
POOL_HALO = 16

F32 = jnp.float32
BF16 = jnp.bfloat16

_NT = (((1,), (1,)), ((), ()))
_TN = (((0,), (0,)), ((), ()))


def _deinterleave_matrix(d):
    n = PERM_TM // d
    i = np.arange(PERM_TM)
    p = np.zeros((PERM_TM, PERM_TM), np.float32)
    p[i, (i % n) * d + i // n] = 1.0
    return p


def _norm_u_kernel(x_ref, g_ref, p4_ref, p16_ref, w_ref, h1_ref, h4_ref, h16_ref, u_ref,
                   hbuf_ref, wbf_ref):
    step = pl.program_id(0)

    @pl.when(step == 0)
    def _():
        wbf_ref[...] = w_ref[...].astype(BF16)
        hbuf_ref[...] = jnp.zeros_like(hbuf_ref)

    def body(h_in_ref, h_out_ref):
        for c in range(u_ref.shape[1] // MXU_WIDTH):
            cs = slice(c * MXU_WIDTH, (c + 1) * MXU_WIDTH)
            u_ref[:, cs] = jnp.dot(h_in_ref[...], wbf_ref[:, cs], preferred_element_type=F32)
        for sb in range(x_ref.shape[0] // PERM_TM):
            rs = slice(sb * PERM_TM, (sb + 1) * PERM_TM)
            x = x_ref[rs, :]
            ms = jnp.mean(x * x, axis=-1, keepdims=True)
            h = (x * lax.rsqrt(ms + NORM_EPS) * g_ref[...]).astype(BF16)
            h1_ref[rs, :] = h
            h_out_ref[rs, :] = h
            for p_ref, out_ref in ((p4_ref, h4_ref), (p16_ref, h16_ref)):
                d = out_ref.shape[0]
                n = PERM_TM // d
                hp = jnp.dot(p_ref[...], h, preferred_element_type=F32).astype(BF16)
                for r in range(d):
                    out_ref[r, sb * n:(sb + 1) * n, :] = hp[r * n:(r + 1) * n, :]

    for parity in range(2):
        pl.when(step % 2 == parity)(
            functools.partial(body, hbuf_ref.at[1 - parity], hbuf_ref.at[parity]))


def _rmsnorm_and_project_u(x2, gain, w_in, batch, seq, *, w_block):
    t, dm = x2.shape
    tm, tn = NORM_TM, PROJ_TN
    tiles = seq // tm
    n_tiles = t // tm
    perms = [jnp.asarray(_deinterleave_matrix(d), BF16) for _, d in DILATED_GROUPS[1:]]
    cur = lambda i: jnp.minimum(i, n_tiles - 1)
    prv = lambda i: jnp.maximum(i - 1, 0)
    out_shape = [jax.ShapeDtypeStruct((t, dm), BF16)]
    out_specs = [pl.BlockSpec((tm, dm), lambda i: (cur(i), 0))]
    for _, d in DILATED_GROUPS[1:]:
        out_shape.append(jax.ShapeDtypeStruct((batch, d, seq // d, dm), BF16))
        out_specs.append(pl.BlockSpec((None, d, tm // d, dm),
                                      lambda i: (cur(i) // tiles, 0, cur(i) % tiles, 0)))
    out_shape.append(jax.ShapeDtypeStruct((t, tn), F32))
    out_specs.append(pl.BlockSpec((tm, tn), lambda i: (prv(i), 0)))
    const = lambda i: (0, 0)
    resident = functools.partial(pl.BlockSpec, pipeline_mode=pl.Buffered(1))
    h1, h4, h16, u = pl.pallas_call(
        _norm_u_kernel,
        grid=(n_tiles + 1,),
        in_specs=[pl.BlockSpec((tm, dm), lambda i: (cur(i), 0)),
                  pl.BlockSpec((1, dm), const),
                  pl.BlockSpec((PERM_TM, PERM_TM), const),
                  pl.BlockSpec((PERM_TM, PERM_TM), const),
                  resident((dm, tn), lambda i: (0, w_block))],
        out_specs=out_specs,
        out_shape=out_shape,
        scratch_shapes=[pltpu.VMEM((2, tm, dm), BF16), pltpu.VMEM((dm, tn), BF16)],
        compiler_params=pltpu.CompilerParams(dimension_semantics=("arbitrary",),
                                             vmem_limit_bytes=VMEM_LIMIT_BYTES),
        name="rmsnorm_proj_u",
    )(x2, gain.reshape(1, dm), *perms, w_in)
    return (h1, h4.reshape(t, dm), h16.reshape(t, dm)), u


def _qkv_kernel(*refs, tm, tn):
    h_refs = refs[:N_ATTN_GROUPS]
    w_ref, gain_ref, cos_ref, sin_ref, out_ref, wt_ref, hbuf_ref, sem_ref = refs[N_ATTN_GROUPS:]
    rows = pl.num_programs(1)
    part = pl.program_id(0) % 3
    step = pl.program_id(0) * rows + pl.program_id(1)
    slot = step % 2

    def tile_copy(g, row_block, into):
        return pltpu.make_async_copy(h_refs[g].at[pl.ds(row_block * tm, tm), :],
                                     hbuf_ref.at[into], sem_ref.at[into])

    def start_fetch(of_step, into):
        for g in range(N_ATTN_GROUPS):
            pl.when(of_step // (3 * rows) == g)(lambda g=g: tile_copy(g, of_step % rows, into).start())

    pl.when(step == 0)(lambda: start_fetch(step, slot))
    pl.when(step + 1 < pl.num_programs(0) * rows)(lambda: start_fetch(step + 1, 1 - slot))
    tile_copy(0, 0, slot).wait()

    @pl.when(pl.program_id(1) == 0)
    def _():
        for c in range(tn // MXU_WIDTH):
            cs = slice(c * MXU_WIDTH, (c + 1) * MXU_WIDTH)
            wt_ref[cs, :] = w_ref[:, cs].T.astype(BF16)

    body = functools.partial(_qkv_body, hbuf_ref.at[slot], gain_ref, cos_ref, sin_ref, out_ref, wt_ref,
                             tm=tm, tn=tn)
    pl.when(part < 2)(functools.partial(body, normed=True))
    pl.when(part == 2)(functools.partial(body, normed=False))


def _qkv_body(h_ref, gain_ref, cos_ref, sin_ref, out_ref, wt_ref, *, tm, tn, normed):
    def chunk(c):
        ts = slice(c * ATTN_TB, (c + 1) * ATTN_TB)
        return lax.dot_general(wt_ref[...], h_ref[ts, :], _NT, preferred_element_type=F32)

    if not normed:
        for c in range(tm // ATTN_TB):
            out_ref[c] = chunk(c).astype(out_ref.dtype)
    else:
        for c in range(tm // ATTN_TB):
            acc = chunk(c)
            for hh in range(tn // HEAD_DIM):
                hs = slice(hh * HEAD_DIM, (hh + 1) * HEAD_DIM)
                for lt in range(ATTN_TB // LANES):
                    ls = slice(lt * LANES, (lt + 1) * LANES)
                    pos = slice(c * ATTN_TB + lt * LANES, c * ATTN_TB + (lt + 1) * LANES)
                    a = acc[hs, ls]
                    ss = jnp.sum(a * a, axis=0, keepdims=True)
                    r = lax.rsqrt(ss * (1.0 / HEAD_DIM) + NORM_EPS)
                    ag = a * gain_ref[...]
                    lo, hi = ag[0:ROT_HALF], ag[ROT_HALF:ROT_DIM]
                    cos, sin = cos_ref[:, pos], sin_ref[:, pos]
                    y = jnp.concatenate([lo * cos - hi * sin, hi * cos + lo * sin, ag[ROT_DIM:]], axis=0)
                    out_ref[c, hs, ls] = (y * r).astype(out_ref.dtype)


def _project_qkv(hs, w_in, gains, cos_t, sin_t, seq):
    t, dm = hs[0].shape
    tm, tn = QKV_TM, PROJ_TN
    seq_tiles = seq // tm
    group = lambda gj: gj // 3
    part = lambda gj: gj % 3
    table = pl.BlockSpec((None, ROT_HALF, tm), lambda gj, i: (group(gj), 0, i % seq_tiles))
    return pl.pallas_call(
        functools.partial(_qkv_kernel, tm=tm, tn=tn),
        grid=(3 * N_ATTN_GROUPS, t // tm),
        in_specs=[*(pl.BlockSpec(memory_space=pl.ANY) for _ in range(N_ATTN_GROUPS)),
                  pl.BlockSpec((dm, tn), lambda gj, i: (0, part(gj) * N_ATTN_GROUPS + group(gj))),
                  pl.BlockSpec((None, HEAD_DIM, LANES), lambda gj, i: (jnp.minimum(part(gj), 1), 0, 0)),
                  table, table],
        out_specs=pl.BlockSpec((None, None, tm // ATTN_TB, tn, ATTN_TB),
                               lambda gj, i: (group(gj), part(gj), i, 0, 0)),
        out_shape=jax.ShapeDtypeStruct((N_ATTN_GROUPS, 3, t // ATTN_TB, tn, ATTN_TB), BF16),
        scratch_shapes=[pltpu.VMEM((tn, dm), BF16), pltpu.VMEM((2, tm, dm), BF16),
                        pltpu.SemaphoreType.DMA((2,))],
        compiler_params=pltpu.CompilerParams(
            dimension_semantics=("arbitrary", "arbitrary"),
            vmem_limit_bytes=VMEM_LIMIT_BYTES),
        name="proj_qkv",
    )(*hs, w_in, gains, cos_t, sin_t)


def _rope_tables_t(seq, d):
    inv_freq = ROPE_THETA ** (-np.arange(0, ROT_DIM, 2, dtype=np.float64) / ROT_DIM)
    pos = np.arange(seq).reshape(seq // d, d).T.reshape(seq)
    ang = pos.astype(np.float64)[None, :] * inv_freq[:, None]
    return np.cos(ang).astype(np.float32), np.sin(ang).astype(np.float32)


def _gate_epilogue(acc, cs, half_bias_ref):
    return 0.5 + 0.5 * jnp.tanh(0.5 * acc + half_bias_ref[:, cs])


def _silu_epilogue(acc, cs, half_bias_ref):
    del cs, half_bias_ref
    half = 0.5 * acc
    return half + half * jnp.tanh(half)


def _proj_kernel(*refs, epilogues, n_aux, tn):
    h_ref, w_ref = refs[0], refs[1]
    aux = refs[2:2 + n_aux]
    out_ref = refs[2 + n_aux]
    wbf_ref = refs[3 + n_aux]
    j = pl.program_id(0)

    @pl.when(pl.program_id(1) == 0)
    def _():
        wbf_ref[...] = w_ref[...].astype(BF16)

    def body(epilogue):
        for c in range(tn // MXU_WIDTH):
            cs = slice(c * MXU_WIDTH, (c + 1) * MXU_WIDTH)
            acc = jnp.dot(h_ref[...], wbf_ref[:, cs], preferred_element_type=F32)
            out_ref[:, cs] = epilogue(acc, cs, *aux).astype(out_ref.dtype)

    if len(epilogues) == 1:
        body(epilogues[0][1])
    else:
        bounds = [first for first, _ in epilogues[1:]] + [None]
        for (first, fn), last in zip(epilogues, bounds):
            cond = (j >= first) if last is None else ((j >= first) & (j < last))
            pl.when(cond)(functools.partial(body, fn))


def _project(h, w_in, *, n_blocks, w_block_of, epilogues, out_dtype, aux=(), aux_specs=(), name):
    t, d = h.shape
    tm, tn = PROJ_TM, PROJ_TN
    kern = functools.partial(_proj_kernel, epilogues=epilogues, n_aux=len(aux), tn=tn)
    return pl.pallas_call(
        kern,
        grid=(n_blocks, t // tm),
        in_specs=[pl.BlockSpec((tm, d), lambda j, i: (i, 0)),
                  pl.BlockSpec((d, tn), lambda j, i: (0, w_block_of(j))),
                  *aux_specs],
        out_specs=pl.BlockSpec((tm, tn), lambda j, i: (i, j)),
        out_shape=jax.ShapeDtypeStruct((t, n_blocks * tn), out_dtype),
        scratch_shapes=[pltpu.VMEM((d, tn), BF16)],
        compiler_params=pltpu.CompilerParams(
            dimension_semantics=("arbitrary", "arbitrary"),
            vmem_limit_bytes=VMEM_LIMIT_BYTES),
        name=name,
    )(h, w_in, *aux)


def _band_bias():
    i = np.arange(BLK)[:, None]
    j = np.arange(2 * BLK)[None, :]
    band = (j >= i) & (j <= i + BLK)
    first = band & (j >= BLK)
    return np.where(np.stack([band, first]), 0.0, -np.inf).astype(np.float32)


class _AttentionBlocks:
    def __init__(self, bias_ref, q_ref, k_ref, v_ref, o_ref, lse_ref, kprev_ref, vprev_ref, firsts):
        self.bias_ref, self.q_ref, self.k_ref, self.v_ref = bias_ref, q_ref, k_ref, v_ref
        self.o_ref, self.lse_ref, self.kprev_ref, self.vprev_ref = o_ref, lse_ref, kprev_ref, vprev_ref
        self.firsts = firsts
        self.per = ATTN_TB // BLK
        self.scores = {}

    def _block(self, ref, n, cs):
        lo = (n % self.per) * BLK
        return ref[n // self.per, cs, lo:lo + BLK]

    def _with_previous(self, ref, prev_ref, n, cs):
        prev = prev_ref[cs, :] if n == 0 else self._block(ref, n - 1, cs)
        return jnp.concatenate([prev, self._block(ref, n, cs)], axis=1)

    def score_phase(self, n):
        first = self.firsts[n]
        bias = self.bias_ref[int(first)] if isinstance(first, bool) else self.bias_ref[jnp.where(first, 1, 0)]
        for h in range(HEADS_PER_GROUP):
            cs = slice(h * HEAD_DIM, (h + 1) * HEAD_DIM)
            kk = self._with_previous(self.k_ref, self.kprev_ref, n, cs)
            s = lax.dot_general(self._block(self.q_ref, n, cs), kk, _TN, preferred_element_type=F32)
            self.scores[n, h] = s + bias

    def value_phase(self, n):
        ts = slice(n * BLK, (n + 1) * BLK)
        lane = lax.broadcasted_iota(jnp.int32, (BLK, LANES), 1)
        m_tile = jnp.zeros((BLK, LANES), F32)
        l_tile = jnp.ones((BLK, LANES), F32)
        for h in range(HEADS_PER_GROUP):
            cs = slice(h * HEAD_DIM, (h + 1) * HEAD_DIM)
            s = self.scores.pop((n, h))
            m = jnp.max(s, axis=-1, keepdims=True)
            p = jnp.exp(s - m)
            l = jnp.sum(p, axis=-1, keepdims=True)
            vv = self._with_previous(self.v_ref, self.vprev_ref, n, cs)
            o = lax.dot_general(p.astype(BF16), vv, _NT, preferred_element_type=F32)
            self.o_ref[ts, cs] = (o / l).astype(self.o_ref.dtype)
            m_tile = jnp.where(lane == h, m, m_tile)
            l_tile = jnp.where(lane == h, l, l_tile)
        self.lse_ref[ts, :] = m_tile + jnp.log(l_tile)

    def carry(self, n_blocks):
        self.kprev_ref[...] = self._block(self.k_ref, n_blocks - 1, slice(None))
        self.vprev_ref[...] = self._block(self.v_ref, n_blocks - 1, slice(None))


def _rest_attn_kernel(h_ref, w_ref, half_bias_ref, bias_ref, q_ref, k_ref, v_ref,
                      out_ref, o_ref, lse_ref, wbf_ref, kprev_ref, vprev_ref,
                      *, tn, n_gate_blocks, steps_per_group, tiles_per_sub):
    j, i = pl.program_id(0), pl.program_id(1)
    step = j * pl.num_programs(1) + i

    @pl.when(i == 0)
    def _():
        wbf_ref[...] = w_ref[...].astype(BF16)

    @pl.when(step == 0)
    def _():
        kprev_ref[...] = jnp.zeros_like(kprev_ref)
        vprev_ref[...] = jnp.zeros_like(vprev_ref)

    n_blocks = tn // MXU_WIDTH
    per = ATTN_TB // BLK
    tiles_per_step = n_blocks // per
    group = step // steps_per_group
    tps = functools.reduce(lambda acc, g: jnp.where(group == g, tiles_per_sub[g], acc),
                           range(len(tiles_per_sub)), tiles_per_sub[0])
    tile0 = (step % steps_per_group) * tiles_per_step
    firsts = [((tile0 + n // per) & (tps - 1)) == 0 if n % per == 0 else False for n in range(n_blocks)]

    def body(epilogue):
        attn = _AttentionBlocks(bias_ref, q_ref, k_ref, v_ref, o_ref, lse_ref, kprev_ref, vprev_ref, firsts)
        for c in range(n_blocks):
            cs = slice(c * MXU_WIDTH, (c + 1) * MXU_WIDTH)
            attn.score_phase(c)
            acc = jnp.dot(h_ref[...], wbf_ref[:, cs], preferred_element_type=F32)
            if c > 0:
                attn.value_phase(c - 1)
            out_ref[:, cs] = epilogue(acc, cs, half_bias_ref).astype(out_ref.dtype)
        attn.value_phase(n_blocks - 1)
        attn.carry(n_blocks)

    pl.when(j < n_gate_blocks)(functools.partial(body, _gate_epilogue))
    pl.when(j >= n_gate_blocks)(functools.partial(body, _silu_epilogue))


def _project_rest_with_attention(h, w_in, half_bias, qkv_t, seq, *, n_gate_blocks, w_block_of):
    t, dm = h.shape
    tm, tn = PROJ_TM, PROJ_TN
    n_blocks = n_gate_blocks + 2
    rows = t // tm
    n_groups, _, n_tiles = qkv_t.shape[:3]
    blocks_per_step = tn // MXU_WIDTH
    tiles_per_step = blocks_per_step * BLK // ATTN_TB
    steps_per_group = n_tiles // tiles_per_step
    assert n_groups * steps_per_group == n_blocks * rows, "attention tiles must fill the projection steps"
    tiles_per_sub = tuple(seq // d // ATTN_TB for _, d in DILATED_GROUPS)
    step = lambda j, i: j * rows + i
    tile = lambda part: pl.BlockSpec(
        (None, None, tiles_per_step, ATTN_OUT, ATTN_TB),
        lambda j, i: (step(j, i) // steps_per_group, part, step(j, i) % steps_per_group, 0, 0))
    tokens = tiles_per_step * ATTN_TB
    per_group = lambda width: pl.BlockSpec(
        (None, tokens, width), lambda j, i: (step(j, i) // steps_per_group, step(j, i) % steps_per_group, 0))
    return pl.pallas_call(
        functools.partial(_rest_attn_kernel, tn=tn, n_gate_blocks=n_gate_blocks,
                          steps_per_group=steps_per_group, tiles_per_sub=tiles_per_sub),
        grid=(n_blocks, rows),
        in_specs=[pl.BlockSpec((tm, dm), lambda j, i: (i, 0)),
                  pl.BlockSpec((dm, tn), lambda j, i: (0, w_block_of(j))),
                  pl.BlockSpec((1, tn), lambda j, i: (0, jnp.minimum(j, n_gate_blocks - 1))),
                  pl.BlockSpec((2, BLK, 2 * BLK), lambda j, i: (0, 0, 0), pipeline_mode=pl.Buffered(1)),
                  tile(0), tile(1), tile(2)],
        out_specs=[pl.BlockSpec((tm, tn), lambda j, i: (i, j)),
                   per_group(ATTN_OUT), per_group(LANES)],
        out_shape=[jax.ShapeDtypeStruct((t, n_blocks * tn), BF16),
                   jax.ShapeDtypeStruct((n_groups, t, ATTN_OUT), BF16),
                   jax.ShapeDtypeStruct((n_groups, t, LANES), F32)],
        scratch_shapes=[pltpu.VMEM((dm, tn), BF16),
                        pltpu.VMEM((ATTN_OUT, BLK), BF16), pltpu.VMEM((ATTN_OUT, BLK), BF16)],
        compiler_params=pltpu.CompilerParams(
            dimension_semantics=("arbitrary", "arbitrary"),
            vmem_limit_bytes=VMEM_LIMIT_BYTES),
        name="proj_rest_attention",
    )(h, w_in, half_bias, jnp.asarray(_band_bias()), qkv_t, qkv_t, qkv_t)


def _rows(ref):
    return jnp.concatenate([ref[r] for r in range(ref.shape[0])], axis=0)


def _unpermute_f32(q, x):
    hi = x.astype(BF16).astype(F32)
    r1 = x - hi
    mid = r1.astype(BF16).astype(F32)
    lo = r1 - mid
    packed = hi + pltpu.roll(mid, HEADS_PER_GROUP, 1) + pltpu.roll(lo, 2 * HEADS_PER_GROUP, 1)
    y = jnp.dot(q, packed.astype(BF16), preferred_element_type=F32)
    return (y + pltpu.roll(y, LANES - HEADS_PER_GROUP, 1)) + pltpu.roll(y, LANES - 2 * HEADS_PER_GROUP, 1)


def _tail_kernel(o0_ref, o1_ref, o2_ref, l0_ref, l1_ref, l2_ref, q4_ref, q16_ref, ga_ref, gp_ref,
                 za_ref, zp_ref, u_ref, uh_ref, wba32_ref, wbp32_ref, pm32_ref, ps_ref,
                 out_ref, a_ref, b_ref, wba_ref, wbp_ref, pm_ref, *, tm, seq, n_tiles):
    step = pl.program_id(0)

    @pl.when(step == 0)
    def _():
        a_ref[...] = jnp.zeros_like(a_ref)
        b_ref[...] = jnp.zeros_like(b_ref)
        wba_ref[...] = wba32_ref[...].astype(BF16)
        wbp_ref[...] = wbp32_ref[...].astype(BF16)
        pm_ref[...] = pm32_ref[...].astype(BF16)

    body = functools.partial(
        _tail_step, o0_ref, o1_ref, o2_ref, l0_ref, l1_ref, l2_ref, q4_ref, q16_ref, ga_ref, gp_ref,
        za_ref, zp_ref, u_ref, uh_ref, wba_ref, wbp_ref, pm_ref, ps_ref, out_ref,
        pos0=(jnp.minimum(step, n_tiles - 1) * tm) % seq, tm=tm)
    for parity in range(2):
        pl.when(step % 2 == parity)(
            functools.partial(body, a_ref.at[1 - parity], b_ref.at[1 - parity],
                              a_ref.at[parity], b_ref.at[parity]))


def _tail_step(o0_ref, o1_ref, o2_ref, l0_ref, l1_ref, l2_ref, q4_ref, q16_ref, ga_ref, gp_ref,
               za_ref, zp_ref, u_ref, uh_ref, wba_ref, wbp_ref, pm_ref, ps_ref, out_ref,
               a_in_ref, b_in_ref, a_ref, b_ref, *, pos0, tm):
    q4, q16 = q4_ref[...], q16_ref[...]
    o0 = o0_ref[...]
    o1 = jnp.dot(q4, _rows(o1_ref), preferred_element_type=F32)
    o2 = jnp.dot(q16, _rows(o2_ref), preferred_element_type=F32)
    l0 = l0_ref[...]
    l1 = _unpermute_f32(q4, _rows(l1_ref))
    l2 = _unpermute_f32(q16, _rows(l2_ref))

    def merge_chunk(c):
        cs = slice(c * MERGE_TN, (c + 1) * MERGE_TN)
        y_attn = jnp.dot(a_in_ref[...], wba_ref[:, cs], preferred_element_type=F32)
        y_pool = jnp.dot(b_in_ref[...], wbp_ref[:, cs], preferred_element_type=F32)
        merged = ga_ref[:, cs].astype(F32) * y_attn + gp_ref[:, cs].astype(F32) * y_pool
        out_ref[:, cs] = merged.astype(out_ref.dtype)

    halo = uh_ref[...]
    halo = jnp.where(pos0 == 0, jnp.zeros_like(halo), halo)
    pos = pos0 + lax.broadcasted_iota(jnp.int32, (tm, 1), 0)
    pooled = []
    for g, ksz in enumerate(POOL_SIZES):
        cs = slice(g * POOL_GROUP, (g + 1) * POOL_GROUP)
        ext = jnp.concatenate([halo[:, cs], u_ref[:, cs]], axis=0)
        win, shift = ext, 1
        while shift < ksz:
            win = win + pltpu.roll(win, shift, 0)
            shift *= 2
        u = ext[POOL_HALO:]
        cnt = jnp.minimum(pos + 1, ksz).astype(F32)
        dlt = win[POOL_HALO:] / cnt - u
        pooled.append(jnp.dot(dlt.astype(BF16), pm_ref[g], preferred_element_type=F32))
    pool = jnp.concatenate(pooled, axis=1) * ps_ref[...]
    b_ref[...] = (pool * zp_ref[...].astype(F32)).astype(BF16)

    for c in range(D_MODEL // MERGE_TN):
        merge_chunk(c)

    mx = jnp.maximum(jnp.maximum(l0, l1), l2)
    e0, e1, e2 = jnp.exp(l0 - mx), jnp.exp(l1 - mx), jnp.exp(l2 - mx)
    inv = 1.0 / (e0 + e1 + e2)
    w0, w1, w2 = e0 * inv, e1 * inv, e2 * inv
    for h in range(HEADS_PER_GROUP):
        cs = slice(h * HEAD_DIM, (h + 1) * HEAD_DIM)
        a = (w0[:, h:h + 1] * o0[:, cs].astype(F32)
             + w1[:, h:h + 1] * o1[:, cs]
             + w2[:, h:h + 1] * o2[:, cs])
        a_ref[:, cs] = (a * za_ref[:, cs].astype(F32)).astype(BF16)


def _residual_epilogue(acc, cs, x_ref):
    return acc + x_ref[:, cs]


def _tail(os_, lses, rest, u, wba, wbp, pm, ps, batch, seq, *, gate_block0, za_block, zp_block):
    t, dm = u.shape[0], D_MODEL
    tm = PERM_TM
    tiles = seq // tm
    n_tiles = t // tm
    halo_per_tile = tm // POOL_HALO
    resident = functools.partial(pl.BlockSpec, pipeline_mode=pl.Buffered(1))
    cur = lambda i: jnp.minimum(i, n_tiles - 1)
    prv = lambda i: jnp.maximum(i - 1, 0)
    row = lambda blk: (lambda i: (cur(i), blk))
    row_prv = lambda blk: (lambda i: (prv(i), blk))
    gate_blocks = D_MODEL // ATTN_OUT
    unperms = [jnp.asarray(_deinterleave_matrix(d).T, BF16) for _, d in DILATED_GROUPS[1:]]
    o_in, l_in, o_specs, l_specs = [], [], [], []
    for g, (_, d) in enumerate(DILATED_GROUPS):
        for arr, width, ins, specs in ((os_, ATTN_OUT, o_in, o_specs), (lses, LANES, l_in, l_specs)):
            if d == 1:
                ins.append(arr)
                specs.append(pl.BlockSpec((None, tm, width), lambda i, g=g: (g, cur(i), 0)))
            else:
                ins.append(arr.reshape(arr.shape[0], batch, d, seq // d, width))
                specs.append(pl.BlockSpec((None, None, d, tm // d, width),
                                          lambda i, g=g: (g, cur(i) // tiles, 0, cur(i) % tiles, 0)))
    in_specs = (
        o_specs + l_specs
        + [resident((tm, tm), lambda i: (0, 0)), resident((tm, tm), lambda i: (0, 0)),
           pl.BlockSpec((tm, D_MODEL), row_prv(gate_block0 // gate_blocks)),
           pl.BlockSpec((tm, D_MODEL), row_prv(gate_block0 // gate_blocks + 1)),
           pl.BlockSpec((tm, ATTN_OUT), row(za_block)),
           pl.BlockSpec((tm, POOL_WIDTH), row(zp_block)),
           pl.BlockSpec((tm, POOL_WIDTH), row(0)),
           pl.BlockSpec((POOL_HALO, POOL_WIDTH),
                        lambda i: (jnp.maximum(cur(i) * halo_per_tile - 1, 0), 0)),
           resident(wba.shape, lambda i: (0, 0)),
           resident(wbp.shape, lambda i: (0, 0)),
           resident(pm.shape, lambda i: (0, 0, 0)),
           resident(ps.shape, lambda i: (0, 0))])
    return pl.pallas_call(
        functools.partial(_tail_kernel, tm=tm, seq=seq, n_tiles=n_tiles),
        grid=(n_tiles + 1,),
        in_specs=in_specs,
        out_specs=pl.BlockSpec((tm, dm), row_prv(0)),
        out_shape=jax.ShapeDtypeStruct((t, dm), BF16),
        scratch_shapes=[pltpu.VMEM((2, tm, ATTN_OUT), BF16), pltpu.VMEM((2, tm, POOL_WIDTH), BF16),
                        pltpu.VMEM(wba.shape, BF16), pltpu.VMEM(wbp.shape, BF16), pltpu.VMEM(pm.shape, BF16)],
        compiler_params=pltpu.CompilerParams(
            dimension_semantics=("arbitrary",),
            vmem_limit_bytes=VMEM_LIMIT_BYTES),
        name="merge_pool",
    )(*o_in, *l_in, *unperms, rest, rest, rest, rest, u, u, wba, wbp, pm, ps)


def kernel(x, norm_gain, w_in, b_gates, q_norm_gain, k_norm_gain, pool_maps, pool_scale,
           w_branch_attn, w_branch_pool, w_out):
    batch, seq, dm = x.shape
    t = batch * seq
    x2 = x.reshape(t, dm)
    hs, u = _rmsnorm_and_project_u(x2, norm_gain, w_in, batch, seq, w_block=10)

    lane_rep = lambda g: jnp.broadcast_to(g.astype(F32)[:, None], (HEAD_DIM, LANES))
    gains = jnp.stack([lane_rep(q_norm_gain * HEAD_DIM ** -0.5), lane_rep(k_norm_gain)])
    tables = [_rope_tables_t(seq, d) for _, d in DILATED_GROUPS]
    qkv_t = _project_qkv(hs, w_in, gains, jnp.stack([c for c, _ in tables]),
                         jnp.stack([s for _, s in tables]), seq)

    n_gate_blocks = 2 * D_MODEL // PROJ_TN
    half_bias = (0.5 * b_gates.astype(F32)).reshape(1, 2 * D_MODEL)
    rest_w_block = lambda j: jnp.where(j < 4, j + 12, jnp.where(j == 4, 9, 11))
    rest, o_all, lse_all = _project_rest_with_attention(
        hs[0], w_in, half_bias, qkv_t, seq, n_gate_blocks=n_gate_blocks, w_block_of=rest_w_block)

    merged = _tail(o_all, lse_all, rest, u,
                   w_branch_attn, w_branch_pool, pool_maps,
                   pool_scale.astype(F32).reshape(1, POOL_WIDTH), batch, seq,
                   gate_block0=0, za_block=4, zp_block=5)
    out = _project(merged, w_out, n_blocks=dm // PROJ_TN, w_block_of=lambda j: j,
                   epilogues=((0, _residual_epilogue),), out_dtype=F32, aux=(x2,),
                   aux_specs=(pl.BlockSpec((PROJ_TM, PROJ_TN), lambda j, i: (i, j)),),
                   name="proj_out")
    return out.reshape(batch, seq, dm)
```

```python
import functools

import numpy as np
import jax
import jax.numpy as jnp
from jax import lax
from jax.experimental import pallas as pl
from jax.experimental.pallas import tpu as pltpu

D_MODEL = 2048
HEAD_DIM = 128
HEADS_PER_GROUP = 8
DILATED_GROUPS = ((128, 1), (512, 4), (2048, 16))
N_ATTN_GROUPS = len(DILATED_GROUPS)
ATTN_OUT = HEADS_PER_GROUP * HEAD_DIM
BLK = 128
ROPE_THETA = 500000.0
ROT_DIM = HEAD_DIM // 4
ROT_HALF = ROT_DIM // 2
POOL_SIZES = (2, 4, 8, 16)
POOL_WIDTH = D_MODEL // 2
POOL_GROUP = POOL_WIDTH // len(POOL_SIZES)
NORM_EPS = 1e-6

LANES = 128
MXU_WIDTH = 256
VMEM_LIMIT_BYTES = 56 * 1024 * 1024

PROJ_TM = 1024
PROJ_TN = 1024
QKV_TM = 2048
PERM_TM = 256
NORM_TM = 512
ATTN_TB = 256
MERGE_TN = 512
POOL_HALO = 16

F32 = jnp.float32
BF16 = jnp.bfloat16

_NT = (((1,), (1,)), ((), ()))
_TN = (((0,), (0,)), ((), ()))


def _deinterleave_matrix(d):
    n = PERM_TM // d
    i = np.arange(PERM_TM)
    p = np.zeros((PERM_TM, PERM_TM), np.float32)
    p[i, (i % n) * d + i // n] = 1.0
    return p


def _norm_u_kernel(x_ref, g_ref, p4_ref, p16_ref, w_ref, h1_ref, h4_ref, h16_ref, u_ref,
                   hbuf_ref, wbf_ref):
    step = pl.program_id(0)

    @pl.when(step == 0)
    def _():
        wbf_ref[...] = w_ref[...].astype(BF16)
        hbuf_ref[...] = jnp.zeros_like(hbuf_ref)

    def body(h_in_ref, h_out_ref):
        for c in range(u_ref.shape[1] // MXU_WIDTH):
            cs = slice(c * MXU_WIDTH, (c + 1) * MXU_WIDTH)
            u_ref[:, cs] = jnp.dot(h_in_ref[...], wbf_ref[:, cs], preferred_element_type=F32)
        for sb in range(x_ref.shape[0] // PERM_TM):
            rs = slice(sb * PERM_TM, (sb + 1) * PERM_TM)
            x = x_ref[rs, :]
            ms = jnp.mean(x * x, axis=-1, keepdims=True)
            h = (x * lax.rsqrt(ms + NORM_EPS) * g_ref[...]).astype(BF16)
            h1_ref[rs, :] = h
            h_out_ref[rs, :] = h
            for p_ref, out_ref in ((p4_ref, h4_ref), (p16_ref, h16_ref)):
                d = out_ref.shape[0]
                n = PERM_TM // d
                hp = jnp.dot(p_ref[...], h, preferred_element_type=F32).astype(BF16)
                for r in range(d):
                    out_ref[r, sb * n:(sb + 1) * n, :] = hp[r * n:(r + 1) * n, :]

    for parity in range(2):
        pl.when(step % 2 == parity)(
            functools.partial(body, hbuf_ref.at[1 - parity], hbuf_ref.at[parity]))


def _rmsnorm_and_project_u(x2, gain, w_in, batch, seq, *, w_block):
    t, dm = x2.shape
    tm, tn = NORM_TM, PROJ_TN
    tiles = seq // tm
    n_tiles = t // tm
    perms = [jnp.asarray(_deinterleave_matrix(d), BF16) for _, d in DILATED_GROUPS[1:]]
    cur = lambda i: jnp.minimum(i, n_tiles - 1)
    prv = lambda i: jnp.maximum(i - 1, 0)
    out_shape = [jax.ShapeDtypeStruct((t, dm), BF16)]
    out_specs = [pl.BlockSpec((tm, dm), lambda i: (cur(i), 0))]
    for _, d in DILATED_GROUPS[1:]:
        out_shape.append(jax.ShapeDtypeStruct((batch, d, seq // d, dm), BF16))
        out_specs.append(pl.BlockSpec((None, d, tm // d, dm),
                                      lambda i: (cur(i) // tiles, 0, cur(i) % tiles, 0)))
    out_shape.append(jax.ShapeDtypeStruct((t, tn), F32))
    out_specs.append(pl.BlockSpec((tm, tn), lambda i: (prv(i), 0)))
    const = lambda i: (0, 0)
    resident = functools.partial(pl.BlockSpec, pipeline_mode=pl.Buffered(1))
    h1, h4, h16, u = pl.pallas_call(
        _norm_u_kernel,
        grid=(n_tiles + 1,),
        in_specs=[pl.BlockSpec((tm, dm), lambda i: (cur(i), 0)),
                  pl.BlockSpec((1, dm), const),
                  pl.BlockSpec((PERM_TM, PERM_TM), const),
                  pl.BlockSpec((PERM_TM, PERM_TM), const),
                  resident((dm, tn), lambda i: (0, w_block))],
        out_specs=out_specs,
        out_shape=out_shape,
        scratch_shapes=[pltpu.VMEM((2, tm, dm), BF16), pltpu.VMEM((dm, tn), BF16)],
        compiler_params=pltpu.CompilerParams(dimension_semantics=("arbitrary",),
                                             vmem_limit_bytes=VMEM_LIMIT_BYTES),
        name="rmsnorm_proj_u",
    )(x2, gain.reshape(1, dm), *perms, w_in)
    return (h1, h4.reshape(t, dm), h16.reshape(t, dm)), u


def _qkv_kernel(*refs, tm, tn):
    h_refs = refs[:N_ATTN_GROUPS]
    w_ref, gain_ref, rope_ref, out_ref, wt_ref, hbuf_ref, sem_ref = refs[N_ATTN_GROUPS:]
    rows = pl.num_programs(1)
    part = pl.program_id(0) % 3
    step = pl.program_id(0) * rows + pl.program_id(1)
    slot = step % 2

    def tile_copy(g, row_block, into):
        return pltpu.make_async_copy(h_refs[g].at[pl.ds(row_block * tm, tm), :],
                                     hbuf_ref.at[into], sem_ref.at[into])

    def start_fetch(of_step, into):
        for g in range(N_ATTN_GROUPS):
            pl.when(of_step // (3 * rows) == g)(lambda g=g: tile_copy(g, of_step % rows, into).start())

    pl.when(step == 0)(lambda: start_fetch(step, slot))
    pl.when(step + 1 < pl.num_programs(0) * rows)(lambda: start_fetch(step + 1, 1 - slot))
    tile_copy(0, 0, slot).wait()

    @pl.when(pl.program_id(1) == 0)
    def _():
        for c in range(tn // MXU_WIDTH):
            cs = slice(c * MXU_WIDTH, (c + 1) * MXU_WIDTH)
            wt_ref[cs, :] = w_ref[:, cs].T.astype(BF16)

    body = functools.partial(_qkv_body, hbuf_ref.at[slot], gain_ref, rope_ref, out_ref, wt_ref,
                             tm=tm, tn=tn)
    pl.when(part < 2)(functools.partial(body, normed=True))
    pl.when(part == 2)(functools.partial(body, normed=False))


def _qkv_body(h_ref, gain_ref, rope_ref, out_ref, wt_ref, *, tm, tn, normed):
    def chunk(c):
        ts = slice(c * ATTN_TB, (c + 1) * ATTN_TB)
        return lax.dot_general(wt_ref[...], h_ref[ts, :], _NT, preferred_element_type=F32)

    if not normed:
        for c in range(tm // ATTN_TB):
            out_ref[c] = chunk(c).astype(out_ref.dtype)
    else:
        for c in range(tm // ATTN_TB):
            acc = chunk(c)
            for hh in range(tn // HEAD_DIM):
                hs = slice(hh * HEAD_DIM, (hh + 1) * HEAD_DIM)
                for lt in range(ATTN_TB // LANES):
                    ls = slice(lt * LANES, (lt + 1) * LANES)
                    pos = slice(c * ATTN_TB + lt * LANES, c * ATTN_TB + (lt + 1) * LANES)
                    a = acc[hs, ls]
                    ss = jnp.sum(a * a, axis=0, keepdims=True)
                    r = lax.rsqrt(ss * (1.0 / HEAD_DIM) + NORM_EPS)
                    ag = a * gain_ref[...]
                    lo, hi = ag[0:ROT_HALF], ag[ROT_HALF:ROT_DIM]
                    cos, sin = rope_ref[0, :, pos], rope_ref[1, :, pos]
                    y = jnp.concatenate([lo * cos - hi * sin, hi * cos + lo * sin, ag[ROT_DIM:]], axis=0)
                    out_ref[c, hs, ls] = (y * r).astype(out_ref.dtype)


def _project_qkv(hs, w_in, gains, rope_t, seq):
    t, dm = hs[0].shape
    tm, tn = QKV_TM, PROJ_TN
    seq_tiles = seq // tm
    group = lambda gj: gj // 3
    part = lambda gj: gj % 3
    table = pl.BlockSpec((None, 2, ROT_HALF, tm), lambda gj, i: (group(gj), 0, 0, i % seq_tiles))
    return pl.pallas_call(
        functools.partial(_qkv_kernel, tm=tm, tn=tn),
        grid=(3 * N_ATTN_GROUPS, t // tm),
        in_specs=[*(pl.BlockSpec(memory_space=pl.ANY) for _ in range(N_ATTN_GROUPS)),
                  pl.BlockSpec((dm, tn), lambda gj, i: (0, part(gj) * N_ATTN_GROUPS + group(gj))),
                  pl.BlockSpec((None, HEAD_DIM, LANES), lambda gj, i: (jnp.minimum(part(gj), 1), 0, 0)),
                  table],
        out_specs=pl.BlockSpec((None, None, tm // ATTN_TB, tn, ATTN_TB),
                               lambda gj, i: (group(gj), part(gj), i, 0, 0)),
        out_shape=jax.ShapeDtypeStruct((N_ATTN_GROUPS, 3, t // ATTN_TB, tn, ATTN_TB), BF16),
        scratch_shapes=[pltpu.VMEM((tn, dm), BF16), pltpu.VMEM((2, tm, dm), BF16),
                        pltpu.SemaphoreType.DMA((2,))],
        compiler_params=pltpu.CompilerParams(
            dimension_semantics=("arbitrary", "arbitrary"),
            vmem_limit_bytes=VMEM_LIMIT_BYTES),
        name="proj_qkv",
    )(*hs, w_in, gains, rope_t)


def _rope_tables_t(seq, d):
    inv_freq = ROPE_THETA ** (-np.arange(0, ROT_DIM, 2, dtype=np.float64) / ROT_DIM)
    pos = np.arange(seq).reshape(seq // d, d).T.reshape(seq)
    ang = pos.astype(np.float64)[None, :] * inv_freq[:, None]
    return np.cos(ang).astype(np.float32), np.sin(ang).astype(np.float32)


def _gate_epilogue(acc, cs, half_bias_ref):
    return 0.5 + 0.5 * jnp.tanh(0.5 * acc + half_bias_ref[:, cs])


def _silu_epilogue(acc, cs, half_bias_ref):
    del cs, half_bias_ref
    half = 0.5 * acc
    return half + half * jnp.tanh(half)


def _proj_kernel(*refs, epilogues, n_aux, tn):
    h_ref, w_ref = refs[0], refs[1]
    aux = refs[2:2 + n_aux]
    out_ref = refs[2 + n_aux]
    wbf_ref = refs[3 + n_aux]
    j = pl.program_id(0)

    @pl.when(pl.program_id(1) == 0)
    def _():
        wbf_ref[...] = w_ref[...].astype(BF16)

    def body(epilogue):
        for c in range(tn // MXU_WIDTH):
            cs = slice(c * MXU_WIDTH, (c + 1) * MXU_WIDTH)
            acc = jnp.dot(h_ref[...], wbf_ref[:, cs], preferred_element_type=F32)
            out_ref[:, cs] = epilogue(acc, cs, *aux).astype(out_ref.dtype)

    if len(epilogues) == 1:
        body(epilogues[0][1])
    else:
        bounds = [first for first, _ in epilogues[1:]] + [None]
        for (first, fn), last in zip(epilogues, bounds):
            cond = (j >= first) if last is None else ((j >= first) & (j < last))
            pl.when(cond)(functools.partial(body, fn))


def _project(h, w_in, *, n_blocks, w_block_of, epilogues, out_dtype, aux=(), aux_specs=(), name):
    t, d = h.shape
    tm, tn = PROJ_TM, PROJ_TN
    kern = functools.partial(_proj_kernel, epilogues=epilogues, n_aux=len(aux), tn=tn)
    return pl.pallas_call(
        kern,
        grid=(n_blocks, t // tm),
        in_specs=[pl.BlockSpec((tm, d), lambda j, i: (i, 0)),
                  pl.BlockSpec((d, tn), lambda j, i: (0, w_block_of(j))),
                  *aux_specs],
        out_specs=pl.BlockSpec((tm, tn), lambda j, i: (i, j)),
        out_shape=jax.ShapeDtypeStruct((t, n_blocks * tn), out_dtype),
        scratch_shapes=[pltpu.VMEM((d, tn), BF16)],
        compiler_params=pltpu.CompilerParams(
            dimension_semantics=("arbitrary", "arbitrary"),
            vmem_limit_bytes=VMEM_LIMIT_BYTES),
        name=name,
    )(h, w_in, *aux)


def _band_bias():
    i = np.arange(BLK)[:, None]
    j = np.arange(2 * BLK)[None, :]
    band = (j >= i) & (j <= i + BLK)
    first = band & (j >= BLK)
    return np.where(np.stack([band, first]), 0.0, -np.inf).astype(np.float32)


class _AttentionBlocks:
    def __init__(self, bias_ref, q_ref, k_ref, v_ref, o_ref, lse_ref, kprev_ref, vprev_ref, firsts):
        self.bias_ref, self.q_ref, self.k_ref, self.v_ref = bias_ref, q_ref, k_ref, v_ref
        self.o_ref, self.lse_ref, self.kprev_ref, self.vprev_ref = o_ref, lse_ref, kprev_ref, vprev_ref
        self.firsts = firsts
        self.per = ATTN_TB // BLK
        self.scores = {}

    def _block(self, ref, n, cs):
        lo = (n % self.per) * BLK
        return ref[n // self.per, cs, lo:lo + BLK]

    def _with_previous(self, ref, prev_ref, n, cs):
        prev = prev_ref[cs, :] if n == 0 else self._block(ref, n - 1, cs)
        return jnp.concatenate([prev, self._block(ref, n, cs)], axis=1)

    def score_phase(self, n):
        first = self.firsts[n]
        bias = self.bias_ref[int(first)] if isinstance(first, bool) else self.bias_ref[jnp.where(first, 1, 0)]
        for h in range(HEADS_PER_GROUP):
            cs = slice(h * HEAD_DIM, (h + 1) * HEAD_DIM)
            kk = self._with_previous(self.k_ref, self.kprev_ref, n, cs)
            s = lax.dot_general(self._block(self.q_ref, n, cs), kk, _TN, preferred_element_type=F32)
            self.scores[n, h] = s + bias

    def value_phase(self, n):
        ts = slice(n * BLK, (n + 1) * BLK)
        lane = lax.broadcasted_iota(jnp.int32, (BLK, LANES), 1)
        m_tile = jnp.zeros((BLK, LANES), F32)
        l_tile = jnp.ones((BLK, LANES), F32)
        for h in range(HEADS_PER_GROUP):
            cs = slice(h * HEAD_DIM, (h + 1) * HEAD_DIM)
            s = self.scores.pop((n, h))
            m = jnp.max(s, axis=-1, keepdims=True)
            p = jnp.exp(s - m)
            l = jnp.sum(p, axis=-1, keepdims=True)
            vv = self._with_previous(self.v_ref, self.vprev_ref, n, cs)
            o = lax.dot_general(p.astype(BF16), vv, _NT, preferred_element_type=F32)
            self.o_ref[ts, cs] = (o / l).astype(self.o_ref.dtype)
            m_tile = jnp.where(lane == h, m, m_tile)
            l_tile = jnp.where(lane == h, l, l_tile)
        self.lse_ref[ts, :] = m_tile + jnp.log(l_tile)

    def carry(self, n_blocks):
        self.kprev_ref[...] = self._block(self.k_ref, n_blocks - 1, slice(None))
        self.vprev_ref[...] = self._block(self.v_ref, n_blocks - 1, slice(None))


def _rest_attn_kernel(h_ref, w_ref, half_bias_ref, bias_ref, qkv_ref,
                      out_ref, o_ref, lse_ref, wbf_ref, kprev_ref, vprev_ref,
                      *, tn, n_gate_blocks, steps_per_group, tiles_per_sub):
    j, i = pl.program_id(0), pl.program_id(1)
    step = j * pl.num_programs(1) + i

    @pl.when(i == 0)
    def _():
        wbf_ref[...] = w_ref[...].astype(BF16)

    @pl.when(step == 0)
    def _():
        kprev_ref[...] = jnp.zeros_like(kprev_ref)
        vprev_ref[...] = jnp.zeros_like(vprev_ref)

    n_blocks = tn // MXU_WIDTH
    per = ATTN_TB // BLK
    tiles_per_step = n_blocks // per
    group = step // steps_per_group
    tps = functools.reduce(lambda acc, g: jnp.where(group == g, tiles_per_sub[g], acc),
                           range(len(tiles_per_sub)), tiles_per_sub[0])
    tile0 = (step % steps_per_group) * tiles_per_step
    firsts = [((tile0 + n // per) & (tps - 1)) == 0 if n % per == 0 else False for n in range(n_blocks)]

    def body(epilogue):
        attn = _AttentionBlocks(bias_ref, qkv_ref.at[0], qkv_ref.at[1], qkv_ref.at[2], o_ref, lse_ref,
                                kprev_ref, vprev_ref, firsts)
        for c in range(n_blocks):
            cs = slice(c * MXU_WIDTH, (c + 1) * MXU_WIDTH)
            attn.score_phase(c)
            acc = jnp.dot(h_ref[...], wbf_ref[:, cs], preferred_element_type=F32)
            if c > 0:
                attn.value_phase(c - 1)
            out_ref[:, cs] = epilogue(acc, cs, half_bias_ref).astype(out_ref.dtype)
        attn.value_phase(n_blocks - 1)
        attn.carry(n_blocks)

    pl.when(j < n_gate_blocks)(functools.partial(body, _gate_epilogue))
    pl.when(j >= n_gate_blocks)(functools.partial(body, _silu_epilogue))


def _project_rest_with_attention(h, w_in, half_bias, qkv_t, seq, *, n_gate_blocks, w_block_of):
    t, dm = h.shape
    tm, tn = PROJ_TM, PROJ_TN
    n_blocks = n_gate_blocks + 2
    rows = t // tm
    n_groups, _, n_tiles = qkv_t.shape[:3]
    blocks_per_step = tn // MXU_WIDTH
    tiles_per_step = blocks_per_step * BLK // ATTN_TB
    steps_per_group = n_tiles // tiles_per_step
    assert n_groups * steps_per_group == n_blocks * rows, "attention tiles must fill the projection steps"
    tiles_per_sub = tuple(seq // d // ATTN_TB for _, d in DILATED_GROUPS)
    step = lambda j, i: j * rows + i
    tiles = pl.BlockSpec(
        (None, 3, tiles_per_step, ATTN_OUT, ATTN_TB),
        lambda j, i: (step(j, i) // steps_per_group, 0, step(j, i) % steps_per_group, 0, 0))
    tokens = tiles_per_step * ATTN_TB
    per_group = lambda width: pl.BlockSpec(
        (None, tokens, width), lambda j, i: (step(j, i) // steps_per_group, step(j, i) % steps_per_group, 0))
    return pl.pallas_call(
        functools.partial(_rest_attn_kernel, tn=tn, n_gate_blocks=n_gate_blocks,
                          steps_per_group=steps_per_group, tiles_per_sub=tiles_per_sub),
        grid=(n_blocks, rows),
        in_specs=[pl.BlockSpec((tm, dm), lambda j, i: (i, 0)),
                  pl.BlockSpec((dm, tn), lambda j, i: (0, w_block_of(j))),
                  pl.BlockSpec((1, tn), lambda j, i: (0, jnp.minimum(j, n_gate_blocks - 1))),
                  pl.BlockSpec((2, BLK, 2 * BLK), lambda j, i: (0, 0, 0), pipeline_mode=pl.Buffered(1)),
                  tiles],
        out_specs=[pl.BlockSpec((tm, tn), lambda j, i: (i, j)),
                   per_group(ATTN_OUT), per_group(LANES)],
        out_shape=[jax.ShapeDtypeStruct((t, n_blocks * tn), BF16),
                   jax.ShapeDtypeStruct((n_groups, t, ATTN_OUT), BF16),
                   jax.ShapeDtypeStruct((n_groups, t, LANES), F32)],
        scratch_shapes=[pltpu.VMEM((dm, tn), BF16),
                        pltpu.VMEM((ATTN_OUT, BLK), BF16), pltpu.VMEM((ATTN_OUT, BLK), BF16)],
        compiler_params=pltpu.CompilerParams(
            dimension_semantics=("arbitrary", "arbitrary"),
            vmem_limit_bytes=VMEM_LIMIT_BYTES),
        name="proj_rest_attention",
    )(h, w_in, half_bias, jnp.asarray(_band_bias()), qkv_t)


def _rows(ref):
    return jnp.concatenate([ref[r] for r in range(ref.shape[0])], axis=0)


def _unpermute_f32(q, x):
    hi = x.astype(BF16).astype(F32)
    r1 = x - hi
    mid = r1.astype(BF16).astype(F32)
    lo = r1 - mid
    packed = hi + pltpu.roll(mid, HEADS_PER_GROUP, 1) + pltpu.roll(lo, 2 * HEADS_PER_GROUP, 1)
    y = jnp.dot(q, packed.astype(BF16), preferred_element_type=F32)
    return (y + pltpu.roll(y, LANES - HEADS_PER_GROUP, 1)) + pltpu.roll(y, LANES - 2 * HEADS_PER_GROUP, 1)


def _tail_kernel(o0_ref, o1_ref, o2_ref, l0_ref, l1_ref, l2_ref, unperm_ref, gates_ref, z_ref,
                 u_ref, uh_ref, wba32_ref, wbp32_ref, pm32_ref, ps_ref,
                 out_ref, a_ref, b_ref, wba_ref, wbp_ref, pm_ref, *, tm, seq, n_tiles):
    step = pl.program_id(0)

    @pl.when(step == 0)
    def _():
        a_ref[...] = jnp.zeros_like(a_ref)
        b_ref[...] = jnp.zeros_like(b_ref)
        wba_ref[...] = wba32_ref[...].astype(BF16)
        wbp_ref[...] = wbp32_ref[...].astype(BF16)
        pm_ref[...] = pm32_ref[...].astype(BF16)

    body = functools.partial(
        _tail_step, o0_ref, o1_ref, o2_ref, l0_ref, l1_ref, l2_ref, unperm_ref, gates_ref, z_ref,
        u_ref, uh_ref, wba_ref, wbp_ref, pm_ref, ps_ref, out_ref,
        pos0=(jnp.minimum(step, n_tiles - 1) * tm) % seq, tm=tm)
    for parity in range(2):
        pl.when(step % 2 == parity)(
            functools.partial(body, a_ref.at[1 - parity], b_ref.at[1 - parity],
                              a_ref.at[parity], b_ref.at[parity]))


def _tail_step(o0_ref, o1_ref, o2_ref, l0_ref, l1_ref, l2_ref, unperm_ref, gates_ref, z_ref,
               u_ref, uh_ref, wba_ref, wbp_ref, pm_ref, ps_ref, out_ref,
               a_in_ref, b_in_ref, a_ref, b_ref, *, pos0, tm):
    q4, q16 = unperm_ref[0], unperm_ref[1]
    o0 = o0_ref[...]
    o1 = jnp.dot(q4, _rows(o1_ref), preferred_element_type=F32)
    o2 = jnp.dot(q16, _rows(o2_ref), preferred_element_type=F32)
    l0 = l0_ref[...]
    l1 = _unpermute_f32(q4, _rows(l1_ref))
    l2 = _unpermute_f32(q16, _rows(l2_ref))

    def merge_chunk(c):
        cs = slice(c * MERGE_TN, (c + 1) * MERGE_TN)
        y_attn = jnp.dot(a_in_ref[...], wba_ref[:, cs], preferred_element_type=F32)
        y_pool = jnp.dot(b_in_ref[...], wbp_ref[:, cs], preferred_element_type=F32)
        gate_attn = gates_ref[:, cs].astype(F32)
        gate_pool = gates_ref[:, D_MODEL + c * MERGE_TN:D_MODEL + (c + 1) * MERGE_TN].astype(F32)
        merged = gate_attn * y_attn + gate_pool * y_pool
        out_ref[:, cs] = merged.astype(out_ref.dtype)

    halo = uh_ref[...]
    halo = jnp.where(pos0 == 0, jnp.zeros_like(halo), halo)
    pos = pos0 + lax.broadcasted_iota(jnp.int32, (tm, 1), 0)
    pooled = []
    for g, ksz in enumerate(POOL_SIZES):
        cs = slice(g * POOL_GROUP, (g + 1) * POOL_GROUP)
        ext = jnp.concatenate([halo[:, cs], u_ref[:, cs]], axis=0)
        win, shift = ext, 1
        while shift < ksz:
            win = win + pltpu.roll(win, shift, 0)
            shift *= 2
        u = ext[POOL_HALO:]
        cnt = jnp.minimum(pos + 1, ksz).astype(F32)
        dlt = win[POOL_HALO:] / cnt - u
        pooled.append(jnp.dot(dlt.astype(BF16), pm_ref[g], preferred_element_type=F32))
    pool = jnp.concatenate(pooled, axis=1) * ps_ref[...]
    b_ref[...] = (pool * z_ref[:, ATTN_OUT:].astype(F32)).astype(BF16)

    for c in range(D_MODEL // MERGE_TN):
        merge_chunk(c)

    mx = jnp.maximum(jnp.maximum(l0, l1), l2)
    e0, e1, e2 = jnp.exp(l0 - mx), jnp.exp(l1 - mx), jnp.exp(l2 - mx)
    inv = 1.0 / (e0 + e1 + e2)
    w0, w1, w2 = e0 * inv, e1 * inv, e2 * inv
    for h in range(HEADS_PER_GROUP):
        cs = slice(h * HEAD_DIM, (h + 1) * HEAD_DIM)
        a = (w0[:, h:h + 1] * o0[:, cs].astype(F32)
             + w1[:, h:h + 1] * o1[:, cs]
             + w2[:, h:h + 1] * o2[:, cs])
        a_ref[:, cs] = (a * z_ref[:, cs].astype(F32)).astype(BF16)


def _residual_epilogue(acc, cs, x_ref):
    return acc + x_ref[:, cs]


def _tail(os_, lses, rest, u, wba, wbp, pm, ps, batch, seq):
    t, dm = u.shape[0], D_MODEL
    tm = PERM_TM
    tiles = seq // tm
    n_tiles = t // tm
    halo_per_tile = tm // POOL_HALO
    resident = functools.partial(pl.BlockSpec, pipeline_mode=pl.Buffered(1))
    cur = lambda i: jnp.minimum(i, n_tiles - 1)
    prv = lambda i: jnp.maximum(i - 1, 0)
    row = lambda blk: (lambda i: (cur(i), blk))
    row_prv = lambda blk: (lambda i: (prv(i), blk))
    unperms = jnp.asarray(np.stack([_deinterleave_matrix(d).T for _, d in DILATED_GROUPS[1:]]), BF16)
    o_in, l_in, o_specs, l_specs = [], [], [], []
    for g, (_, d) in enumerate(DILATED_GROUPS):
        for arr, width, ins, specs in ((os_, ATTN_OUT, o_in, o_specs), (lses, LANES, l_in, l_specs)):
            if d == 1:
                ins.append(arr)
                specs.append(pl.BlockSpec((None, tm, width), lambda i, g=g: (g, cur(i), 0)))
            else:
                ins.append(arr.reshape(arr.shape[0], batch, d, seq // d, width))
                specs.append(pl.BlockSpec((None, None, d, tm // d, width),
                                          lambda i, g=g: (g, cur(i) // tiles, 0, cur(i) % tiles, 0)))
    in_specs = (
        o_specs + l_specs
        + [resident(unperms.shape, lambda i: (0, 0, 0)),
           pl.BlockSpec((tm, 2 * D_MODEL), row_prv(0)),
           pl.BlockSpec((tm, ATTN_OUT + POOL_WIDTH), row(2 * D_MODEL // (ATTN_OUT + POOL_WIDTH))),
           pl.BlockSpec((tm, POOL_WIDTH), row(0)),
           pl.BlockSpec((POOL_HALO, POOL_WIDTH),
                        lambda i: (jnp.maximum(cur(i) * halo_per_tile - 1, 0), 0)),
           resident(wba.shape, lambda i: (0, 0)),
           resident(wbp.shape, lambda i: (0, 0)),
           resident(pm.shape, lambda i: (0, 0, 0)),
           resident(ps.shape, lambda i: (0, 0))])
    return pl.pallas_call(
        functools.partial(_tail_kernel, tm=tm, seq=seq, n_tiles=n_tiles),
        grid=(n_tiles + 1,),
        in_specs=in_specs,
        out_specs=pl.BlockSpec((tm, dm), row_prv(0)),
        out_shape=jax.ShapeDtypeStruct((t, dm), BF16),
        scratch_shapes=[pltpu.VMEM((2, tm, ATTN_OUT), BF16), pltpu.VMEM((2, tm, POOL_WIDTH), BF16),
                        pltpu.VMEM(wba.shape, BF16), pltpu.VMEM(wbp.shape, BF16), pltpu.VMEM(pm.shape, BF16)],
        compiler_params=pltpu.CompilerParams(
            dimension_semantics=("arbitrary",),
            vmem_limit_bytes=VMEM_LIMIT_BYTES),
        name="merge_pool",
    )(*o_in, *l_in, unperms, rest, rest, u, u, wba, wbp, pm, ps)


def kernel(x, norm_gain, w_in, b_gates, q_norm_gain, k_norm_gain, pool_maps, pool_scale,
           w_branch_attn, w_branch_pool, w_out):
    batch, seq, dm = x.shape
    t = batch * seq
    x2 = x.reshape(t, dm)
    hs, u = _rmsnorm_and_project_u(x2, norm_gain, w_in, batch, seq, w_block=10)

    lane_rep = lambda g: jnp.broadcast_to(g.astype(F32)[:, None], (HEAD_DIM, LANES))
    gains = jnp.stack([lane_rep(q_norm_gain * HEAD_DIM ** -0.5), lane_rep(k_norm_gain)])
    rope_t = jnp.asarray(np.stack([np.stack(_rope_tables_t(seq, d)) for _, d in DILATED_GROUPS]))
    qkv_t = _project_qkv(hs, w_in, gains, rope_t, seq)

    n_gate_blocks = 2 * D_MODEL // PROJ_TN
    half_bias = (0.5 * b_gates.astype(F32)).reshape(1, 2 * D_MODEL)
    rest_w_block = lambda j: jnp.where(j < 4, j + 12, jnp.where(j == 4, 9, 11))
    rest, o_all, lse_all = _project_rest_with_attention(
        hs[0], w_in, half_bias, qkv_t, seq, n_gate_blocks=n_gate_blocks, w_block_of=rest_w_block)

    merged = _tail(o_all, lse_all, rest, u,
                   w_branch_attn, w_branch_pool, pool_maps,
                   pool_scale.astype(F32).reshape(1, POOL_WIDTH), batch, seq)
    out = _project(merged, w_out, n_blocks=dm // PROJ_TN, w_block_of=lambda j: j,
                   epilogues=((0, _residual_epilogue),), out_dtype=F32, aux=(x2,),
                   aux_specs=(pl.BlockSpec((PROJ_TM, PROJ_TN), lambda j, i: (i, j)),),
                   name="proj_out")
    return out.reshape(batch, seq, dm)
```

```python
import functools

import numpy as np
import jax
import jax.numpy as jnp
from jax import lax
from jax.experimental import pallas as pl
from jax.experimental.pallas import tpu as pltpu

D_MODEL = 2048
HEAD_DIM = 128
HEADS_PER_GROUP = 8
DILATED_GROUPS = ((128, 1), (512, 4), (2048, 16))
N_ATTN_GROUPS = len(DILATED_GROUPS)
ATTN_OUT = HEADS_PER_GROUP * HEAD_DIM
BLK = 128
ROPE_THETA = 500000.0
ROT_DIM = HEAD_DIM // 4
ROT_HALF = ROT_DIM // 2
POOL_SIZES = (2, 4, 8, 16)
POOL_WIDTH = D_MODEL // 2
POOL_GROUP = POOL_WIDTH // len(POOL_SIZES)
NORM_EPS = 1e-6

LANES = 128
MXU_WIDTH = 256
VMEM_LIMIT_BYTES = 56 * 1024 * 1024

PROJ_TM = 1024
PROJ_TN = 1024
QKV_TM = 2048
PERM_TM = 256
NORM_TM = 512
ATTN_TB = 256
MERGE_TN = 512
POOL_HALO = 16

F32 = jnp.float32
BF16 = jnp.bfloat16

_NT = (((1,), (1,)), ((), ()))
_TN = (((0,), (0,)), ((), ()))


def _deinterleave_matrix(d):
    n = PERM_TM // d
    i = np.arange(PERM_TM)
    p = np.zeros((PERM_TM, PERM_TM), np.float32)
    p[i, (i % n) * d + i // n] = 1.0
    return p


def _norm_u_kernel(x_ref, g_ref, p4_ref, p16_ref, w_ref, h1_ref, h4_ref, h16_ref, u_ref,
                   hbuf_ref, wbf_ref):
    step = pl.program_id(0)

    @pl.when(step == 0)
    def _():
        wbf_ref[...] = w_ref[...].astype(BF16)
        hbuf_ref[...] = jnp.zeros_like(hbuf_ref)

    def body(h_in_ref, h_out_ref):
        for c in range(u_ref.shape[1] // MXU_WIDTH):
            cs = slice(c * MXU_WIDTH, (c + 1) * MXU_WIDTH)
            u_ref[:, cs] = jnp.dot(h_in_ref[...], wbf_ref[:, cs], preferred_element_type=F32)
        for sb in range(x_ref.shape[0] // PERM_TM):
            rs = slice(sb * PERM_TM, (sb + 1) * PERM_TM)
            x = x_ref[rs, :]
            ms = jnp.mean(x * x, axis=-1, keepdims=True)
            h = (x * lax.rsqrt(ms + NORM_EPS) * g_ref[...]).astype(BF16)
            h1_ref[rs, :] = h
            h_out_ref[rs, :] = h
            for p_ref, out_ref in ((p4_ref, h4_ref), (p16_ref, h16_ref)):
                d = out_ref.shape[0]
                n = PERM_TM // d
                hp = jnp.dot(p_ref[...], h, preferred_element_type=F32).astype(BF16)
                for r in range(d):
                    out_ref[r, sb * n:(sb + 1) * n, :] = hp[r * n:(r + 1) * n, :]

    for parity in range(2):
        pl.when(step % 2 == parity)(
            functools.partial(body, hbuf_ref.at[1 - parity], hbuf_ref.at[parity]))


def _rmsnorm_and_project_u(x2, gain, w_in, batch, seq, *, w_block):
    t, dm = x2.shape
    tm, tn = NORM_TM, PROJ_TN
    tiles = seq // tm
    n_tiles = t // tm
    perms = [jnp.asarray(_deinterleave_matrix(d), BF16) for _, d in DILATED_GROUPS[1:]]
    cur = lambda i: jnp.minimum(i, n_tiles - 1)
    prv = lambda i: jnp.maximum(i - 1, 0)
    out_shape = [jax.ShapeDtypeStruct((t, dm), BF16)]
    out_specs = [pl.BlockSpec((tm, dm), lambda i: (cur(i), 0))]
    for _, d in DILATED_GROUPS[1:]:
        out_shape.append(jax.ShapeDtypeStruct((batch, d, seq // d, dm), BF16))
        out_specs.append(pl.BlockSpec((None, d, tm // d, dm),
                                      lambda i: (cur(i) // tiles, 0, cur(i) % tiles, 0)))
    out_shape.append(jax.ShapeDtypeStruct((t, tn), F32))
    out_specs.append(pl.BlockSpec((tm, tn), lambda i: (prv(i), 0)))
    const = lambda i: (0, 0)
    resident = functools.partial(pl.BlockSpec, pipeline_mode=pl.Buffered(1))
    h1, h4, h16, u = pl.pallas_call(
        _norm_u_kernel,
        grid=(n_tiles + 1,),
        in_specs=[pl.BlockSpec((tm, dm), lambda i: (cur(i), 0)),
                  pl.BlockSpec((1, dm), const),
                  pl.BlockSpec((PERM_TM, PERM_TM), const),
                  pl.BlockSpec((PERM_TM, PERM_TM), const),
                  resident((dm, tn), lambda i: (0, w_block))],
        out_specs=out_specs,
        out_shape=out_shape,
        scratch_shapes=[pltpu.VMEM((2, tm, dm), BF16), pltpu.VMEM((dm, tn), BF16)],
        compiler_params=pltpu.CompilerParams(dimension_semantics=("arbitrary",),
                                             vmem_limit_bytes=VMEM_LIMIT_BYTES),
        name="rmsnorm_proj_u",
    )(x2, gain.reshape(1, dm), *perms, w_in)
    return (h1, h4.reshape(t, dm), h16.reshape(t, dm)), u


def _qkv_kernel(*refs, tm, tn):
    h_refs = refs[:N_ATTN_GROUPS]
    w_ref, gain_ref, rope_ref, out_ref, wt_even_ref, wt_odd_ref, hbuf_ref, sem_ref = refs[N_ATTN_GROUPS:]
    rows = pl.num_programs(1)
    block, row = pl.program_id(0), pl.program_id(1)
    part = (block + 2) % 3
    step = (block - 1) * rows + row
    n_steps = (pl.num_programs(0) - 1) * rows
    slot = (step + 2 * rows) % 2

    def tile_copy(g, row_block, into):
        return pltpu.make_async_copy(h_refs[g].at[pl.ds(row_block * tm, tm), :],
                                     hbuf_ref.at[into], sem_ref.at[into])

    def start_fetch(of_step, into):
        for g in range(N_ATTN_GROUPS):
            pl.when(of_step // (3 * rows) == g)(lambda g=g: tile_copy(g, of_step % rows, into).start())

    pl.when((step >= -1) & (step + 1 < n_steps))(lambda: start_fetch(step + 1, 1 - slot))

    def transpose_piece(wt_next_ref):
        piece = pl.ds(pl.multiple_of(row * MXU_WIDTH, MXU_WIDTH), MXU_WIDTH)
        wt_next_ref[piece, :] = w_ref[...].T.astype(BF16)

    def project(wt_ref, wt_next_ref, normed):
        tile_copy(0, 0, slot).wait()
        _qkv_body(hbuf_ref.at[slot], gain_ref, rope_ref, out_ref, wt_ref,
                  functools.partial(transpose_piece, wt_next_ref), tm=tm, tn=tn, normed=normed)

    pl.when(block == 0)(functools.partial(transpose_piece, wt_odd_ref))
    for parity, wt_ref, wt_next_ref in ((1, wt_odd_ref, wt_even_ref), (0, wt_even_ref, wt_odd_ref)):
        for normed in (True, False):
            of_kind = (part < 2) if normed else (part == 2)
            pl.when((block > 0) & (block % 2 == parity) & of_kind)(
                functools.partial(project, wt_ref, wt_next_ref, normed))


def _qkv_body(h_ref, gain_ref, rope_ref, out_ref, wt_ref, side_job, *, tm, tn, normed):
    def chunk(c):
        ts = slice(c * ATTN_TB, (c + 1) * ATTN_TB)
        return lax.dot_general(wt_ref[...], h_ref[ts, :], _NT, preferred_element_type=F32)

    if not normed:
        for c in range(tm // ATTN_TB):
            out_ref[c] = chunk(c).astype(out_ref.dtype)
            if c == 0:
                side_job()
    else:
        for c in range(tm // ATTN_TB):
            acc = chunk(c)
            if c == 0:
                side_job()
            for hh in range(tn // HEAD_DIM):
                hs = slice(hh * HEAD_DIM, (hh + 1) * HEAD_DIM)
                for lt in range(ATTN_TB // LANES):
                    ls = slice(lt * LANES, (lt + 1) * LANES)
                    pos = slice(c * ATTN_TB + lt * LANES, c * ATTN_TB + (lt + 1) * LANES)
                    a = acc[hs, ls]
                    ss = jnp.sum(a * a, axis=0, keepdims=True)
                    r = lax.rsqrt(ss * (1.0 / HEAD_DIM) + NORM_EPS)
                    ag = a * gain_ref[...]
                    lo, hi = ag[0:ROT_HALF], ag[ROT_HALF:ROT_DIM]
                    cos, sin = rope_ref[0, :, pos], rope_ref[1, :, pos]
                    y = jnp.concatenate([lo * cos - hi * sin, hi * cos + lo * sin, ag[ROT_DIM:]], axis=0)
                    out_ref[c, hs, ls] = (y * r).astype(out_ref.dtype)


def _project_qkv(hs, w_in, gains, rope_t, seq):
    t, dm = hs[0].shape
    tm, tn = QKV_TM, PROJ_TN
    seq_tiles = seq // tm
    n_blocks = 3 * N_ATTN_GROUPS
    pieces = tn // MXU_WIDTH
    assert t // tm == pieces
    cur = lambda b: jnp.maximum(b - 1, 0)
    nxt = lambda b: jnp.minimum(b, n_blocks - 1)
    group = lambda gj: gj // 3
    part = lambda gj: gj % 3
    out_row = lambda b, i: jnp.where(b == 0, 0, i)
    table = pl.BlockSpec((None, 2, ROT_HALF, tm),
                         lambda b, i: (group(cur(b)), 0, 0, out_row(b, i) % seq_tiles))
    return pl.pallas_call(
        functools.partial(_qkv_kernel, tm=tm, tn=tn),
        grid=(n_blocks + 1, t // tm),
        in_specs=[*(pl.BlockSpec(memory_space=pl.ANY) for _ in range(N_ATTN_GROUPS)),
                  pl.BlockSpec((dm, MXU_WIDTH), lambda b, i: (
                      0, (part(nxt(b)) * N_ATTN_GROUPS + group(nxt(b))) * pieces + i)),
                  pl.BlockSpec((None, HEAD_DIM, LANES), lambda b, i: (jnp.minimum(part(cur(b)), 1), 0, 0)),
                  table],
        out_specs=pl.BlockSpec((None, None, tm // ATTN_TB, tn, ATTN_TB),
                               lambda b, i: (group(cur(b)), part(cur(b)), out_row(b, i), 0, 0)),
        out_shape=jax.ShapeDtypeStruct((N_ATTN_GROUPS, 3, t // ATTN_TB, tn, ATTN_TB), BF16),
        scratch_shapes=[pltpu.VMEM((tn, dm), BF16), pltpu.VMEM((tn, dm), BF16),
                        pltpu.VMEM((2, tm, dm), BF16), pltpu.SemaphoreType.DMA((2,))],
        compiler_params=pltpu.CompilerParams(
            dimension_semantics=("arbitrary", "arbitrary"),
            vmem_limit_bytes=VMEM_LIMIT_BYTES),
        name="proj_qkv",
    )(*hs, w_in, gains, rope_t)


def _rope_tables_t(seq, d):
    inv_freq = ROPE_THETA ** (-np.arange(0, ROT_DIM, 2, dtype=np.float64) / ROT_DIM)
    pos = np.arange(seq).reshape(seq // d, d).T.reshape(seq)
    ang = pos.astype(np.float64)[None, :] * inv_freq[:, None]
    return np.cos(ang).astype(np.float32), np.sin(ang).astype(np.float32)


def _gate_epilogue(acc, cs, half_bias_ref):
    return 0.5 + 0.5 * jnp.tanh(0.5 * acc + half_bias_ref[:, cs])


def _silu_epilogue(acc, cs, half_bias_ref):
    del cs, half_bias_ref
    half = 0.5 * acc
    return half + half * jnp.tanh(half)


def _proj_kernel(*refs, epilogues, n_aux, tn):
    h_ref, w_ref = refs[0], refs[1]
    aux = refs[2:2 + n_aux]
    out_ref = refs[2 + n_aux]
    wbf_ref = refs[3 + n_aux]
    j = pl.program_id(0)

    @pl.when(pl.program_id(1) == 0)
    def _():
        wbf_ref[...] = w_ref[...].astype(BF16)

    def body(epilogue):
        for c in range(tn // MXU_WIDTH):
            cs = slice(c * MXU_WIDTH, (c + 1) * MXU_WIDTH)
            acc = jnp.dot(h_ref[...], wbf_ref[:, cs], preferred_element_type=F32)
            out_ref[:, cs] = epilogue(acc, cs, *aux).astype(out_ref.dtype)

    if len(epilogues) == 1:
        body(epilogues[0][1])
    else:
        bounds = [first for first, _ in epilogues[1:]] + [None]
        for (first, fn), last in zip(epilogues, bounds):
            cond = (j >= first) if last is None else ((j >= first) & (j < last))
            pl.when(cond)(functools.partial(body, fn))


def _project(h, w_in, *, n_blocks, w_block_of, epilogues, out_dtype, aux=(), aux_specs=(), name):
    t, d = h.shape
    tm, tn = PROJ_TM, PROJ_TN
    kern = functools.partial(_proj_kernel, epilogues=epilogues, n_aux=len(aux), tn=tn)
    return pl.pallas_call(
        kern,
        grid=(n_blocks, t // tm),
        in_specs=[pl.BlockSpec((tm, d), lambda j, i: (i, 0)),
                  pl.BlockSpec((d, tn), lambda j, i: (0, w_block_of(j))),
                  *aux_specs],
        out_specs=pl.BlockSpec((tm, tn), lambda j, i: (i, j)),
        out_shape=jax.ShapeDtypeStruct((t, n_blocks * tn), out_dtype),
        scratch_shapes=[pltpu.VMEM((d, tn), BF16)],
        compiler_params=pltpu.CompilerParams(
            dimension_semantics=("arbitrary", "arbitrary"),
            vmem_limit_bytes=VMEM_LIMIT_BYTES),
        name=name,
    )(h, w_in, *aux)


def _band_bias():
    i = np.arange(BLK)[:, None]
    j = np.arange(2 * BLK)[None, :]
    band = (j >= i) & (j <= i + BLK)
    first = band & (j >= BLK)
    return np.where(np.stack([band, first]), 0.0, -np.inf).astype(np.float32)


class _AttentionBlocks:
    def __init__(self, bias_ref, q_ref, k_ref, v_ref, o_ref, lse_ref, kprev_ref, vprev_ref, firsts):
        self.bias_ref, self.q_ref, self.k_ref, self.v_ref = bias_ref, q_ref, k_ref, v_ref
        self.o_ref, self.lse_ref, self.kprev_ref, self.vprev_ref = o_ref, lse_ref, kprev_ref, vprev_ref
        self.firsts = firsts
        self.per = ATTN_TB // BLK
        self.scores = {}

    def _block(self, ref, n, cs):
        lo = (n % self.per) * BLK
        return ref[n // self.per, cs, lo:lo + BLK]

    def _with_previous(self, ref, prev_ref, n, cs):
        prev = prev_ref[cs, :] if n == 0 else self._block(ref, n - 1, cs)
        return jnp.concatenate([prev, self._block(ref, n, cs)], axis=1)

    def score_phase(self, n):
        first = self.firsts[n]
        bias = self.bias_ref[int(first)] if isinstance(first, bool) else self.bias_ref[jnp.where(first, 1, 0)]
        for h in range(HEADS_PER_GROUP):
            cs = slice(h * HEAD_DIM, (h + 1) * HEAD_DIM)
            kk = self._with_previous(self.k_ref, self.kprev_ref, n, cs)
            s = lax.dot_general(self._block(self.q_ref, n, cs), kk, _TN, preferred_element_type=F32)
            self.scores[n, h] = s + bias

    def value_phase(self, n):
        ts = slice(n * BLK, (n + 1) * BLK)
        lane = lax.broadcasted_iota(jnp.int32, (BLK, LANES), 1)
        m_tile = jnp.zeros((BLK, LANES), F32)
        l_tile = jnp.ones((BLK, LANES), F32)
        for h in range(HEADS_PER_GROUP):
            cs = slice(h * HEAD_DIM, (h + 1) * HEAD_DIM)
            s = self.scores.pop((n, h))
            m = jnp.max(s, axis=-1, keepdims=True)
            p = jnp.exp(s - m)
            l = jnp.sum(p, axis=-1, keepdims=True)
            vv = self._with_previous(self.v_ref, self.vprev_ref, n, cs)
            o = lax.dot_general(p.astype(BF16), vv, _NT, preferred_element_type=F32)
            self.o_ref[ts, cs] = (o / l).astype(self.o_ref.dtype)
            m_tile = jnp.where(lane == h, m, m_tile)
            l_tile = jnp.where(lane == h, l, l_tile)
        self.lse_ref[ts, :] = m_tile + jnp.log(l_tile)

    def carry(self, n_blocks):
        self.kprev_ref[...] = self._block(self.k_ref, n_blocks - 1, slice(None))
        self.vprev_ref[...] = self._block(self.v_ref, n_blocks - 1, slice(None))


def _rest_attn_kernel(h_ref, w_ref, half_bias_ref, bias_ref, qkv_ref,
                      out_ref, o_ref, lse_ref, wbf_ref, kprev_ref, vprev_ref,
                      *, tn, n_gate_blocks, steps_per_group, tiles_per_sub):
    j, i = pl.program_id(0), pl.program_id(1)
    step = j * pl.num_programs(1) + i

    @pl.when(i == 0)
    def _():
        wbf_ref[...] = w_ref[...].astype(BF16)

    @pl.when(step == 0)
    def _():
        kprev_ref[...] = jnp.zeros_like(kprev_ref)
        vprev_ref[...] = jnp.zeros_like(vprev_ref)

    n_blocks = tn // MXU_WIDTH
    per = ATTN_TB // BLK
    tiles_per_step = n_blocks // per
    group = step // steps_per_group
    tps = functools.reduce(lambda acc, g: jnp.where(group == g, tiles_per_sub[g], acc),
                           range(len(tiles_per_sub)), tiles_per_sub[0])
    tile0 = (step % steps_per_group) * tiles_per_step
    firsts = [((tile0 + n // per) & (tps - 1)) == 0 if n % per == 0 else False for n in range(n_blocks)]

    def body(epilogue):
        attn = _AttentionBlocks(bias_ref, qkv_ref.at[0], qkv_ref.at[1], qkv_ref.at[2], o_ref, lse_ref,
                                kprev_ref, vprev_ref, firsts)
        for c in range(n_blocks):
            cs = slice(c * MXU_WIDTH, (c + 1) * MXU_WIDTH)
            attn.score_phase(c)
            acc = jnp.dot(h_ref[...], wbf_ref[:, cs], preferred_element_type=F32)
            if c > 0:
                attn.value_phase(c - 1)
            out_ref[:, cs] = epilogue(acc, cs, half_bias_ref).astype(out_ref.dtype)
        attn.value_phase(n_blocks - 1)
        attn.carry(n_blocks)

    pl.when(j < n_gate_blocks)(functools.partial(body, _gate_epilogue))
    pl.when(j >= n_gate_blocks)(functools.partial(body, _silu_epilogue))


def _project_rest_with_attention(h, w_in, half_bias, qkv_t, seq, *, n_gate_blocks, w_block_of):
    t, dm = h.shape
    tm, tn = PROJ_TM, PROJ_TN
    n_blocks = n_gate_blocks + 2
    rows = t // tm
    n_groups, _, n_tiles = qkv_t.shape[:3]
    blocks_per_step = tn // MXU_WIDTH
    tiles_per_step = blocks_per_step * BLK // ATTN_TB
    steps_per_group = n_tiles // tiles_per_step
    assert n_groups * steps_per_group == n_blocks * rows, "attention tiles must fill the projection steps"
    tiles_per_sub = tuple(seq // d // ATTN_TB for _, d in DILATED_GROUPS)
    step = lambda j, i: j * rows + i
    tiles = pl.BlockSpec(
        (None, 3, tiles_per_step, ATTN_OUT, ATTN_TB),
        lambda j, i: (step(j, i) // steps_per_group, 0, step(j, i) % steps_per_group, 0, 0))
    tokens = tiles_per_step * ATTN_TB
    per_group = lambda width: pl.BlockSpec(
        (None, tokens, width), lambda j, i: (step(j, i) // steps_per_group, step(j, i) % steps_per_group, 0))
    return pl.pallas_call(
        functools.partial(_rest_attn_kernel, tn=tn, n_gate_blocks=n_gate_blocks,
                          steps_per_group=steps_per_group, tiles_per_sub=tiles_per_sub),
        grid=(n_blocks, rows),
        in_specs=[pl.BlockSpec((tm, dm), lambda j, i: (i, 0)),
                  pl.BlockSpec((dm, tn), lambda j, i: (0, w_block_of(j))),
                  pl.BlockSpec((1, tn), lambda j, i: (0, jnp.minimum(j, n_gate_blocks - 1))),
                  pl.BlockSpec((2, BLK, 2 * BLK), lambda j, i: (0, 0, 0), pipeline_mode=pl.Buffered(1)),
                  tiles],
        out_specs=[pl.BlockSpec((tm, tn), lambda j, i: (i, j)),
                   per_group(ATTN_OUT), per_group(LANES)],
        out_shape=[jax.ShapeDtypeStruct((t, n_blocks * tn), BF16),
                   jax.ShapeDtypeStruct((n_groups, t, ATTN_OUT), BF16),
                   jax.ShapeDtypeStruct((n_groups, t, LANES), F32)],
        scratch_shapes=[pltpu.VMEM((dm, tn), BF16),
                        pltpu.VMEM((ATTN_OUT, BLK), BF16), pltpu.VMEM((ATTN_OUT, BLK), BF16)],
        compiler_params=pltpu.CompilerParams(
            dimension_semantics=("arbitrary", "arbitrary"),
            vmem_limit_bytes=VMEM_LIMIT_BYTES),
        name="proj_rest_attention",
    )(h, w_in, half_bias, jnp.asarray(_band_bias()), qkv_t)


def _rows(ref):
    return jnp.concatenate([ref[r] for r in range(ref.shape[0])], axis=0)


def _unpermute_f32(q, x):
    hi = x.astype(BF16).astype(F32)
    r1 = x - hi
    mid = r1.astype(BF16).astype(F32)
    lo = r1 - mid
    packed = hi + pltpu.roll(mid, HEADS_PER_GROUP, 1) + pltpu.roll(lo, 2 * HEADS_PER_GROUP, 1)
    y = jnp.dot(q, packed.astype(BF16), preferred_element_type=F32)
    return (y + pltpu.roll(y, LANES - HEADS_PER_GROUP, 1)) + pltpu.roll(y, LANES - 2 * HEADS_PER_GROUP, 1)


def _tail_kernel(o0_ref, o1_ref, o2_ref, l0_ref, l1_ref, l2_ref, unperm_ref, gates_ref, z_ref,
                 u_ref, uh_ref, wba32_ref, wbp32_ref, pm32_ref, ps_ref,
                 out_ref, a_ref, b_ref, wba_ref, wbp_ref, pm_ref, *, tm, seq, n_tiles):
    step = pl.program_id(0)

    @pl.when(step == 0)
    def _():
        a_ref[...] = jnp.zeros_like(a_ref)
        b_ref[...] = jnp.zeros_like(b_ref)
        wba_ref[...] = wba32_ref[...].astype(BF16)
        wbp_ref[...] = wbp32_ref[...].astype(BF16)
        pm_ref[...] = pm32_ref[...].astype(BF16)

    body = functools.partial(
        _tail_step, o0_ref, o1_ref, o2_ref, l0_ref, l1_ref, l2_ref, unperm_ref, gates_ref, z_ref,
        u_ref, uh_ref, wba_ref, wbp_ref, pm_ref, ps_ref, out_ref,
        pos0=(jnp.minimum(step, n_tiles - 1) * tm) % seq, tm=tm)
    for parity in range(2):
        pl.when(step % 2 == parity)(
            functools.partial(body, a_ref.at[1 - parity], b_ref.at[1 - parity],
                              a_ref.at[parity], b_ref.at[parity]))


def _tail_step(o0_ref, o1_ref, o2_ref, l0_ref, l1_ref, l2_ref, unperm_ref, gates_ref, z_ref,
               u_ref, uh_ref, wba_ref, wbp_ref, pm_ref, ps_ref, out_ref,
               a_in_ref, b_in_ref, a_ref, b_ref, *, pos0, tm):
    q4, q16 = unperm_ref[0], unperm_ref[1]
    o0 = o0_ref[...]
    o1 = jnp.dot(q4, _rows(o1_ref), preferred_element_type=F32)
    o2 = jnp.dot(q16, _rows(o2_ref), preferred_element_type=F32)
    l0 = l0_ref[...]
    l1 = _unpermute_f32(q4, _rows(l1_ref))
    l2 = _unpermute_f32(q16, _rows(l2_ref))

    def merge_chunk(c):
        cs = slice(c * MERGE_TN, (c + 1) * MERGE_TN)
        y_attn = jnp.dot(a_in_ref[...], wba_ref[:, cs], preferred_element_type=F32)
        y_pool = jnp.dot(b_in_ref[...], wbp_ref[:, cs], preferred_element_type=F32)
        gate_attn = gates_ref[:, cs].astype(F32)
        gate_pool = gates_ref[:, D_MODEL + c * MERGE_TN:D_MODEL + (c + 1) * MERGE_TN].astype(F32)
        merged = gate_attn * y_attn + gate_pool * y_pool
        out_ref[:, cs] = merged.astype(out_ref.dtype)

    halo = uh_ref[...]
    halo = jnp.where(pos0 == 0, jnp.zeros_like(halo), halo)
    pos = pos0 + lax.broadcasted_iota(jnp.int32, (tm, 1), 0)
    pooled = []
    for g, ksz in enumerate(POOL_SIZES):
        cs = slice(g * POOL_GROUP, (g + 1) * POOL_GROUP)
        ext = jnp.concatenate([halo[:, cs], u_ref[:, cs]], axis=0)
        win, shift = ext, 1
        while shift < ksz:
            win = win + pltpu.roll(win, shift, 0)
            shift *= 2
        u = ext[POOL_HALO:]
        cnt = jnp.minimum(pos + 1, ksz).astype(F32)
        dlt = win[POOL_HALO:] / cnt - u
        pooled.append(jnp.dot(dlt.astype(BF16), pm_ref[g], preferred_element_type=F32))
    pool = jnp.concatenate(pooled, axis=1) * ps_ref[...]
    b_ref[...] = (pool * z_ref[:, ATTN_OUT:].astype(F32)).astype(BF16)

    for c in range(D_MODEL // MERGE_TN):
        merge_chunk(c)

    mx = jnp.maximum(jnp.maximum(l0, l1), l2)
    e0, e1, e2 = jnp.exp(l0 - mx), jnp.exp(l1 - mx), jnp.exp(l2 - mx)
    inv = 1.0 / (e0 + e1 + e2)
    w0, w1, w2 = e0 * inv, e1 * inv, e2 * inv
    for h in range(HEADS_PER_GROUP):
        cs = slice(h * HEAD_DIM, (h + 1) * HEAD_DIM)
        a = (w0[:, h:h + 1] * o0[:, cs].astype(F32)
             + w1[:, h:h + 1] * o1[:, cs]
             + w2[:, h:h + 1] * o2[:, cs])
        a_ref[:, cs] = (a * z_ref[:, cs].astype(F32)).astype(BF16)


def _residual_epilogue(acc, cs, x_ref):
    return acc + x_ref[:, cs]


def _tail(os_, lses, rest, u, wba, wbp, pm, ps, batch, seq):
    t, dm = u.shape[0], D_MODEL
    tm = PERM_TM
    tiles = seq // tm
    n_tiles = t // tm
    halo_per_tile = tm // POOL_HALO
    resident = functools.partial(pl.BlockSpec, pipeline_mode=pl.Buffered(1))
    cur = lambda i: jnp.minimum(i, n_tiles - 1)
    prv = lambda i: jnp.maximum(i - 1, 0)
    row = lambda blk: (lambda i: (cur(i), blk))
    row_prv = lambda blk: (lambda i: (prv(i), blk))
    unperms = jnp.asarray(np.stack([_deinterleave_matrix(d).T for _, d in DILATED_GROUPS[1:]]), BF16)
    o_in, l_in, o_specs, l_specs = [], [], [], []
    for g, (_, d) in enumerate(DILATED_GROUPS):
        for arr, width, ins, specs in ((os_, ATTN_OUT, o_in, o_specs), (lses, LANES, l_in, l_specs)):
            if d == 1:
                ins.append(arr)
                specs.append(pl.BlockSpec((None, tm, width), lambda i, g=g: (g, cur(i), 0)))
            else:
                ins.append(arr.reshape(arr.shape[0], batch, d, seq // d, width))
                specs.append(pl.BlockSpec((None, None, d, tm // d, width),
                                          lambda i, g=g: (g, cur(i) // tiles, 0, cur(i) % tiles, 0)))
    in_specs = (
        o_specs + l_specs
        + [resident(unperms.shape, lambda i: (0, 0, 0)),
           pl.BlockSpec((tm, 2 * D_MODEL), row_prv(0)),
           pl.BlockSpec((tm, ATTN_OUT + POOL_WIDTH), row(2 * D_MODEL // (ATTN_OUT + POOL_WIDTH))),
           pl.BlockSpec((tm, POOL_WIDTH), row(0)),
           pl.BlockSpec((POOL_HALO, POOL_WIDTH),
                        lambda i: (jnp.maximum(cur(i) * halo_per_tile - 1, 0), 0)),
           resident(wba.shape, lambda i: (0, 0)),
           resident(wbp.shape, lambda i: (0, 0)),
           resident(pm.shape, lambda i: (0, 0, 0)),
           resident(ps.shape, lambda i: (0, 0))])
    return pl.pallas_call(
        functools.partial(_tail_kernel, tm=tm, seq=seq, n_tiles=n_tiles),
        grid=(n_tiles + 1,),
        in_specs=in_specs,
        out_specs=pl.BlockSpec((tm, dm), row_prv(0)),
        out_shape=jax.ShapeDtypeStruct((t, dm), BF16),
        scratch_shapes=[pltpu.VMEM((2, tm, ATTN_OUT), BF16), pltpu.VMEM((2, tm, POOL_WIDTH), BF16),
                        pltpu.VMEM(wba.shape, BF16), pltpu.VMEM(wbp.shape, BF16), pltpu.VMEM(pm.shape, BF16)],
        compiler_params=pltpu.CompilerParams(
            dimension_semantics=("arbitrary",),
            vmem_limit_bytes=VMEM_LIMIT_BYTES),
        name="merge_pool",
    )(*o_in, *l_in, unperms, rest, rest, u, u, wba, wbp, pm, ps)


def kernel(x, norm_gain, w_in, b_gates, q_norm_gain, k_norm_gain, pool_maps, pool_scale,
           w_branch_attn, w_branch_pool, w_out):
    batch, seq, dm = x.shape
    t = batch * seq
    x2 = x.reshape(t, dm)
    hs, u = _rmsnorm_and_project_u(x2, norm_gain, w_in, batch, seq, w_block=10)

    lane_rep = lambda g: jnp.broadcast_to(g.astype(F32)[:, None], (HEAD_DIM, LANES))
    gains = jnp.stack([lane_rep(q_norm_gain * HEAD_DIM ** -0.5), lane_rep(k_norm_gain)])
    rope_t = jnp.asarray(np.stack([np.stack(_rope_tables_t(seq, d)) for _, d in DILATED_GROUPS]))
    qkv_t = _project_qkv(hs, w_in, gains, rope_t, seq)

    n_gate_blocks = 2 * D_MODEL // PROJ_TN
    half_bias = (0.5 * b_gates.astype(F32)).reshape(1, 2 * D_MODEL)
    rest_w_block = lambda j: jnp.where(j < 4, j + 12, jnp.where(j == 4, 9, 11))
    rest, o_all, lse_all = _project_rest_with_attention(
        hs[0], w_in, half_bias, qkv_t, seq, n_gate_blocks=n_gate_blocks, w_block_of=rest_w_block)

    merged = _tail(o_all, lse_all, rest, u,
                   w_branch_attn, w_branch_pool, pool_maps,
                   pool_scale.astype(F32).reshape(1, POOL_WIDTH), batch, seq)
    out = _project(merged, w_out, n_blocks=dm // PROJ_TN, w_block_of=lambda j: j,
                   epilogues=((0, _residual_epilogue),), out_dtype=F32, aux=(x2,),
                   aux_specs=(pl.BlockSpec((PROJ_TM, PROJ_TN), lambda j, i: (i, j)),),
                   name="proj_out")
    return out.reshape(batch, seq, dm)
```

```python
import functools

import numpy as np
import jax
import jax.numpy as jnp
from jax import lax
from jax.experimental import pallas as pl
from jax.experimental.pallas import tpu as pltpu

D_MODEL = 2048
HEAD_DIM = 128
HEADS_PER_GROUP = 8
DILATED_GROUPS = ((128, 1), (512, 4), (2048, 16))
N_ATTN_GROUPS = len(DILATED_GROUPS)
ATTN_OUT = HEADS_PER_GROUP * HEAD_DIM
BLK = 128
ROPE_THETA = 500000.0
ROT_DIM = HEAD_DIM // 4
ROT_HALF = ROT_DIM // 2
POOL_SIZES = (2, 4, 8, 16)
POOL_WIDTH = D_MODEL // 2
POOL_GROUP = POOL_WIDTH // len(POOL_SIZES)
NORM_EPS = 1e-6

LANES = 128
MXU_WIDTH = 256
VMEM_LIMIT_BYTES = 56 * 1024 * 1024

PROJ_TM = 1024
PROJ_TN = 1024
OUT_TM = 512
QKV_TM = 2048
PERM_TM = 256
NORM_TM = 512
ATTN_TB = 256
MERGE_TN = 512
POOL_HALO = 16

F32 = jnp.float32
BF16 = jnp.bfloat16

_NT = (((1,), (1,)), ((), ()))
_TN = (((0,), (0,)), ((), ()))


def _deinterleave_matrix(d):
    n = PERM_TM // d
    i = np.arange(PERM_TM)
    p = np.zeros((PERM_TM, PERM_TM), np.float32)
    p[i, (i % n) * d + i // n] = 1.0
    return p


def _norm_u_kernel(x_ref, g_ref, p4_ref, p16_ref, w_ref, h1_ref, h4_ref, h16_ref, u_ref,
                   hbuf_ref, wbf_ref):
    step = pl.program_id(0)

    @pl.when(step == 0)
    def _():
        wbf_ref[...] = w_ref[...].astype(BF16)
        hbuf_ref[...] = jnp.zeros_like(hbuf_ref)

    def body(h_in_ref, h_out_ref):
        for c in range(u_ref.shape[1] // MXU_WIDTH):
            cs = slice(c * MXU_WIDTH, (c + 1) * MXU_WIDTH)
            u_ref[:, cs] = jnp.dot(h_in_ref[...], wbf_ref[:, cs], preferred_element_type=F32)
        for sb in range(x_ref.shape[0] // PERM_TM):
            rs = slice(sb * PERM_TM, (sb + 1) * PERM_TM)
            x = x_ref[rs, :]
            ms = jnp.mean(x * x, axis=-1, keepdims=True)
            h = (x * lax.rsqrt(ms + NORM_EPS) * g_ref[...]).astype(BF16)
            h1_ref[rs, :] = h
            h_out_ref[rs, :] = h
            for p_ref, out_ref in ((p4_ref, h4_ref), (p16_ref, h16_ref)):
                d = out_ref.shape[0]
                n = PERM_TM // d
                hp = jnp.dot(p_ref[...], h, preferred_element_type=F32).astype(BF16)
                for r in range(d):
                    out_ref[r, sb * n:(sb + 1) * n, :] = hp[r * n:(r + 1) * n, :]

    for parity in range(2):
        pl.when(step % 2 == parity)(
            functools.partial(body, hbuf_ref.at[1 - parity], hbuf_ref.at[parity]))


def _rmsnorm_and_project_u(x2, gain, w_in, batch, seq, *, w_block):
    t, dm = x2.shape
    tm, tn = NORM_TM, PROJ_TN
    tiles = seq // tm
    n_tiles = t // tm
    perms = [jnp.asarray(_deinterleave_matrix(d), BF16) for _, d in DILATED_GROUPS[1:]]
    cur = lambda i: jnp.minimum(i, n_tiles - 1)
    prv = lambda i: jnp.maximum(i - 1, 0)
    out_shape = [jax.ShapeDtypeStruct((t, dm), BF16)]
    out_specs = [pl.BlockSpec((tm, dm), lambda i: (cur(i), 0))]
    for _, d in DILATED_GROUPS[1:]:
        out_shape.append(jax.ShapeDtypeStruct((batch, d, seq // d, dm), BF16))
        out_specs.append(pl.BlockSpec((None, d, tm // d, dm),
                                      lambda i: (cur(i) // tiles, 0, cur(i) % tiles, 0)))
    out_shape.append(jax.ShapeDtypeStruct((t, tn), F32))
    out_specs.append(pl.BlockSpec((tm, tn), lambda i: (prv(i), 0)))
    const = lambda i: (0, 0)
    resident = functools.partial(pl.BlockSpec, pipeline_mode=pl.Buffered(1))
    h1, h4, h16, u = pl.pallas_call(
        _norm_u_kernel,
        grid=(n_tiles + 1,),
        in_specs=[pl.BlockSpec((tm, dm), lambda i: (cur(i), 0)),
                  pl.BlockSpec((1, dm), const),
                  pl.BlockSpec((PERM_TM, PERM_TM), const),
                  pl.BlockSpec((PERM_TM, PERM_TM), const),
                  resident((dm, tn), lambda i: (0, w_block))],
        out_specs=out_specs,
        out_shape=out_shape,
        scratch_shapes=[pltpu.VMEM((2, tm, dm), BF16), pltpu.VMEM((dm, tn), BF16)],
        compiler_params=pltpu.CompilerParams(dimension_semantics=("arbitrary",),
                                             vmem_limit_bytes=VMEM_LIMIT_BYTES),
        name="rmsnorm_proj_u",
    )(x2, gain.reshape(1, dm), *perms, w_in)
    return (h1, h4.reshape(t, dm), h16.reshape(t, dm)), u


def _qkv_kernel(*refs, tm, tn):
    h_refs = refs[:N_ATTN_GROUPS]
    w_ref, gain_ref, rope_ref, out_ref, wt_ref, hbuf_ref, sem_ref = refs[N_ATTN_GROUPS:]
    rows = pl.num_programs(1)
    part = pl.program_id(0) % 3
    step = pl.program_id(0) * rows + pl.program_id(1)
    slot = step % 2

    def tile_copy(g, row_block, into):
        return pltpu.make_async_copy(h_refs[g].at[pl.ds(row_block * tm, tm), :],
                                     hbuf_ref.at[into], sem_ref.at[into])

    def start_fetch(of_step, into):
        for g in range(N_ATTN_GROUPS):
            pl.when(of_step // (3 * rows) == g)(lambda g=g: tile_copy(g, of_step % rows, into).start())

    pl.when(step == 0)(lambda: start_fetch(step, slot))
    pl.when(step + 1 < pl.num_programs(0) * rows)(lambda: start_fetch(step + 1, 1 - slot))
    tile_copy(0, 0, slot).wait()

    @pl.when(pl.program_id(1) == 0)
    def _():
        for c in range(tn // MXU_WIDTH):
            cs = slice(c * MXU_WIDTH, (c + 1) * MXU_WIDTH)
            wt_ref[cs, :] = w_ref[:, cs].T.astype(BF16)

    body = functools.partial(_qkv_body, hbuf_ref.at[slot], gain_ref, rope_ref, out_ref, wt_ref,
                             tm=tm, tn=tn)
    pl.when(part < 2)(functools.partial(body, normed=True))
    pl.when(part == 2)(functools.partial(body, normed=False))


def _qkv_body(h_ref, gain_ref, rope_ref, out_ref, wt_ref, *, tm, tn, normed):
    def chunk(c):
        ts = slice(c * ATTN_TB, (c + 1) * ATTN_TB)
        return lax.dot_general(wt_ref[...], h_ref[ts, :], _NT, preferred_element_type=F32)

    if not normed:
        for c in range(tm // ATTN_TB):
            out_ref[c] = chunk(c).astype(out_ref.dtype)
    else:
        for c in range(tm // ATTN_TB):
            acc = chunk(c)
            for hh in range(tn // HEAD_DIM):
                hs = slice(hh * HEAD_DIM, (hh + 1) * HEAD_DIM)
                for lt in range(ATTN_TB // LANES):
                    ls = slice(lt * LANES, (lt + 1) * LANES)
                    pos = slice(c * ATTN_TB + lt * LANES, c * ATTN_TB + (lt + 1) * LANES)
                    a = acc[hs, ls]
                    ss = jnp.sum(a * a, axis=0, keepdims=True)
                    r = lax.rsqrt(ss * (1.0 / HEAD_DIM) + NORM_EPS)
                    ag = a * gain_ref[...]
                    lo, hi = ag[0:ROT_HALF], ag[ROT_HALF:ROT_DIM]
                    cos, sin = rope_ref[0, :, pos], rope_ref[1, :, pos]
                    y = jnp.concatenate([lo * cos - hi * sin, hi * cos + lo * sin, ag[ROT_DIM:]], axis=0)
                    out_ref[c, hs, ls] = (y * r).astype(out_ref.dtype)


def _project_qkv(hs, w_in, gains, rope_t, seq):
    t, dm = hs[0].shape
    tm, tn = QKV_TM, PROJ_TN
    seq_tiles = seq // tm
    group = lambda gj: gj // 3
    part = lambda gj: gj % 3
    table = pl.BlockSpec((None, 2, ROT_HALF, tm), lambda gj, i: (group(gj), 0, 0, i % seq_tiles))
    return pl.pallas_call(
        functools.partial(_qkv_kernel, tm=tm, tn=tn),
        grid=(3 * N_ATTN_GROUPS, t // tm),
        in_specs=[*(pl.BlockSpec(memory_space=pl.ANY) for _ in range(N_ATTN_GROUPS)),
                  pl.BlockSpec((dm, tn), lambda gj, i: (0, part(gj) * N_ATTN_GROUPS + group(gj))),
                  pl.BlockSpec((None, HEAD_DIM, LANES), lambda gj, i: (jnp.minimum(part(gj), 1), 0, 0)),
                  table],
        out_specs=pl.BlockSpec((None, None, tm // ATTN_TB, tn, ATTN_TB),
                               lambda gj, i: (group(gj), part(gj), i, 0, 0)),
        out_shape=jax.ShapeDtypeStruct((N_ATTN_GROUPS, 3, t // ATTN_TB, tn, ATTN_TB), BF16),
        scratch_shapes=[pltpu.VMEM((tn, dm), BF16), pltpu.VMEM((2, tm, dm), BF16),
                        pltpu.SemaphoreType.DMA((2,))],
        compiler_params=pltpu.CompilerParams(
            dimension_semantics=("arbitrary", "arbitrary"),
            vmem_limit_bytes=VMEM_LIMIT_BYTES),
        name="proj_qkv",
    )(*hs, w_in, gains, rope_t)


def _rope_tables_t(seq, d):
    inv_freq = ROPE_THETA ** (-np.arange(0, ROT_DIM, 2, dtype=np.float64) / ROT_DIM)
    pos = np.arange(seq).reshape(seq // d, d).T.reshape(seq)
    ang = pos.astype(np.float64)[None, :] * inv_freq[:, None]
    return np.cos(ang).astype(np.float32), np.sin(ang).astype(np.float32)


def _gate_epilogue(acc, cs, half_bias_ref):
    return 0.5 + 0.5 * jnp.tanh(0.5 * acc + half_bias_ref[:, cs])


def _silu_epilogue(acc, cs, half_bias_ref):
    del cs, half_bias_ref
    half = 0.5 * acc
    return half + half * jnp.tanh(half)


def _out_kernel(h_ref, w_ref, x_ref, out_ref, wstage_ref, wbf_ref, sem_ref):
    n_chunks = wstage_ref.shape[0]

    def chunk_copy(c):
        return pltpu.make_async_copy(w_ref.at[:, pl.ds(c * MXU_WIDTH, MXU_WIDTH)],
                                     wstage_ref.at[c], sem_ref.at[c])

    def body(first):
        for c in range(n_chunks):
            cs = slice(c * MXU_WIDTH, (c + 1) * MXU_WIDTH)
            if first:
                chunk_copy(c).wait()
                wbf_ref[:, cs] = wstage_ref[c].astype(BF16)
            acc = jnp.dot(h_ref[...], wbf_ref[:, cs], preferred_element_type=F32)
            out_ref[:, cs] = (acc + x_ref[:, cs]).astype(out_ref.dtype)

    @pl.when(pl.program_id(0) == 0)
    def _():
        for c in range(n_chunks):
            chunk_copy(c).start()
        body(True)

    pl.when(pl.program_id(0) > 0)(functools.partial(body, False))


def _project_out(h, w, x2):
    t, d = h.shape
    dn = w.shape[1]
    tm = OUT_TM
    n_chunks = dn // MXU_WIDTH
    row_tile = lambda width: pl.BlockSpec((tm, width), lambda i: (i, 0))
    return pl.pallas_call(
        _out_kernel,
        grid=(t // tm,),
        in_specs=[row_tile(d), pl.BlockSpec(memory_space=pl.ANY), row_tile(dn)],
        out_specs=row_tile(dn),
        out_shape=jax.ShapeDtypeStruct((t, dn), F32),
        scratch_shapes=[pltpu.VMEM((n_chunks, d, MXU_WIDTH), F32), pltpu.VMEM((d, dn), BF16),
                        pltpu.SemaphoreType.DMA((n_chunks,))],
        compiler_params=pltpu.CompilerParams(
            dimension_semantics=("arbitrary",),
            vmem_limit_bytes=VMEM_LIMIT_BYTES),
        name="proj_out",
    )(h, w, x2)


def _band_bias():
    i = np.arange(BLK)[:, None]
    j = np.arange(2 * BLK)[None, :]
    band = (j >= i) & (j <= i + BLK)
    first = band & (j >= BLK)
    return np.where(np.stack([band, first]), 0.0, -np.inf).astype(np.float32)


class _AttentionBlocks:
    def __init__(self, bias_ref, q_ref, k_ref, v_ref, o_ref, lse_ref, kprev_ref, vprev_ref, firsts):
        self.bias_ref, self.q_ref, self.k_ref, self.v_ref = bias_ref, q_ref, k_ref, v_ref
        self.o_ref, self.lse_ref, self.kprev_ref, self.vprev_ref = o_ref, lse_ref, kprev_ref, vprev_ref
        self.firsts = firsts
        self.per = ATTN_TB // BLK
        self.scores = {}

    def _block(self, ref, n, cs):
        lo = (n % self.per) * BLK
        return ref[n // self.per, cs, lo:lo + BLK]

    def _with_previous(self, ref, prev_ref, n, cs):
        prev = prev_ref[cs, :] if n == 0 else self._block(ref, n - 1, cs)
        return jnp.concatenate([prev, self._block(ref, n, cs)], axis=1)

    def score_phase(self, n):
        first = self.firsts[n]
        bias = self.bias_ref[int(first)] if isinstance(first, bool) else self.bias_ref[jnp.where(first, 1, 0)]
        for h in range(HEADS_PER_GROUP):
            cs = slice(h * HEAD_DIM, (h + 1) * HEAD_DIM)
            kk = self._with_previous(self.k_ref, self.kprev_ref, n, cs)
            s = lax.dot_general(self._block(self.q_ref, n, cs), kk, _TN, preferred_element_type=F32)
            self.scores[n, h] = s + bias

    def value_phase(self, n):
        ts = slice(n * BLK, (n + 1) * BLK)
        lane = lax.broadcasted_iota(jnp.int32, (BLK, LANES), 1)
        m_tile = jnp.zeros((BLK, LANES), F32)
        l_tile = jnp.ones((BLK, LANES), F32)
        for h in range(HEADS_PER_GROUP):
            cs = slice(h * HEAD_DIM, (h + 1) * HEAD_DIM)
            s = self.scores.pop((n, h))
            m = jnp.max(s, axis=-1, keepdims=True)
            p = jnp.exp(s - m)
            l = jnp.sum(p, axis=-1, keepdims=True)
            vv = self._with_previous(self.v_ref, self.vprev_ref, n, cs)
            o = lax.dot_general(p.astype(BF16), vv, _NT, preferred_element_type=F32)
            self.o_ref[ts, cs] = (o / l).astype(self.o_ref.dtype)
            m_tile = jnp.where(lane == h, m, m_tile)
            l_tile = jnp.where(lane == h, l, l_tile)
        self.lse_ref[ts, :] = m_tile + jnp.log(l_tile)

    def carry(self, n_blocks):
        self.kprev_ref[...] = self._block(self.k_ref, n_blocks - 1, slice(None))
        self.vprev_ref[...] = self._block(self.v_ref, n_blocks - 1, slice(None))


def _rest_attn_kernel(h_ref, w_ref, half_bias_ref, bias_ref, qkv_ref,
                      out_ref, o_ref, lse_ref, wbf_ref, kprev_ref, vprev_ref,
                      *, tn, n_gate_blocks, steps_per_group, tiles_per_sub):
    j, i = pl.program_id(0), pl.program_id(1)
    step = j * pl.num_programs(1) + i

    @pl.when(i == 0)
    def _():
        wbf_ref[...] = w_ref[...].astype(BF16)

    @pl.when(step == 0)
    def _():
        kprev_ref[...] = jnp.zeros_like(kprev_ref)
        vprev_ref[...] = jnp.zeros_like(vprev_ref)

    n_blocks = tn // MXU_WIDTH
    per = ATTN_TB // BLK
    tiles_per_step = n_blocks // per
    group = step // steps_per_group
    tps = functools.reduce(lambda acc, g: jnp.where(group == g, tiles_per_sub[g], acc),
                           range(len(tiles_per_sub)), tiles_per_sub[0])
    tile0 = (step % steps_per_group) * tiles_per_step
    firsts = [((tile0 + n // per) & (tps - 1)) == 0 if n % per == 0 else False for n in range(n_blocks)]

    def body(epilogue):
        attn = _AttentionBlocks(bias_ref, qkv_ref.at[0], qkv_ref.at[1], qkv_ref.at[2], o_ref, lse_ref,
                                kprev_ref, vprev_ref, firsts)
        for c in range(n_blocks):
            cs = slice(c * MXU_WIDTH, (c + 1) * MXU_WIDTH)
            attn.score_phase(c)
            acc = jnp.dot(h_ref[...], wbf_ref[:, cs], preferred_element_type=F32)
            if c > 0:
                attn.value_phase(c - 1)
            out_ref[:, cs] = epilogue(acc, cs, half_bias_ref).astype(out_ref.dtype)
        attn.value_phase(n_blocks - 1)
        attn.carry(n_blocks)

    pl.when(j < n_gate_blocks)(functools.partial(body, _gate_epilogue))
    pl.when(j >= n_gate_blocks)(functools.partial(body, _silu_epilogue))


def _project_rest_with_attention(h, w_in, half_bias, qkv_t, seq, *, n_gate_blocks, w_block_of):
    t, dm = h.shape
    tm, tn = PROJ_TM, PROJ_TN
    n_blocks = n_gate_blocks + 2
    rows = t // tm
    n_groups, _, n_tiles = qkv_t.shape[:3]
    blocks_per_step = tn // MXU_WIDTH
    tiles_per_step = blocks_per_step * BLK // ATTN_TB
    steps_per_group = n_tiles // tiles_per_step
    assert n_groups * steps_per_group == n_blocks * rows, "attention tiles must fill the projection steps"
    tiles_per_sub = tuple(seq // d // ATTN_TB for _, d in DILATED_GROUPS)
    step = lambda j, i: j * rows + i
    tiles = pl.BlockSpec(
        (None, 3, tiles_per_step, ATTN_OUT, ATTN_TB),
        lambda j, i: (step(j, i) // steps_per_group, 0, step(j, i) % steps_per_group, 0, 0))
    tokens = tiles_per_step * ATTN_TB
    per_group = lambda width: pl.BlockSpec(
        (None, tokens, width), lambda j, i: (step(j, i) // steps_per_group, step(j, i) % steps_per_group, 0))
    return pl.pallas_call(
        functools.partial(_rest_attn_kernel, tn=tn, n_gate_blocks=n_gate_blocks,
                          steps_per_group=steps_per_group, tiles_per_sub=tiles_per_sub),
        grid=(n_blocks, rows),
        in_specs=[pl.BlockSpec((tm, dm), lambda j, i: (i, 0)),
                  pl.BlockSpec((dm, tn), lambda j, i: (0, w_block_of(j))),
                  pl.BlockSpec((1, tn), lambda j, i: (0, jnp.minimum(j, n_gate_blocks - 1))),
                  pl.BlockSpec((2, BLK, 2 * BLK), lambda j, i: (0, 0, 0), pipeline_mode=pl.Buffered(1)),
                  tiles],
        out_specs=[pl.BlockSpec((tm, tn), lambda j, i: (i, j)),
                   per_group(ATTN_OUT), per_group(LANES)],
        out_shape=[jax.ShapeDtypeStruct((t, n_blocks * tn), BF16),
                   jax.ShapeDtypeStruct((n_groups, t, ATTN_OUT), BF16),
                   jax.ShapeDtypeStruct((n_groups, t, LANES), F32)],
        scratch_shapes=[pltpu.VMEM((dm, tn), BF16),
                        pltpu.VMEM((ATTN_OUT, BLK), BF16), pltpu.VMEM((ATTN_OUT, BLK), BF16)],
        compiler_params=pltpu.CompilerParams(
            dimension_semantics=("arbitrary", "arbitrary"),
            vmem_limit_bytes=VMEM_LIMIT_BYTES),
        name="proj_rest_attention",
    )(h, w_in, half_bias, jnp.asarray(_band_bias()), qkv_t)


def _rows(ref):
    return jnp.concatenate([ref[r] for r in range(ref.shape[0])], axis=0)


def _unpermute_f32(q, x):
    hi = x.astype(BF16).astype(F32)
    r1 = x - hi
    mid = r1.astype(BF16).astype(F32)
    lo = r1 - mid
    packed = hi + pltpu.roll(mid, HEADS_PER_GROUP, 1) + pltpu.roll(lo, 2 * HEADS_PER_GROUP, 1)
    y = jnp.dot(q, packed.astype(BF16), preferred_element_type=F32)
    return (y + pltpu.roll(y, LANES - HEADS_PER_GROUP, 1)) + pltpu.roll(y, LANES - 2 * HEADS_PER_GROUP, 1)


def _tail_kernel(o0_ref, o1_ref, o2_ref, l0_ref, l1_ref, l2_ref, unperm_ref, gates_ref, z_ref,
                 u_ref, uh_ref, wba32_ref, wbp32_ref, pm32_ref, ps_ref,
                 out_ref, a_ref, b_ref, wba_ref, wbp_ref, pm_ref, *, tm, seq, n_tiles):
    step = pl.program_id(0)

    @pl.when(step == 0)
    def _():
        a_ref[...] = jnp.zeros_like(a_ref)
        b_ref[...] = jnp.zeros_like(b_ref)
        wba_ref[...] = wba32_ref[...].astype(BF16)
        wbp_ref[...] = wbp32_ref[...].astype(BF16)
        pm_ref[...] = pm32_ref[...].astype(BF16)

    body = functools.partial(
        _tail_step, o0_ref, o1_ref, o2_ref, l0_ref, l1_ref, l2_ref, unperm_ref, gates_ref, z_ref,
        u_ref, uh_ref, wba_ref, wbp_ref, pm_ref, ps_ref, out_ref,
        pos0=(jnp.minimum(step, n_tiles - 1) * tm) % seq, tm=tm)
    for parity in range(2):
        pl.when(step % 2 == parity)(
            functools.partial(body, a_ref.at[1 - parity], b_ref.at[1 - parity],
                              a_ref.at[parity], b_ref.at[parity]))


def _tail_step(o0_ref, o1_ref, o2_ref, l0_ref, l1_ref, l2_ref, unperm_ref, gates_ref, z_ref,
               u_ref, uh_ref, wba_ref, wbp_ref, pm_ref, ps_ref, out_ref,
               a_in_ref, b_in_ref, a_ref, b_ref, *, pos0, tm):
    q4, q16 = unperm_ref[0], unperm_ref[1]
    o0 = o0_ref[...]
    o1 = jnp.dot(q4, _rows(o1_ref), preferred_element_type=F32)
    o2 = jnp.dot(q16, _rows(o2_ref), preferred_element_type=F32)
    l0 = l0_ref[...]
    l1 = _unpermute_f32(q4, _rows(l1_ref))
    l2 = _unpermute_f32(q16, _rows(l2_ref))

    def merge_chunk(c):
        cs = slice(c * MERGE_TN, (c + 1) * MERGE_TN)
        y_attn = jnp.dot(a_in_ref[...], wba_ref[:, cs], preferred_element_type=F32)
        y_pool = jnp.dot(b_in_ref[...], wbp_ref[:, cs], preferred_element_type=F32)
        gate_attn = gates_ref[:, cs].astype(F32)
        gate_pool = gates_ref[:, D_MODEL + c * MERGE_TN:D_MODEL + (c + 1) * MERGE_TN].astype(F32)
        merged = gate_attn * y_attn + gate_pool * y_pool
        out_ref[:, cs] = merged.astype(out_ref.dtype)

    halo = uh_ref[...]
    halo = jnp.where(pos0 == 0, jnp.zeros_like(halo), halo)
    pos = pos0 + lax.broadcasted_iota(jnp.int32, (tm, 1), 0)
    pooled = []
    for g, ksz in enumerate(POOL_SIZES):
        cs = slice(g * POOL_GROUP, (g + 1) * POOL_GROUP)
        ext = jnp.concatenate([halo[:, cs], u_ref[:, cs]], axis=0)
        win, shift = ext, 1
        while shift < ksz:
            win = win + pltpu.roll(win, shift, 0)
            shift *= 2
        u = ext[POOL_HALO:]
        cnt = jnp.minimum(pos + 1, ksz).astype(F32)
        dlt = win[POOL_HALO:] / cnt - u
        pooled.append(jnp.dot(dlt.astype(BF16), pm_ref[g], preferred_element_type=F32))
    pool = jnp.concatenate(pooled, axis=1) * ps_ref[...]
    b_ref[...] = (pool * z_ref[:, ATTN_OUT:].astype(F32)).astype(BF16)

    for c in range(D_MODEL // MERGE_TN):
        merge_chunk(c)

    mx = jnp.maximum(jnp.maximum(l0, l1), l2)
    e0, e1, e2 = jnp.exp(l0 - mx), jnp.exp(l1 - mx), jnp.exp(l2 - mx)
    inv = 1.0 / (e0 + e1 + e2)
    w0, w1, w2 = e0 * inv, e1 * inv, e2 * inv
    for h in range(HEADS_PER_GROUP):
        cs = slice(h * HEAD_DIM, (h + 1) * HEAD_DIM)
        a = (w0[:, h:h + 1] * o0[:, cs].astype(F32)
             + w1[:, h:h + 1] * o1[:, cs]
             + w2[:, h:h + 1] * o2[:, cs])
        a_ref[:, cs] = (a * z_ref[:, cs].astype(F32)).astype(BF16)


def _tail(os_, lses, rest, u, wba, wbp, pm, ps, batch, seq):
    t, dm = u.shape[0], D_MODEL
    tm = PERM_TM
    tiles = seq // tm
    n_tiles = t // tm
    halo_per_tile = tm // POOL_HALO
    resident = functools.partial(pl.BlockSpec, pipeline_mode=pl.Buffered(1))
    cur = lambda i: jnp.minimum(i, n_tiles - 1)
    prv = lambda i: jnp.maximum(i - 1, 0)
    row = lambda blk: (lambda i: (cur(i), blk))
    row_prv = lambda blk: (lambda i: (prv(i), blk))
    unperms = jnp.asarray(np.stack([_deinterleave_matrix(d).T for _, d in DILATED_GROUPS[1:]]), BF16)
    o_in, l_in, o_specs, l_specs = [], [], [], []
    for g, (_, d) in enumerate(DILATED_GROUPS):
        for arr, width, ins, specs in ((os_, ATTN_OUT, o_in, o_specs), (lses, LANES, l_in, l_specs)):
            if d == 1:
                ins.append(arr)
                specs.append(pl.BlockSpec((None, tm, width), lambda i, g=g: (g, cur(i), 0)))
            else:
                ins.append(arr.reshape(arr.shape[0], batch, d, seq // d, width))
                specs.append(pl.BlockSpec((None, None, d, tm // d, width),
                                          lambda i, g=g: (g, cur(i) // tiles, 0, cur(i) % tiles, 0)))
    in_specs = (
        o_specs + l_specs
        + [resident(unperms.shape, lambda i: (0, 0, 0)),
           pl.BlockSpec((tm, 2 * D_MODEL), row_prv(0)),
           pl.BlockSpec((tm, ATTN_OUT + POOL_WIDTH), row(2 * D_MODEL // (ATTN_OUT + POOL_WIDTH))),
           pl.BlockSpec((tm, POOL_WIDTH), row(0)),
           pl.BlockSpec((POOL_HALO, POOL_WIDTH),
                        lambda i: (jnp.maximum(cur(i) * halo_per_tile - 1, 0), 0)),
           resident(wba.shape, lambda i: (0, 0)),
           resident(wbp.shape, lambda i: (0, 0)),
           resident(pm.shape, lambda i: (0, 0, 0)),
           resident(ps.shape, lambda i: (0, 0))])
    return pl.pallas_call(
        functools.partial(_tail_kernel, tm=tm, seq=seq, n_tiles=n_tiles),
        grid=(n_tiles + 1,),
        in_specs=in_specs,
        out_specs=pl.BlockSpec((tm, dm), row_prv(0)),
        out_shape=jax.ShapeDtypeStruct((t, dm), BF16),
        scratch_shapes=[pltpu.VMEM((2, tm, ATTN_OUT), BF16), pltpu.VMEM((2, tm, POOL_WIDTH), BF16),
                        pltpu.VMEM(wba.shape, BF16), pltpu.VMEM(wbp.shape, BF16), pltpu.VMEM(pm.shape, BF16)],
        compiler_params=pltpu.CompilerParams(
            dimension_semantics=("arbitrary",),
            vmem_limit_bytes=VMEM_LIMIT_BYTES),
        name="merge_pool",
    )(*o_in, *l_in, unperms, rest, rest, u, u, wba, wbp, pm, ps)


def kernel(x, norm_gain, w_in, b_gates, q_norm_gain, k_norm_gain, pool_maps, pool_scale,
           w_branch_attn, w_branch_pool, w_out):
    batch, seq, dm = x.shape
    t = batch * seq
    x2 = x.reshape(t, dm)
    hs, u = _rmsnorm_and_project_u(x2, norm_gain, w_in, batch, seq, w_block=10)

    lane_rep = lambda g: jnp.broadcast_to(g.astype(F32)[:, None], (HEAD_DIM, LANES))
    gains = jnp.stack([lane_rep(q_norm_gain * HEAD_DIM ** -0.5), lane_rep(k_norm_gain)])
    rope_t = jnp.asarray(np.stack([np.stack(_rope_tables_t(seq, d)) for _, d in DILATED_GROUPS]))
    qkv_t = _project_qkv(hs, w_in, gains, rope_t, seq)

    n_gate_blocks = 2 * D_MODEL // PROJ_TN
    half_bias = (0.5 * b_gates.astype(F32)).reshape(1, 2 * D_MODEL)
    rest_w_block = lambda j: jnp.where(j < 4, j + 12, jnp.where(j == 4, 9, 11))
    rest, o_all, lse_all = _project_rest_with_attention(
        hs[0], w_in, half_bias, qkv_t, seq, n_gate_blocks=n_gate_blocks, w_block_of=rest_w_block)

    merged = _tail(o_all, lse_all, rest, u,
                   w_branch_attn, w_branch_pool, pool_maps,
                   pool_scale.astype(F32).reshape(1, POOL_WIDTH), batch, seq)
    return _project_out(merged, w_out, x2).reshape(batch, seq, dm)
```

```python
import functools

import numpy as np
import jax
import jax.numpy as jnp
from jax import lax
from jax.experimental import pallas as pl
from jax.experimental.pallas import tpu as pltpu

D_MODEL = 2048
HEAD_DIM = 128
HEADS_PER_GROUP = 8
DILATED_GROUPS = ((128, 1), (512, 4), (2048, 16))
N_ATTN_GROUPS = len(DILATED_GROUPS)
ATTN_OUT = HEADS_PER_GROUP * HEAD_DIM
BLK = 128
ROPE_THETA = 500000.0
ROT_DIM = HEAD_DIM // 4
ROT_HALF = ROT_DIM // 2
POOL_SIZES = (2, 4, 8, 16)
POOL_WIDTH = D_MODEL // 2
POOL_GROUP = POOL_WIDTH // len(POOL_SIZES)
NORM_EPS = 1e-6

LANES = 128
MXU_WIDTH = 256
VMEM_LIMIT_BYTES = 56 * 1024 * 1024

PROJ_TM = 1024
PROJ_TN = 1024
OUT_TM = 512
QKV_TM = 2048
PERM_TM = 256
NORM_TM = 512
ATTN_TB = 256
MERGE_TN = 512
POOL_HALO = 16

F32 = jnp.float32
BF16 = jnp.bfloat16

_NT = (((1,), (1,)), ((), ()))
_TN = (((0,), (0,)), ((), ()))


def _deinterleave_matrix(d):
    n = PERM_TM // d
    i = np.arange(PERM_TM)
    p = np.zeros((PERM_TM, PERM_TM), np.float32)
    p[i, (i % n) * d + i // n] = 1.0
    return p


def _norm_u_kernel(x_ref, g_ref, p4_ref, p16_ref, w_ref, h1_ref, h4_ref, h16_ref, u_ref,
                   hbuf_ref, wbf_ref):
    step = pl.program_id(0)

    @pl.when(step == 0)
    def _():
        wbf_ref[...] = w_ref[...].astype(BF16)
        hbuf_ref[...] = jnp.zeros_like(hbuf_ref)

    def body(h_in_ref, h_out_ref):
        for c in range(u_ref.shape[1] // MXU_WIDTH):
            cs = slice(c * MXU_WIDTH, (c + 1) * MXU_WIDTH)
            u_ref[:, cs] = jnp.dot(h_in_ref[...], wbf_ref[:, cs], preferred_element_type=F32)
        for sb in range(x_ref.shape[0] // PERM_TM):
            rs = slice(sb * PERM_TM, (sb + 1) * PERM_TM)
            x = x_ref[rs, :]
            ms = jnp.mean(x * x, axis=-1, keepdims=True)
            h = (x * lax.rsqrt(ms + NORM_EPS) * g_ref[...]).astype(BF16)
            h1_ref[rs, :] = h
            h_out_ref[rs, :] = h
            for p_ref, out_ref in ((p4_ref, h4_ref), (p16_ref, h16_ref)):
                d = out_ref.shape[0]
                n = PERM_TM // d
                hp = jnp.dot(p_ref[...], h, preferred_element_type=F32).astype(BF16)
                for r in range(d):
                    out_ref[r, sb * n:(sb + 1) * n, :] = hp[r * n:(r + 1) * n, :]

    for parity in range(2):
        pl.when(step % 2 == parity)(
            functools.partial(body, hbuf_ref.at[1 - parity], hbuf_ref.at[parity]))


def _rmsnorm_and_project_u(x2, gain, w_in, batch, seq, *, w_block):
    t, dm = x2.shape
    tm, tn = NORM_TM, PROJ_TN
    tiles = seq // tm
    n_tiles = t // tm
    perms = [jnp.asarray(_deinterleave_matrix(d), BF16) for _, d in DILATED_GROUPS[1:]]
    cur = lambda i: jnp.minimum(i, n_tiles - 1)
    prv = lambda i: jnp.maximum(i - 1, 0)
    out_shape = [jax.ShapeDtypeStruct((t, dm), BF16)]
    out_specs = [pl.BlockSpec((tm, dm), lambda i: (cur(i), 0))]
    for _, d in DILATED_GROUPS[1:]:
        out_shape.append(jax.ShapeDtypeStruct((batch, d, seq // d, dm), BF16))
        out_specs.append(pl.BlockSpec((None, d, tm // d, dm),
                                      lambda i: (cur(i) // tiles, 0, cur(i) % tiles, 0)))
    out_shape.append(jax.ShapeDtypeStruct((t, tn), F32))
    out_specs.append(pl.BlockSpec((tm, tn), lambda i: (prv(i), 0)))
    const = lambda i: (0, 0)
    resident = functools.partial(pl.BlockSpec, pipeline_mode=pl.Buffered(1))
    h1, h4, h16, u = pl.pallas_call(
        _norm_u_kernel,
        grid=(n_tiles + 1,),
        in_specs=[pl.BlockSpec((tm, dm), lambda i: (cur(i), 0)),
                  pl.BlockSpec((1, dm), const),
                  pl.BlockSpec((PERM_TM, PERM_TM), const),
                  pl.BlockSpec((PERM_TM, PERM_TM), const),
                  resident((dm, tn), lambda i: (0, w_block))],
        out_specs=out_specs,
        out_shape=out_shape,
        scratch_shapes=[pltpu.VMEM((2, tm, dm), BF16), pltpu.VMEM((dm, tn), BF16)],
        compiler_params=pltpu.CompilerParams(dimension_semantics=("arbitrary",),
                                             vmem_limit_bytes=VMEM_LIMIT_BYTES),
        name="rmsnorm_proj_u",
    )(x2, gain.reshape(1, dm), *perms, w_in)
    return (h1, h4.reshape(t, dm), h16.reshape(t, dm)), u


def _qkv_kernel(*refs, tm, tn):
    h_refs = refs[:N_ATTN_GROUPS]
    w_ref, gain_ref, rope_ref, out_ref, wt_ref, hbuf_ref, sem_ref = refs[N_ATTN_GROUPS:]
    rows = pl.num_programs(1)
    part = pl.program_id(0) % 3
    step = pl.program_id(0) * rows + pl.program_id(1)
    slot = step % 2

    def tile_copy(g, row_block, into):
        return pltpu.make_async_copy(h_refs[g].at[pl.ds(row_block * tm, tm), :],
                                     hbuf_ref.at[into], sem_ref.at[into])

    def start_fetch(of_step, into):
        for g in range(N_ATTN_GROUPS):
            pl.when(of_step // (3 * rows) == g)(lambda g=g: tile_copy(g, of_step % rows, into).start())

    pl.when(step == 0)(lambda: start_fetch(step, slot))
    pl.when(step + 1 < pl.num_programs(0) * rows)(lambda: start_fetch(step + 1, 1 - slot))
    tile_copy(0, 0, slot).wait()

    @pl.when(pl.program_id(1) == 0)
    def _():
        for c in range(tn // MXU_WIDTH):
            cs = slice(c * MXU_WIDTH, (c + 1) * MXU_WIDTH)
            wt_ref[cs, :] = w_ref[:, cs].T.astype(BF16)

    body = functools.partial(_qkv_body, hbuf_ref.at[slot], gain_ref, rope_ref, out_ref, wt_ref,
                             tm=tm, tn=tn)
    pl.when(part < 2)(functools.partial(body, normed=True))
    pl.when(part == 2)(functools.partial(body, normed=False))


def _qkv_body(h_ref, gain_ref, rope_ref, out_ref, wt_ref, *, tm, tn, normed):
    def chunk(c):
        ts = slice(c * ATTN_TB, (c + 1) * ATTN_TB)
        return lax.dot_general(wt_ref[...], h_ref[ts, :], _NT, preferred_element_type=F32)

    if not normed:
        for c in range(tm // ATTN_TB):
            out_ref[c] = chunk(c).astype(out_ref.dtype)
    else:
        for c in range(tm // ATTN_TB):
            acc = chunk(c)
            for hh in range(tn // HEAD_DIM):
                hs = slice(hh * HEAD_DIM, (hh + 1) * HEAD_DIM)
                for lt in range(ATTN_TB // LANES):
                    ls = slice(lt * LANES, (lt + 1) * LANES)
                    pos = slice(c * ATTN_TB + lt * LANES, c * ATTN_TB + (lt + 1) * LANES)
                    a = acc[hs, ls]
                    ss = jnp.sum(a * a, axis=0, keepdims=True)
                    r = lax.rsqrt(ss * (1.0 / HEAD_DIM) + NORM_EPS)
                    ag = a * gain_ref[...]
                    lo, hi = ag[0:ROT_HALF], ag[ROT_HALF:ROT_DIM]
                    cos, sin = rope_ref[0, :, pos], rope_ref[1, :, pos]
                    y = jnp.concatenate([lo * cos - hi * sin, hi * cos + lo * sin, ag[ROT_DIM:]], axis=0)
                    out_ref[c, hs, ls] = (y * r).astype(out_ref.dtype)


def _project_qkv(hs, w_in, gains, rope_t, seq):
    t, dm = hs[0].shape
    tm, tn = QKV_TM, PROJ_TN
    seq_tiles = seq // tm
    group = lambda gj: gj // 3
    part = lambda gj: gj % 3
    table = pl.BlockSpec((None, 2, ROT_HALF, tm), lambda gj, i: (group(gj), 0, 0, i % seq_tiles))
    return pl.pallas_call(
        functools.partial(_qkv_kernel, tm=tm, tn=tn),
        grid=(3 * N_ATTN_GROUPS, t // tm),
        in_specs=[*(pl.BlockSpec(memory_space=pl.ANY) for _ in range(N_ATTN_GROUPS)),
                  pl.BlockSpec((dm, tn), lambda gj, i: (0, part(gj) * N_ATTN_GROUPS + group(gj))),
                  pl.BlockSpec((None, HEAD_DIM, LANES), lambda gj, i: (jnp.minimum(part(gj), 1), 0, 0)),
                  table],
        out_specs=pl.BlockSpec((None, None, tm // ATTN_TB, tn, ATTN_TB),
                               lambda gj, i: (group(gj), part(gj), i, 0, 0)),
        out_shape=jax.ShapeDtypeStruct((N_ATTN_GROUPS, 3, t // ATTN_TB, tn, ATTN_TB), BF16),
        scratch_shapes=[pltpu.VMEM((tn, dm), BF16), pltpu.VMEM((2, tm, dm), BF16),
                        pltpu.SemaphoreType.DMA((2,))],
        compiler_params=pltpu.CompilerParams(
            dimension_semantics=("arbitrary", "arbitrary"),
            vmem_limit_bytes=VMEM_LIMIT_BYTES),
        name="proj_qkv",
    )(*hs, w_in, gains, rope_t)


def _rope_tables_t(seq, d):
    inv_freq = ROPE_THETA ** (-np.arange(0, ROT_DIM, 2, dtype=np.float64) / ROT_DIM)
    pos = np.arange(seq).reshape(seq // d, d).T.reshape(seq)
    ang = pos.astype(np.float64)[None, :] * inv_freq[:, None]
    return np.cos(ang).astype(np.float32), np.sin(ang).astype(np.float32)


def _gate_epilogue(acc, cs, half_bias_ref):
    return 0.5 + 0.5 * jnp.tanh(0.5 * acc + half_bias_ref[:, cs])


def _silu_epilogue(acc, cs, half_bias_ref):
    del cs, half_bias_ref
    half = 0.5 * acc
    return half + half * jnp.tanh(half)


def _band_bias():
    i = np.arange(BLK)[:, None]
    j = np.arange(2 * BLK)[None, :]
    band = (j >= i) & (j <= i + BLK)
    first = band & (j >= BLK)
    return np.where(np.stack([band, first]), 0.0, -np.inf).astype(np.float32)


class _AttentionBlocks:
    def __init__(self, bias_ref, q_ref, k_ref, v_ref, o_ref, lse_ref, kprev_ref, vprev_ref, firsts):
        self.bias_ref, self.q_ref, self.k_ref, self.v_ref = bias_ref, q_ref, k_ref, v_ref
        self.o_ref, self.lse_ref, self.kprev_ref, self.vprev_ref = o_ref, lse_ref, kprev_ref, vprev_ref
        self.firsts = firsts
        self.per = ATTN_TB // BLK
        self.scores = {}

    def _block(self, ref, n, cs):
        lo = (n % self.per) * BLK
        return ref[n // self.per, cs, lo:lo + BLK]

    def _with_previous(self, ref, prev_ref, n, cs):
        prev = prev_ref[cs, :] if n == 0 else self._block(ref, n - 1, cs)
        return jnp.concatenate([prev, self._block(ref, n, cs)], axis=1)

    def score_phase(self, n):
        first = self.firsts[n]
        bias = self.bias_ref[int(first)] if isinstance(first, bool) else self.bias_ref[jnp.where(first, 1, 0)]
        for h in range(HEADS_PER_GROUP):
            cs = slice(h * HEAD_DIM, (h + 1) * HEAD_DIM)
            kk = self._with_previous(self.k_ref, self.kprev_ref, n, cs)
            s = lax.dot_general(self._block(self.q_ref, n, cs), kk, _TN, preferred_element_type=F32)
            self.scores[n, h] = s + bias

    def value_phase(self, n):
        ts = slice(n * BLK, (n + 1) * BLK)
        lane = lax.broadcasted_iota(jnp.int32, (BLK, LANES), 1)
        m_tile = jnp.zeros((BLK, LANES), F32)
        l_tile = jnp.ones((BLK, LANES), F32)
        for h in range(HEADS_PER_GROUP):
            cs = slice(h * HEAD_DIM, (h + 1) * HEAD_DIM)
            s = self.scores.pop((n, h))
            m = jnp.max(s, axis=-1, keepdims=True)
            p = jnp.exp(s - m)
            l = jnp.sum(p, axis=-1, keepdims=True)
            vv = self._with_previous(self.v_ref, self.vprev_ref, n, cs)
            o = lax.dot_general(p.astype(BF16), vv, _NT, preferred_element_type=F32)
            self.o_ref[ts, cs] = (o / l).astype(self.o_ref.dtype)
            m_tile = jnp.where(lane == h, m, m_tile)
            l_tile = jnp.where(lane == h, l, l_tile)
        self.lse_ref[ts, :] = m_tile + jnp.log(l_tile)

    def carry(self, n_blocks):
        self.kprev_ref[...] = self._block(self.k_ref, n_blocks - 1, slice(None))
        self.vprev_ref[...] = self._block(self.v_ref, n_blocks - 1, slice(None))


def _rest_attn_kernel(h_ref, w_ref, half_bias_ref, bias_ref, qkv_ref,
                      out_ref, o_ref, lse_ref, wbf_ref, kprev_ref, vprev_ref,
                      *, tn, n_gate_blocks, steps_per_group, tiles_per_sub):
    j, i = pl.program_id(0), pl.program_id(1)
    step = j * pl.num_programs(1) + i

    @pl.when(i == 0)
    def _():
        wbf_ref[...] = w_ref[...].astype(BF16)

    @pl.when(step == 0)
    def _():
        kprev_ref[...] = jnp.zeros_like(kprev_ref)
        vprev_ref[...] = jnp.zeros_like(vprev_ref)

    n_blocks = tn // MXU_WIDTH
    per = ATTN_TB // BLK
    tiles_per_step = n_blocks // per
    group = step // steps_per_group
    tps = functools.reduce(lambda acc, g: jnp.where(group == g, tiles_per_sub[g], acc),
                           range(len(tiles_per_sub)), tiles_per_sub[0])
    tile0 = (step % steps_per_group) * tiles_per_step
    firsts = [((tile0 + n // per) & (tps - 1)) == 0 if n % per == 0 else False for n in range(n_blocks)]

    def body(epilogue):
        attn = _AttentionBlocks(bias_ref, qkv_ref.at[0], qkv_ref.at[1], qkv_ref.at[2], o_ref, lse_ref,
                                kprev_ref, vprev_ref, firsts)
        for c in range(n_blocks):
            cs = slice(c * MXU_WIDTH, (c + 1) * MXU_WIDTH)
            attn.score_phase(c)
            acc = jnp.dot(h_ref[...], wbf_ref[:, cs], preferred_element_type=F32)
            if c > 0:
                attn.value_phase(c - 1)
            out_ref[:, cs] = epilogue(acc, cs, half_bias_ref).astype(out_ref.dtype)
        attn.value_phase(n_blocks - 1)
        attn.carry(n_blocks)

    pl.when(j < n_gate_blocks)(functools.partial(body, _gate_epilogue))
    pl.when(j >= n_gate_blocks)(functools.partial(body, _silu_epilogue))


def _project_rest_with_attention(h, w_in, half_bias, qkv_t, seq, *, n_gate_blocks, w_block_of):
    t, dm = h.shape
    tm, tn = PROJ_TM, PROJ_TN
    n_blocks = n_gate_blocks + 2
    rows = t // tm
    n_groups, _, n_tiles = qkv_t.shape[:3]
    blocks_per_step = tn // MXU_WIDTH
    tiles_per_step = blocks_per_step * BLK // ATTN_TB
    steps_per_group = n_tiles // tiles_per_step
    assert n_groups * steps_per_group == n_blocks * rows, "attention tiles must fill the projection steps"
    tiles_per_sub = tuple(seq // d // ATTN_TB for _, d in DILATED_GROUPS)
    step = lambda j, i: j * rows + i
    tiles = pl.BlockSpec(
        (None, 3, tiles_per_step, ATTN_OUT, ATTN_TB),
        lambda j, i: (step(j, i) // steps_per_group, 0, step(j, i) % steps_per_group, 0, 0))
    tokens = tiles_per_step * ATTN_TB
    per_group = lambda width: pl.BlockSpec(
        (None, tokens, width), lambda j, i: (step(j, i) // steps_per_group, step(j, i) % steps_per_group, 0))
    return pl.pallas_call(
        functools.partial(_rest_attn_kernel, tn=tn, n_gate_blocks=n_gate_blocks,
                          steps_per_group=steps_per_group, tiles_per_sub=tiles_per_sub),
        grid=(n_blocks, rows),
        in_specs=[pl.BlockSpec((tm, dm), lambda j, i: (i, 0)),
                  pl.BlockSpec((dm, tn), lambda j, i: (0, w_block_of(j))),
                  pl.BlockSpec((1, tn), lambda j, i: (0, jnp.minimum(j, n_gate_blocks - 1))),
                  pl.BlockSpec((2, BLK, 2 * BLK), lambda j, i: (0, 0, 0), pipeline_mode=pl.Buffered(1)),
                  tiles],
        out_specs=[pl.BlockSpec((tm, tn), lambda j, i: (i, j)),
                   per_group(ATTN_OUT), per_group(LANES)],
        out_shape=[jax.ShapeDtypeStruct((t, n_blocks * tn), BF16),
                   jax.ShapeDtypeStruct((n_groups, t, ATTN_OUT), BF16),
                   jax.ShapeDtypeStruct((n_groups, t, LANES), F32)],
        scratch_shapes=[pltpu.VMEM((dm, tn), BF16),
                        pltpu.VMEM((ATTN_OUT, BLK), BF16), pltpu.VMEM((ATTN_OUT, BLK), BF16)],
        compiler_params=pltpu.CompilerParams(
            dimension_semantics=("arbitrary", "arbitrary"),
            vmem_limit_bytes=VMEM_LIMIT_BYTES),
        name="proj_rest_attention",
    )(h, w_in, half_bias, jnp.asarray(_band_bias()), qkv_t)


def _rows(ref):
    return jnp.concatenate([ref[r] for r in range(ref.shape[0])], axis=0)


def _unpermute_f32(q, x):
    hi = x.astype(BF16).astype(F32)
    r1 = x - hi
    mid = r1.astype(BF16).astype(F32)
    lo = r1 - mid
    packed = hi + pltpu.roll(mid, HEADS_PER_GROUP, 1) + pltpu.roll(lo, 2 * HEADS_PER_GROUP, 1)
    y = jnp.dot(q, packed.astype(BF16), preferred_element_type=F32)
    return (y + pltpu.roll(y, LANES - HEADS_PER_GROUP, 1)) + pltpu.roll(y, LANES - 2 * HEADS_PER_GROUP, 1)


def _tail_kernel(o0_ref, o1_ref, o2_ref, l0_ref, l1_ref, l2_ref, unperm_ref, gates_ref, z_ref,
                 u_ref, uh_ref, wba32_ref, wbp32_ref, pm32_ref, ps_ref,
                 out_ref, a_ref, b_ref, wba_ref, wbp_ref, pm_ref, *, tm, seq, n_tiles):
    step = pl.program_id(0)

    @pl.when(step == 0)
    def _():
        a_ref[...] = jnp.zeros_like(a_ref)
        b_ref[...] = jnp.zeros_like(b_ref)
        wba_ref[...] = wba32_ref[...].astype(BF16)
        wbp_ref[...] = wbp32_ref[...].astype(BF16)
        pm_ref[...] = pm32_ref[...].astype(BF16)

    body = functools.partial(
        _tail_step, o0_ref, o1_ref, o2_ref, l0_ref, l1_ref, l2_ref, unperm_ref, gates_ref, z_ref,
        u_ref, uh_ref, wba_ref, wbp_ref, pm_ref, ps_ref, out_ref,
        pos0=(jnp.minimum(step, n_tiles - 1) * tm) % seq, tm=tm)
    for parity in range(2):
        pl.when(step % 2 == parity)(
            functools.partial(body, a_ref.at[1 - parity], b_ref.at[1 - parity],
                              a_ref.at[parity], b_ref.at[parity]))


def _tail_step(o0_ref, o1_ref, o2_ref, l0_ref, l1_ref, l2_ref, unperm_ref, gates_ref, z_ref,
               u_ref, uh_ref, wba_ref, wbp_ref, pm_ref, ps_ref, out_ref,
               a_in_ref, b_in_ref, a_ref, b_ref, *, pos0, tm):
    q4, q16 = unperm_ref[0], unperm_ref[1]
    o0 = o0_ref[...]
    o1 = jnp.dot(q4, _rows(o1_ref), preferred_element_type=F32)
    o2 = jnp.dot(q16, _rows(o2_ref), preferred_element_type=F32)
    l0 = l0_ref[...]
    l1 = _unpermute_f32(q4, _rows(l1_ref))
    l2 = _unpermute_f32(q16, _rows(l2_ref))

    def merge_chunk(c):
        cs = slice(c * MERGE_TN, (c + 1) * MERGE_TN)
        y_attn = jnp.dot(a_in_ref[...], wba_ref[:, cs], preferred_element_type=F32)
        y_pool = jnp.dot(b_in_ref[...], wbp_ref[:, cs], preferred_element_type=F32)
        gate_attn = gates_ref[:, cs].astype(F32)
        gate_pool = gates_ref[:, D_MODEL + c * MERGE_TN:D_MODEL + (c + 1) * MERGE_TN].astype(F32)
        merged = gate_attn * y_attn + gate_pool * y_pool
        out_ref[:, cs] = merged.astype(out_ref.dtype)

    halo = uh_ref[...]
    halo = jnp.where(pos0 == 0, jnp.zeros_like(halo), halo)
    pos = pos0 + lax.broadcasted_iota(jnp.int32, (tm, 1), 0)
    pooled = []
    for g, ksz in enumerate(POOL_SIZES):
        cs = slice(g * POOL_GROUP, (g + 1) * POOL_GROUP)
        ext = jnp.concatenate([halo[:, cs], u_ref[:, cs]], axis=0)
        win, shift = ext, 1
        while shift < ksz:
            win = win + pltpu.roll(win, shift, 0)
            shift *= 2
        u = ext[POOL_HALO:]
        cnt = jnp.minimum(pos + 1, ksz).astype(F32)
        dlt = win[POOL_HALO:] / cnt - u
        pooled.append(jnp.dot(dlt.astype(BF16), pm_ref[g], preferred_element_type=F32))
    pool = jnp.concatenate(pooled, axis=1) * ps_ref[...]
    b_ref[...] = (pool * z_ref[:, ATTN_OUT:].astype(F32)).astype(BF16)

    for c in range(D_MODEL // MERGE_TN):
        merge_chunk(c)

    mx = jnp.maximum(jnp.maximum(l0, l1), l2)
    e0, e1, e2 = jnp.exp(l0 - mx), jnp.exp(l1 - mx), jnp.exp(l2 - mx)
    inv = 1.0 / (e0 + e1 + e2)
    w0, w1, w2 = e0 * inv, e1 * inv, e2 * inv
    for h in range(HEADS_PER_GROUP):
        cs = slice(h * HEAD_DIM, (h + 1) * HEAD_DIM)
        a = (w0[:, h:h + 1] * o0[:, cs].astype(F32)
             + w1[:, h:h + 1] * o1[:, cs]
             + w2[:, h:h + 1] * o2[:, cs])
        a_ref[:, cs] = (a * z_ref[:, cs].astype(F32)).astype(BF16)


def _out_kernel(h_hbm, w_hbm, x_hbm, out_hbm, wstage_ref, wbf_ref, sem_ref, *, tm):
    t, d = h_hbm.shape
    dn = out_hbm.shape[1]
    copy = pltpu.make_async_copy(w_hbm, wstage_ref, sem_ref.at[0])
    copy.start()
    copy.wait()
    wbf_ref[...] = wstage_ref[...].astype(BF16)

    def step(h_ref, x_ref, out_ref):
        for c in range(dn // MXU_WIDTH):
            cs = slice(c * MXU_WIDTH, (c + 1) * MXU_WIDTH)
            acc = jnp.dot(h_ref[...], wbf_ref[:, cs], preferred_element_type=F32)
            out_ref[:, cs] = acc + x_ref[:, cs]

    streamed = lambda width: pl.BlockSpec((tm, width), lambda i: (i, 0), pipeline_mode=pl.Buffered(3))
    pltpu.emit_pipeline(
        step, grid=(t // tm,),
        in_specs=[streamed(d), streamed(dn)],
        out_specs=[pl.BlockSpec((tm, dn), lambda i: (i, 0))],
    )(h_hbm, x_hbm, out_hbm)


def _project_out(h, w, x2):
    t, d = h.shape
    dn = w.shape[1]
    anywhere = pl.BlockSpec(memory_space=pl.ANY)
    return pl.pallas_call(
        functools.partial(_out_kernel, tm=OUT_TM),
        in_specs=[anywhere, anywhere, anywhere],
        out_specs=anywhere,
        out_shape=jax.ShapeDtypeStruct((t, dn), F32),
        scratch_shapes=[pltpu.VMEM((d, dn), F32), pltpu.VMEM((d, dn), BF16),
                        pltpu.SemaphoreType.DMA((1,))],
        compiler_params=pltpu.CompilerParams(vmem_limit_bytes=VMEM_LIMIT_BYTES),
        name="proj_out",
    )(h, w, x2)


def _tail(os_, lses, rest, u, wba, wbp, pm, ps, batch, seq):
    t, dm = u.shape[0], D_MODEL
    tm = PERM_TM
    tiles = seq // tm
    n_tiles = t // tm
    halo_per_tile = tm // POOL_HALO
    resident = functools.partial(pl.BlockSpec, pipeline_mode=pl.Buffered(1))
    cur = lambda i: jnp.minimum(i, n_tiles - 1)
    prv = lambda i: jnp.maximum(i - 1, 0)
    row = lambda blk: (lambda i: (cur(i), blk))
    row_prv = lambda blk: (lambda i: (prv(i), blk))
    unperms = jnp.asarray(np.stack([_deinterleave_matrix(d).T for _, d in DILATED_GROUPS[1:]]), BF16)
    o_in, l_in, o_specs, l_specs = [], [], [], []
    for g, (_, d) in enumerate(DILATED_GROUPS):
        for arr, width, ins, specs in ((os_, ATTN_OUT, o_in, o_specs), (lses, LANES, l_in, l_specs)):
            if d == 1:
                ins.append(arr)
                specs.append(pl.BlockSpec((None, tm, width), lambda i, g=g: (g, cur(i), 0)))
            else:
                ins.append(arr.reshape(arr.shape[0], batch, d, seq // d, width))
                specs.append(pl.BlockSpec((None, None, d, tm // d, width),
                                          lambda i, g=g: (g, cur(i) // tiles, 0, cur(i) % tiles, 0)))
    in_specs = (
        o_specs + l_specs
        + [resident(unperms.shape, lambda i: (0, 0, 0)),
           pl.BlockSpec((tm, 2 * D_MODEL), row_prv(0)),
           pl.BlockSpec((tm, ATTN_OUT + POOL_WIDTH), row(2 * D_MODEL // (ATTN_OUT + POOL_WIDTH))),
           pl.BlockSpec((tm, POOL_WIDTH), row(0)),
           pl.BlockSpec((POOL_HALO, POOL_WIDTH),
                        lambda i: (jnp.maximum(cur(i) * halo_per_tile - 1, 0), 0)),
           resident(wba.shape, lambda i: (0, 0)),
           resident(wbp.shape, lambda i: (0, 0)),
           resident(pm.shape, lambda i: (0, 0, 0)),
           resident(ps.shape, lambda i: (0, 0))])
    return pl.pallas_call(
        functools.partial(_tail_kernel, tm=tm, seq=seq, n_tiles=n_tiles),
        grid=(n_tiles + 1,),
        in_specs=in_specs,
        out_specs=pl.BlockSpec((tm, dm), row_prv(0)),
        out_shape=jax.ShapeDtypeStruct((t, dm), BF16),
        scratch_shapes=[pltpu.VMEM((2, tm, ATTN_OUT), BF16), pltpu.VMEM((2, tm, POOL_WIDTH), BF16),
                        pltpu.VMEM(wba.shape, BF16), pltpu.VMEM(wbp.shape, BF16), pltpu.VMEM(pm.shape, BF16)],
        compiler_params=pltpu.CompilerParams(
            dimension_semantics=("arbitrary",),
            vmem_limit_bytes=VMEM_LIMIT_BYTES),
        name="merge_pool",
    )(*o_in, *l_in, unperms, rest, rest, u, u, wba, wbp, pm, ps)


def kernel(x, norm_gain, w_in, b_gates, q_norm_gain, k_norm_gain, pool_maps, pool_scale,
           w_branch_attn, w_branch_pool, w_out):
    batch, seq, dm = x.shape
    t = batch * seq
    x2 = x.reshape(t, dm)
    hs, u = _rmsnorm_and_project_u(x2, norm_gain, w_in, batch, seq, w_block=10)

    lane_rep = lambda g: jnp.broadcast_to(g.astype(F32)[:, None], (HEAD_DIM, LANES))
    gains = jnp.stack([lane_rep(q_norm_gain * HEAD_DIM ** -0.5), lane_rep(k_norm_gain)])
    rope_t = jnp.asarray(np.stack([np.stack(_rope_tables_t(seq, d)) for _, d in DILATED_GROUPS]))
    qkv_t = _project_qkv(hs, w_in, gains, rope_t, seq)

    n_gate_blocks = 2 * D_MODEL // PROJ_TN
    half_bias = (0.5 * b_gates.astype(F32)).reshape(1, 2 * D_MODEL)
    rest_w_block = lambda j: jnp.where(j < 4, j + 12, jnp.where(j == 4, 9, 11))
    rest, o_all, lse_all = _project_rest_with_attention(
        hs[0], w_in, half_bias, qkv_t, seq, n_gate_blocks=n_gate_blocks, w_block_of=rest_w_block)

    merged = _tail(o_all, lse_all, rest, u,
                   w_branch_attn, w_branch_pool, pool_maps,
                   pool_scale.astype(F32).reshape(1, POOL_WIDTH), batch, seq)
    return _project_out(merged, w_out, x2).reshape(batch, seq, dm)
```
